```python
import math
import jax, jax.numpy as jnp
from jax import lax
import numpy as np

D_MODEL = 1024
BATCH = 2
SEQ = 8192
DEPTH = 1

N_HEADS = 8
HEAD_DIM = 64
ATTN_WIDTH = N_HEADS * HEAD_DIM
MOBA_BLOCK = 256
MOBA_TOPK = 3
Q_CHUNK = 64
NUM_BUCKETS = 32
MAX_DISTANCE = 128
POOL_WINDOWS = (2, 4, 8, 16)
POOL_GROUPS = 4
POOL_GROUP_DIM = 128
POOL_WIDTH = POOL_GROUPS * POOL_GROUP_DIM
N_BRANCHES = 2
IN_COLS = 3 * ATTN_WIDTH + POOL_WIDTH + N_BRANCHES * D_MODEL
N_EXPERTS = 32
TOP_K = 4
D_EXPERT = D_MODEL
SWIGLU_LIMIT = 7.0
SWIGLU_ALPHA = 1.702
EXPERT_BLOCK = 128
PLE_DIM = 256
RMS_EPS = 1e-6

kernel_name = "hybrid_moba_pool_moe_block"


def rms_norm(x, g):
    xf = x.astype(jnp.float32)
    y = xf * lax.rsqrt(jnp.mean(xf * xf, axis=-1, keepdims=True) + RMS_EPS)
    return (y * g.astype(jnp.float32)).astype(x.dtype)


def t5_causal_bucket(rel):
    n = jnp.maximum(rel, 0)
    max_exact = NUM_BUCKETS // 2
    nf = jnp.maximum(n, 1).astype(jnp.float32)
    large = max_exact + (jnp.log(nf / max_exact) / math.log(MAX_DISTANCE / max_exact)
                         * (NUM_BUCKETS - max_exact)).astype(jnp.int32)
    large = jnp.minimum(large, NUM_BUCKETS - 1)
    return jnp.where(n < max_exact, n, large)


def moba_attention(q, k, v, rel_bias):
    b, h, s, dh = q.shape
    nb = -(-s // MOBA_BLOCK)
    s_pad = nb * MOBA_BLOCK
    k_eff = min(MOBA_TOPK, nb)
    pad = ((0, 0), (0, 0), (0, s_pad - s), (0, 0))
    k = jnp.pad(k, pad)
    v = jnp.pad(v, pad)
    k_blocks = k.reshape(b, h, nb, MOBA_BLOCK, dh)
    v_blocks = v.reshape(b, h, nb, MOBA_BLOCK, dh)
    k_mean = jnp.mean(k_blocks.astype(jnp.float32), axis=3)
    bias_t = rel_bias.astype(jnp.float32).T
    b_idx = jnp.arange(b)[:, None, None, None]
    h_idx = jnp.arange(h)[None, :, None, None]
    scale = dh ** -0.5
    neg = jnp.finfo(jnp.float32).min
    in_block = jnp.arange(MOBA_BLOCK)
    n_sel = k_eff * MOBA_BLOCK

    def chunk(c):
        q0 = c * Q_CHUNK
        j = q0 // MOBA_BLOCK
        qc = lax.dynamic_slice_in_dim(q, q0, Q_CHUNK, axis=2).astype(jnp.float32) * scale
        qpos = q0 + jnp.arange(Q_CHUNK)
        gate = jnp.einsum('bhqd,bhnd->bhqn', qc, k_mean)
        gate = jnp.where(jnp.arange(nb) < j, gate, neg)
        _, sel = lax.top_k(gate, k_eff)
        valid = jnp.arange(k_eff) < j
        k_sel = k_blocks[b_idx, h_idx, sel].astype(jnp.float32)
        v_sel = v_blocks[b_idx, h_idx, sel].astype(jnp.float32)
        s_sel = jnp.einsum('bhqd,bhqnkd->bhqnk', qc, k_sel)
        rel_sel = qpos[:, None, None] - (sel[..., None] * MOBA_BLOCK + in_block)
        s_sel = s_sel + bias_t[h_idx[..., None], t5_causal_bucket(rel_sel)]
        s_sel = jnp.where(valid[:, None], s_sel, neg)
        k_own = lax.dynamic_slice_in_dim(k, j * MOBA_BLOCK, MOBA_BLOCK, axis=2).astype(jnp.float32)
        v_own = lax.dynamic_slice_in_dim(v, j * MOBA_BLOCK, MOBA_BLOCK, axis=2).astype(jnp.float32)
        rel_own = qpos[:, None] - (j * MOBA_BLOCK + in_block)[None, :]
        s_own = jnp.einsum('bhqd,bhkd->bhqk', qc, k_own) + bias_t[:, t5_causal_bucket(rel_own)]
        s_own = jnp.where(rel_own >= 0, s_own, neg)
        logits = jnp.concatenate([s_sel.reshape(b, h, Q_CHUNK, n_sel), s_own], axis=-1)
        probs = jax.nn.softmax(logits, axis=-1)
        p_sel = probs[..., :n_sel].reshape(b, h, Q_CHUNK, k_eff, MOBA_BLOCK)
        p_own = probs[..., n_sel:]
        return (jnp.einsum('bhqnk,bhqnkd->bhqd', p_sel, v_sel)
                + jnp.einsum('bhqk,bhkd->bhqd', p_own, v_own))

    out = lax.map(chunk, jnp.arange(s // Q_CHUNK))
    out = out.transpose(1, 0, 3, 2, 4).reshape(b, s, h * dh)
    return out.astype(q.dtype)


def causal_multiscale_pool(u):
    b, s, _ = u.shape
    ug = u.astype(jnp.float32).reshape(b, s, POOL_GROUPS, POOL_GROUP_DIM)
    csum = jnp.concatenate([jnp.zeros((b, 1, POOL_GROUPS, POOL_GROUP_DIM), jnp.float32),
                            jnp.cumsum(ug, axis=1)], axis=1)
    t = jnp.arange(s)
    w = jnp.array(POOL_WINDOWS, dtype=jnp.int32)
    lo = jnp.maximum(t[:, None] + 1 - w[None, :], 0)
    cnt = jnp.minimum(t[:, None] + 1, w[None, :]).astype(jnp.float32)
    g_idx = jnp.arange(POOL_GROUPS)[None, :]
    window_sum = csum[:, 1:] - csum[:, lo, g_idx]
    return window_sum / cnt[None, :, :, None] - ug


def moe_ffn(hf, router_w, router_b, w_gate_up, b_gate_up, w_down, b_down):
    t, d = hf.shape
    logits = (hf @ router_w + router_b).astype(jnp.float32)
    top_vals, top_idx = lax.top_k(logits, TOP_K)
    gates = jax.nn.softmax(top_vals, axis=-1).astype(hf.dtype)
    n_assign = t * TOP_K
    flat_e = top_idx.reshape(-1)
    flat_tok = jnp.arange(n_assign, dtype=jnp.int32) // TOP_K
    flat_w = gates.reshape(-1)
    order = jnp.argsort(flat_e)
    sorted_e = flat_e[order]
    counts = jnp.bincount(flat_e, length=N_EXPERTS)
    starts = jnp.cumsum(counts) - counts
    padded = (counts + EXPERT_BLOCK - 1) // EXPERT_BLOCK * EXPERT_BLOCK
    pend = jnp.cumsum(padded)
    pstart = pend - padded
    dest = pstart[sorted_e] + jnp.arange(n_assign) - starts[sorted_e]
    n_pad = n_assign + N_EXPERTS * EXPERT_BLOCK
    n_blk = n_pad // EXPERT_BLOCK
    row_tok = jnp.zeros((n_pad,), jnp.int32).at[dest].set(flat_tok[order])
    row_w = jnp.zeros((n_pad,), hf.dtype).at[dest].set(flat_w[order])
    blk_expert = jnp.minimum(
        jnp.searchsorted(pend, jnp.arange(n_blk) * EXPERT_BLOCK, side='right'), N_EXPERTS - 1)

    def run_block(args):
        tok, wt, e = args
        xb = hf[tok]
        gu = xb @ w_gate_up[e] + b_gate_up[e]
        glu_in = jnp.minimum(gu[:, 0::2], SWIGLU_LIMIT)
        lin = jnp.clip(gu[:, 1::2], -SWIGLU_LIMIT, SWIGLU_LIMIT)
        hid = (lin + 1) * (glu_in * jax.nn.sigmoid(SWIGLU_ALPHA * glu_in))
        y = hid @ w_down[e] + b_down[e]
        return y * wt[:, None]

    ys = lax.map(run_block, (row_tok.reshape(n_blk, EXPERT_BLOCK),
                             row_w.reshape(n_blk, EXPERT_BLOCK), blk_expert))
    return jax.ops.segment_sum(ys.reshape(n_pad, d), row_tok, num_segments=t)


def setup_inputs(seed: int = 0) -> dict:
    key = jax.random.key(seed)
    ks = jax.random.split(key, 24)

    def nrm(k, shape, scale):
        return jax.random.normal(k, shape, jnp.float32) * scale

    D, E, F = D_MODEL, N_EXPERTS, D_EXPERT
    return {
        "x": nrm(ks[0], (BATCH, SEQ, D), 1.0),
        "p": nrm(ks[1], (DEPTH, BATCH, SEQ, PLE_DIM), 1.0),
        "rel_bias": nrm(ks[2], (NUM_BUCKETS, N_HEADS), 0.5),
        "norm_mix_g": 1.0 + nrm(ks[3], (DEPTH, D), 0.05),
        "w_in": nrm(ks[4], (DEPTH, D, IN_COLS), D ** -0.5),
        "pool_w": nrm(ks[5], (DEPTH, POOL_GROUPS, POOL_GROUP_DIM, POOL_GROUP_DIM), POOL_GROUP_DIM ** -0.5),
        "pool_scale": 1.0 + nrm(ks[6], (DEPTH, POOL_WIDTH), 0.1),
        "w_o_attn": nrm(ks[7], (DEPTH, ATTN_WIDTH, D), ATTN_WIDTH ** -0.5),
        "w_o_pool": nrm(ks[8], (DEPTH, POOL_WIDTH, D), POOL_WIDTH ** -0.5),
        "w_out": nrm(ks[9], (DEPTH, D, D), D ** -0.5),
        "norm_ffn_g": 1.0 + nrm(ks[10], (DEPTH, D), 0.05),
        "router_w": nrm(ks[11], (DEPTH, D, E), D ** -0.5),
        "router_b": nrm(ks[12], (DEPTH, E), 0.01),
        "w_gate_up": nrm(ks[13], (DEPTH, E, D, 2 * F), D ** -0.5),
        "b_gate_up": nrm(ks[14], (DEPTH, E, 2 * F), 0.02),
        "w_down": nrm(ks[15], (DEPTH, E, F, D), F ** -0.5),
        "b_down": nrm(ks[16], (DEPTH, E, D), 0.02),
        "norm_ple_g": 1.0 + nrm(ks[17], (DEPTH, D), 0.05),
        "w_ple_gate": nrm(ks[18], (DEPTH, D, D), D ** -0.5),
        "w_ple_proj": nrm(ks[19], (DEPTH, PLE_DIM, D), PLE_DIM ** -0.5),
        "norm_final_g": 1.0 + nrm(ks[20], (D,), 0.05),
    }


def reference(x, p, rel_bias, norm_mix_g, w_in, pool_w, pool_scale, w_o_attn, w_o_pool, w_out,
              norm_ffn_g, router_w, router_b, w_gate_up, b_gate_up, w_down, b_down,
              norm_ple_g, w_ple_gate, w_ple_proj, norm_final_g):
    b, s, d = x.shape
    cuts = [ATTN_WIDTH, 2 * ATTN_WIDTH, 3 * ATTN_WIDTH, 3 * ATTN_WIDTH + POOL_WIDTH]

    def heads(t):
        return t.reshape(b, s, N_HEADS, HEAD_DIM).transpose(0, 2, 1, 3)

    for i in range(DEPTH):
        h = rms_norm(x, norm_mix_g[i])
        q, k, v, u, g = jnp.split(h @ w_in[i], cuts, axis=-1)
        attn = moba_attention(heads(q), heads(k), heads(v), rel_bias)
        pooled = jnp.einsum('bsgc,gcd->bsgd', causal_multiscale_pool(u),
                            pool_w[i].astype(jnp.float32)).reshape(b, s, POOL_WIDTH)
        pooled = (pooled * pool_scale[i].astype(jnp.float32)).astype(x.dtype)
        gates = jax.nn.sigmoid(g.reshape(b, s, N_BRANCHES, d))
        merged = gates[:, :, 0] * (attn @ w_o_attn[i]) + gates[:, :, 1] * (pooled @ w_o_pool[i])
        x = x + merged @ w_out[i]
        hf = rms_norm(x, norm_ffn_g[i]).reshape(b * s, d)
        x = x + moe_ffn(hf, router_w[i], router_b[i], w_gate_up[i], b_gate_up[i],
                        w_down[i], b_down[i]).reshape(b, s, d)
        hp = rms_norm(x, norm_ple_g[i])
        x = x + jax.nn.sigmoid(hp @ w_ple_gate[i]) * (p[i] @ w_ple_proj[i])
    return rms_norm(x, norm_final_g)
```

```python
import functools
import math

import jax
import jax.numpy as jnp
from jax import lax
from jax.experimental import pallas as pl
from jax.experimental.pallas import tpu as pltpu

F32 = jnp.float32
BF16 = jnp.bfloat16
I32 = jnp.int32

D_MODEL = 1024
N_HEADS = 8
HEAD_DIM = 64
ATTN_WIDTH = N_HEADS * HEAD_DIM
MOBA_BLOCK = 256
MOBA_TOPK = 3
NUM_BUCKETS = 32
MAX_DISTANCE = 128
POOL_WINDOWS = (2, 4, 8, 16)
POOL_GROUPS = 4
POOL_GROUP_DIM = 128
POOL_WIDTH = POOL_GROUPS * POOL_GROUP_DIM
N_EXPERTS = 32
TOP_K = 4
SWIGLU_LIMIT = 7.0
SWIGLU_ALPHA = 1.702
RMS_EPS = 1e-6

LANES = 128
SUBLANES = 8
ROW_TILES = D_MODEL // LANES

IN_TM = 512
MIX_TM = 256
EXPERT_TM = 256
OUT_TM = 256
DISPATCH_CHUNK = 64
POOL_HALO = 16
MASKED = -1e30

VMEM_LIMIT = 56 * 1024 * 1024

_NT = (((1,), (1,)), ((), ()))


def _rms(x, g):
    ms = jnp.mean(x * x, axis=-1, keepdims=True)
    return x * lax.rsqrt(ms + RMS_EPS) * g


def _sigmoid(x):
    return 1.0 / (1.0 + jnp.exp(-x))


def _in_proj_kernel(x_ref, g_ref, w_ref, q_ref, k_ref, v_ref, u_ref, gl_ref):
    h = _rms(x_ref[...], g_ref[...]).astype(BF16)
    aw = ATTN_WIDTH

    def proj(lo, hi):
        return jnp.dot(h, w_ref[:, lo:hi], preferred_element_type=F32)

    q_ref[...] = proj(0, aw).astype(BF16)
    k_ref[...] = proj(aw, 2 * aw).astype(BF16)
    v_ref[...] = proj(2 * aw, 3 * aw).astype(BF16)
    u_ref[...] = proj(3 * aw, 3 * aw + POOL_WIDTH)
    base = 3 * aw + POOL_WIDTH
    for c in range(2):
        lo = base + c * D_MODEL
        gl_ref[:, c * D_MODEL:(c + 1) * D_MODEL] = proj(lo, lo + D_MODEL).astype(BF16)


def _in_proj(x2, g, w_bf16):
    t = x2.shape[0]
    in_cols = w_bf16.shape[1]
    row = lambda i: (i, 0)
    fixed = lambda i: (0, 0)
    return pl.pallas_call(
        _in_proj_kernel,
        grid=(t // IN_TM,),
        in_specs=[
            pl.BlockSpec((IN_TM, D_MODEL), row),
            pl.BlockSpec((1, D_MODEL), fixed),
            pl.BlockSpec((D_MODEL, in_cols), fixed),
        ],
        out_specs=[
            pl.BlockSpec((IN_TM, ATTN_WIDTH), row),
            pl.BlockSpec((IN_TM, ATTN_WIDTH), row),
            pl.BlockSpec((IN_TM, ATTN_WIDTH), row),
            pl.BlockSpec((IN_TM, POOL_WIDTH), row),
            pl.BlockSpec((IN_TM, 2 * D_MODEL), row),
        ],
        out_shape=[
            jax.ShapeDtypeStruct((t, ATTN_WIDTH), BF16),
            jax.ShapeDtypeStruct((t, ATTN_WIDTH), BF16),
            jax.ShapeDtypeStruct((t, ATTN_WIDTH), BF16),
            jax.ShapeDtypeStruct((t, POOL_WIDTH), F32),
            jax.ShapeDtypeStruct((t, 2 * D_MODEL), BF16),
        ],
        compiler_params=pltpu.CompilerParams(
            dimension_semantics=("parallel",), vmem_limit_bytes=VMEM_LIMIT),
        name="in_proj",
    )(x2, g, w_bf16)


def _attn_kernel(bfar_ref, q_ref, k_ref, vt_ref, bown_ref, bprev_ref, o_ref,
                 kmean_s, am_s, m_s, l_s, acc_s, *, n_blocks):
    bh = pl.program_id(0)
    j = pl.program_id(1)
    blk = MOBA_BLOCK

    @pl.when(j == 0)
    def _():
        for n in range(n_blocks):
            kb = k_ref[0, n * blk:(n + 1) * blk, :].astype(F32)
            kmean_s[n:n + 1, :] = jnp.mean(kb, axis=0, keepdims=True)

    q = q_ref[0] * jnp.asarray(HEAD_DIM ** -0.5, BF16)

    gate = lax.dot_general(kmean_s[...], q.astype(F32), _NT,
                           precision=lax.Precision.HIGHEST, preferred_element_type=F32)
    n_iota = lax.broadcasted_iota(I32, gate.shape, 0)
    past = n_iota < j
    gate = jnp.where(past, gate, jnp.finfo(F32).min)
    sel = jnp.zeros(gate.shape, jnp.bool_)
    for _ in range(MOBA_TOPK):
        mx = jnp.max(gate, axis=0, keepdims=True)
        idx = jnp.min(jnp.where(gate == mx, n_iota, n_blocks), axis=0, keepdims=True)
        pick = n_iota == idx
        sel = jnp.logical_or(sel, pick)
        gate = jnp.where(pick, -jnp.inf, gate)
    sel = jnp.logical_and(sel, past)
    bfar = bfar_ref[bh % N_HEADS]
    am_s[...] = jnp.where(sel, jnp.where(n_iota == j - 1, 0.0, bfar), MASKED)

    def scores(n):
        kb = k_ref[0, pl.ds(pl.multiple_of(n * blk, blk), blk), :]
        return lax.dot_general(kb, q, _NT, preferred_element_type=F32)

    def pv(n, p):
        vt = vt_ref[0, :, pl.ds(pl.multiple_of(n * blk, blk), blk)]
        return jnp.dot(vt, p.astype(BF16), preferred_element_type=F32)

    s = scores(j) + bown_ref[0]
    m = jnp.max(s, axis=0, keepdims=True)
    p = jnp.exp(s - m)
    m_s[...] = m
    l_s[...] = jnp.sum(p, axis=0, keepdims=True)
    acc_s[...] = pv(j, p)

    def accumulate(n, s):
        m_old = m_s[...]
        m_new = jnp.maximum(m_old, jnp.max(s, axis=0, keepdims=True))
        alpha = jnp.exp(m_old - m_new)
        p = jnp.exp(s - m_new)
        m_s[...] = m_new
        l_s[...] = alpha * l_s[...] + jnp.sum(p, axis=0, keepdims=True)
        acc_s[...] = alpha * acc_s[...] + pv(n, p)

    @pl.when(j >= 1)
    def _():
        n = j - 1
        accumulate(n, scores(n) + bprev_ref[0] + am_s[pl.ds(n, 1), :])

    def far_step(n, carry):
        accumulate(n, scores(n) + am_s[pl.ds(n, 1), :])
        return carry

    lax.fori_loop(0, j - 1, far_step, 0)

    o_ref[0] = (acc_s[...] / l_s[...]).astype(o_ref.dtype)


def _attention(qh, kh, vth, bown, bprev, bfar):
    bhn, s, dh = qh.shape
    nb = s // MOBA_BLOCK
    kern = functools.partial(_attn_kernel, n_blocks=nb)
    grid_spec = pltpu.PrefetchScalarGridSpec(
        num_scalar_prefetch=0,
        grid=(bhn, nb),
        in_specs=[
            pl.BlockSpec(memory_space=pltpu.SMEM),
            pl.BlockSpec((1, MOBA_BLOCK, dh), lambda b, j: (b, j, 0)),
            pl.BlockSpec((1, s, dh), lambda b, j: (b, 0, 0)),
            pl.BlockSpec((1, dh, s), lambda b, j: (b, 0, 0)),
            pl.BlockSpec((1, MOBA_BLOCK, MOBA_BLOCK), lambda b, j: (b % N_HEADS, 0, 0)),
            pl.BlockSpec((1, MOBA_BLOCK, MOBA_BLOCK), lambda b, j: (b % N_HEADS, 0, 0)),
        ],
        out_specs=pl.BlockSpec((1, dh, MOBA_BLOCK), lambda b, j: (b, 0, j)),
        scratch_shapes=[
            pltpu.VMEM((nb, dh), F32),
            pltpu.VMEM((nb, MOBA_BLOCK), F32),
            pltpu.VMEM((1, MOBA_BLOCK), F32),
            pltpu.VMEM((1, MOBA_BLOCK), F32),
            pltpu.VMEM((dh, MOBA_BLOCK), F32),
        ],
    )
    return pl.pallas_call(
        kern,
        grid_spec=grid_spec,
        out_shape=jax.ShapeDtypeStruct((bhn, dh, s), BF16),
        compiler_params=pltpu.CompilerParams(
            dimension_semantics=("parallel", "arbitrary"), vmem_limit_bytes=VMEM_LIMIT),
        name="moba_attention",
    )(bfar, qh, kh, vth, bown, bprev)


def _t5_bias_tables(rel_bias):
    blk = MOBA_BLOCK
    n = jnp.arange(2 * blk)
    max_exact = NUM_BUCKETS // 2
    nf = jnp.maximum(n, 1).astype(F32)
    large = max_exact + (jnp.log(nf / max_exact) / math.log(MAX_DISTANCE / max_exact)
                         * (NUM_BUCKETS - max_exact)).astype(I32)
    large = jnp.minimum(large, NUM_BUCKETS - 1)
    bucket = jnp.where(n < max_exact, n, large)
    tbl = rel_bias.astype(F32)[bucket].T
    key = jnp.arange(blk)[:, None]
    qry = jnp.arange(blk)[None, :]
    d = qry - key
    bown = jnp.where(d >= 0, tbl[:, jnp.maximum(d, 0)], MASKED)
    bprev = tbl[:, d + blk]
    bfar = rel_bias.astype(F32)[NUM_BUCKETS - 1]
    return bown, bprev, bfar


def _mix_kernel(x_ref, attn_ref, u_ref, halo_ref, gl_ref, pw_ref, ps_ref, woa_ref, wop_ref,
                wout_ref, gffn_ref, rwt_ref, rb_ref,
                x1_ref, hf_ref, idx_ref, rank_ref, gate_ref, cnt_ref,
                ext_s, carry_s, *, tiles_per_seq):
    i = pl.program_id(0)
    tm = MIX_TM

    @pl.when(i == 0)
    def _():
        carry_s[...] = jnp.zeros_like(carry_s)

    first = (i % tiles_per_seq) == 0
    ext_s[0:POOL_HALO, :] = jnp.where(first, 0.0, halo_ref[...])
    ext_s[POOL_HALO:POOL_HALO + tm, :] = u_ref[...]
    pos = (i % tiles_per_seq) * tm + lax.broadcasted_iota(I32, (tm, 1), 0)
    pooled_parts = []
    for g, w in enumerate(POOL_WINDOWS):
        c0, c1 = g * POOL_GROUP_DIM, (g + 1) * POOL_GROUP_DIM
        win = ext_s[POOL_HALO:POOL_HALO + tm, c0:c1]
        for sft in range(1, w):
            win = win + ext_s[POOL_HALO - sft:POOL_HALO - sft + tm, c0:c1]
        cnt = jnp.minimum(pos + 1, w).astype(F32)
        pin = win / cnt - ext_s[POOL_HALO:POOL_HALO + tm, c0:c1]
        pooled_parts.append(jnp.dot(pin.astype(BF16), pw_ref[g], preferred_element_type=F32))
    pooled = jnp.concatenate(pooled_parts, axis=1) * ps_ref[...]

    a = jnp.dot(attn_ref[...], woa_ref[...], preferred_element_type=F32)
    pm = jnp.dot(pooled.astype(BF16), wop_ref[...], preferred_element_type=F32)
    g0 = _sigmoid(gl_ref[:, 0:D_MODEL].astype(F32))
    g1 = _sigmoid(gl_ref[:, D_MODEL:2 * D_MODEL].astype(F32))
    merged = g0 * a + g1 * pm
    x1 = x_ref[...] + jnp.dot(merged.astype(BF16), wout_ref[...], preferred_element_type=F32)
    x1_ref[...] = x1

    hf = _rms(x1, gffn_ref[...])
    hfb = hf.astype(BF16)
    for s in range(ROW_TILES):
        hf_ref[pl.ds(s, tm, stride=ROW_TILES), :] = hfb[:, s * LANES:(s + 1) * LANES].astype(F32)

    logits = lax.dot_general(rwt_ref[...], hfb.astype(F32), _NT,
                             precision=lax.Precision.HIGHEST,
                             preferred_element_type=F32) + rb_ref[...]
    e_iota = lax.broadcasted_iota(I32, logits.shape, 0)
    vals, picks = [], []
    for _ in range(TOP_K):
        mx = jnp.max(logits, axis=0, keepdims=True)
        idx = jnp.min(jnp.where(logits == mx, e_iota, N_EXPERTS), axis=0, keepdims=True)
        pick = e_iota == idx
        vals.append(mx)
        picks.append(pick)
        idx_ref[len(picks) - 1:len(picks), :] = idx
        logits = jnp.where(pick, -jnp.inf, logits)
    ex = [jnp.exp(v - vals[0]) for v in vals]
    den = ex[0] + ex[1] + ex[2] + ex[3]
    for k in range(TOP_K):
        gate_ref[k:k + 1, :] = ex[k] / den

    onehot = jnp.zeros(e_iota.shape, F32)
    for pick in picks:
        onehot = onehot + pick.astype(F32)
    ra = lax.broadcasted_iota(I32, (tm, tm), 0)
    rb = lax.broadcasted_iota(I32, (tm, tm), 1)
    upper = (ra < rb).astype(BF16)
    before = jnp.dot(onehot.astype(BF16), upper, preferred_element_type=F32) + carry_s[...]
    for k, pick in enumerate(picks):
        rank_ref[k:k + 1, :] = jnp.sum(jnp.where(pick, before, 0.0), axis=0,
                                       keepdims=True).astype(I32)
    carry_s[...] = carry_s[...] + jnp.sum(onehot, axis=1, keepdims=True)
    cnt_ref[...] = jnp.broadcast_to(carry_s[...], cnt_ref.shape).astype(I32)


def _mix(x2, attn, u, gl, pw, ps, woa, wop, wout, gffn, rwt, rb, seq):
    t = x2.shape[0]
    tm = MIX_TM
    tiles_per_seq = seq // tm
    halo_per_tile = tm // POOL_HALO
    row = lambda i: (i, 0)
    fixed2 = lambda i: (0, 0)
    fixed3 = lambda i: (0, 0, 0)
    col = lambda i: (0, i)
    kern = functools.partial(_mix_kernel, tiles_per_seq=tiles_per_seq)
    return pl.pallas_call(
        kern,
        grid=(t // tm,),
        in_specs=[
            pl.BlockSpec((tm, D_MODEL), row),
            pl.BlockSpec((tm, ATTN_WIDTH), row),
            pl.BlockSpec((tm, POOL_WIDTH), row),
            pl.BlockSpec((POOL_HALO, POOL_WIDTH),
                         lambda i: (jnp.maximum(i * halo_per_tile - 1, 0), 0)),
            pl.BlockSpec((tm, 2 * D_MODEL), row),
            pl.BlockSpec((POOL_GROUPS, POOL_GROUP_DIM, POOL_GROUP_DIM), fixed3),
            pl.BlockSpec((1, POOL_WIDTH), fixed2),
            pl.BlockSpec((ATTN_WIDTH, D_MODEL), fixed2),
            pl.BlockSpec((POOL_WIDTH, D_MODEL), fixed2),
            pl.BlockSpec((D_MODEL, D_MODEL), fixed2),
            pl.BlockSpec((1, D_MODEL), fixed2),
            pl.BlockSpec((N_EXPERTS, D_MODEL), fixed2),
            pl.BlockSpec((N_EXPERTS, 1), fixed2),
        ],
        out_specs=[
            pl.BlockSpec((tm, D_MODEL), row),
            pl.BlockSpec((tm * ROW_TILES, LANES), row),
            pl.BlockSpec((TOP_K, tm), col),
            pl.BlockSpec((TOP_K, tm), col),
            pl.BlockSpec((TOP_K, tm), col),
            pl.BlockSpec((N_EXPERTS, LANES), fixed2),
        ],
        out_shape=[
            jax.ShapeDtypeStruct((t, D_MODEL), F32),
            jax.ShapeDtypeStruct((t * ROW_TILES, LANES), F32),
            jax.ShapeDtypeStruct((TOP_K, t), I32),
            jax.ShapeDtypeStruct((TOP_K, t), I32),
            jax.ShapeDtypeStruct((TOP_K, t), F32),
            jax.ShapeDtypeStruct((N_EXPERTS, LANES), I32),
        ],
        scratch_shapes=[
            pltpu.VMEM((POOL_HALO + tm, POOL_WIDTH), F32),
            pltpu.VMEM((N_EXPERTS, 1), F32),
        ],
        compiler_params=pltpu.CompilerParams(
            dimension_semantics=("arbitrary",), vmem_limit_bytes=VMEM_LIMIT),
        name="mix_router",
    )(x2, attn, u, u, gl, pw, ps, woa, wop, wout, gffn, rwt, rb)


def _row(ref, r):
    return ref.at[pl.ds(pl.multiple_of(r * ROW_TILES, ROW_TILES), ROW_TILES)]


def _dispatch_kernel(dest_ref, zstart_ref, hf_ref, xs_ref, zero_s, zsem, sem, *, n_tokens):
    blk_rows = EXPERT_TM * ROW_TILES
    zero_s[...] = jnp.zeros_like(zero_s)

    def zero_copy(e):
        start = pl.multiple_of(zstart_ref[e] * ROW_TILES, ROW_TILES)
        return pltpu.make_async_copy(zero_s, xs_ref.at[pl.ds(start, blk_rows)], zsem)

    for e in range(N_EXPERTS):
        @pl.when(zstart_ref[e] >= 0)
        def _():
            zero_copy(e).start()
    for e in range(N_EXPERTS):
        @pl.when(zstart_ref[e] >= 0)
        def _():
            zero_copy(e).wait()

    chunk_rows = DISPATCH_CHUNK * TOP_K * ROW_TILES

    def chunk_wait():
        pltpu.make_async_copy(hf_ref.at[pl.ds(0, chunk_rows)],
                              xs_ref.at[pl.ds(0, chunk_rows)], sem).wait()

    def chunk(c, carry):
        def tok(tt, carry2):
            t = c * DISPATCH_CHUNK + tt
            for k in range(TOP_K):
                pltpu.make_async_copy(_row(hf_ref, t), _row(xs_ref, dest_ref[t * TOP_K + k]),
                                      sem).start()
            return carry2

        lax.fori_loop(0, DISPATCH_CHUNK, tok, 0)

        @pl.when(c >= 1)
        def _():
            chunk_wait()
        return carry

    lax.fori_loop(0, n_tokens // DISPATCH_CHUNK, chunk, 0)
    chunk_wait()


def _dispatch(dest_flat, zstart, hf_rows, n_pad):
    n_tokens = hf_rows.shape[0] // ROW_TILES
    kern = functools.partial(_dispatch_kernel, n_tokens=n_tokens)
    grid_spec = pltpu.PrefetchScalarGridSpec(
        num_scalar_prefetch=2,
        grid=(1,),
        in_specs=[pl.BlockSpec(memory_space=pl.ANY)],
        out_specs=pl.BlockSpec(memory_space=pl.ANY),
        scratch_shapes=[
            pltpu.VMEM((EXPERT_TM * ROW_TILES, LANES), F32),
            pltpu.SemaphoreType.DMA,
            pltpu.SemaphoreType.DMA,
        ],
    )
    return pl.pallas_call(
        kern,
        grid_spec=grid_spec,
        out_shape=jax.ShapeDtypeStruct((n_pad * ROW_TILES, LANES), F32),
        compiler_params=pltpu.CompilerParams(
            dimension_semantics=("arbitrary",), vmem_limit_bytes=VMEM_LIMIT),
        name="dispatch_rows",
    )(dest_flat, zstart, hf_rows)


def _expert_kernel(be_ref, nused_ref, x_ref, wg_ref, wu_ref, wd_ref, bg_ref, bu_ref, bd_ref,
                   y_ref):
    i = pl.program_id(0)
    tm = EXPERT_TM

    @pl.when(i < nused_ref[0])
    def _():
        xb = jnp.concatenate(
            [x_ref[pl.ds(s, tm, stride=ROW_TILES), :] for s in range(ROW_TILES)],
            axis=1).astype(BF16)
        gate = jnp.dot(xb, wg_ref[0], preferred_element_type=F32) + bg_ref[0]
        up = jnp.dot(xb, wu_ref[0], preferred_element_type=F32) + bu_ref[0]
        glu = jnp.minimum(gate, SWIGLU_LIMIT)
        lin = jnp.clip(up, -SWIGLU_LIMIT, SWIGLU_LIMIT)
        hid = (lin + 1.0) * (glu * _sigmoid(SWIGLU_ALPHA * glu))
        y = jnp.dot(hid.astype(BF16), wd_ref[0], preferred_element_type=F32) + bd_ref[0]
        for s in range(ROW_TILES):
            y_ref[pl.ds(s, tm, stride=ROW_TILES), :] = y[:, s * LANES:(s + 1) * LANES]


def _experts(blk_expert, n_used, xs, wg, wu, wd, bg, bu, bd):
    n_blk = blk_expert.shape[0]
    tm = EXPERT_TM

    def xmap(i, be, nu):
        return (jnp.minimum(i, nu[0] - 1), 0)

    def wmap(i, be, nu):
        return (be[jnp.minimum(i, nu[0] - 1)], 0, 0)

    grid_spec = pltpu.PrefetchScalarGridSpec(
        num_scalar_prefetch=2,
        grid=(n_blk,),
        in_specs=[
            pl.BlockSpec((tm * ROW_TILES, LANES), xmap),
            pl.BlockSpec((1, D_MODEL, D_MODEL), wmap),
            pl.BlockSpec((1, D_MODEL, D_MODEL), wmap),
            pl.BlockSpec((1, D_MODEL, D_MODEL), wmap),
            pl.BlockSpec((1, 1, D_MODEL), wmap),
            pl.BlockSpec((1, 1, D_MODEL), wmap),
            pl.BlockSpec((1, 1, D_MODEL), wmap),
        ],
        out_specs=pl.BlockSpec((tm * ROW_TILES, LANES), xmap),
    )
    return pl.pallas_call(
        _expert_kernel,
        grid_spec=grid_spec,
        out_shape=jax.ShapeDtypeStruct(xs.shape, F32),
        compiler_params=pltpu.CompilerParams(
            dimension_semantics=("arbitrary",), vmem_limit_bytes=VMEM_LIMIT),
        name="expert_ffn",
    )(blk_expert, n_used, xs, wg, wu, wd, bg, bu, bd)


def _combine_kernel(dest_ref, x1_ref, gate_ref, ys_ref, p_ref, gple_ref, wpg_ref, wpp_ref,
                    gfin_ref, o_ref, ybuf, sem):
    i = pl.program_id(0)
    tm = OUT_TM

    def tok(r, carry):
        t = i * tm + r
        for k in range(TOP_K):
            pltpu.make_async_copy(_row(ys_ref, dest_ref[t * TOP_K + k]),
                                  _row(ybuf, k * tm + r), sem).start()
        return carry

    lax.fori_loop(0, tm, tok, 0)
    pltpu.make_async_copy(ys_ref.at[pl.ds(0, TOP_K * tm * ROW_TILES)], ybuf, sem).wait()

    moe = None
    for k in range(TOP_K):
        yk = jnp.concatenate(
            [ybuf[pl.ds(k * tm * ROW_TILES + s, tm, stride=ROW_TILES), :]
             for s in range(ROW_TILES)], axis=1)
        term = gate_ref[:, k:k + 1] * yk
        moe = term if moe is None else moe + term
    x2 = x1_ref[...] + moe
    hp = _rms(x2, gple_ref[...]).astype(BF16)
    pg = _sigmoid(jnp.dot(hp, wpg_ref[...], preferred_element_type=F32))
    proj = jnp.dot(p_ref[...].astype(BF16), wpp_ref[...], preferred_element_type=F32)
    x3 = x2 + pg * proj
    o_ref[...] = _rms(x3, gfin_ref[...])


def _combine(dest_flat, x1, gates_tk, ys, p2, gple, wpg, wpp, gfin):
    t = x1.shape[0]
    tm = OUT_TM
    ple = p2.shape[1]
    row = lambda i, d: (i, 0)
    fixed = lambda i, d: (0, 0)
    grid_spec = pltpu.PrefetchScalarGridSpec(
        num_scalar_prefetch=1,
        grid=(t // tm,),
        in_specs=[
            pl.BlockSpec((tm, D_MODEL), row),
            pl.BlockSpec((tm, TOP_K), row),
            pl.BlockSpec(memory_space=pl.ANY),
            pl.BlockSpec((tm, ple), row),
            pl.BlockSpec((1, D_MODEL), fixed),
            pl.BlockSpec((D_MODEL, D_MODEL), fixed),
            pl.BlockSpec((ple, D_MODEL), fixed),
            pl.BlockSpec((1, D_MODEL), fixed),
        ],
        out_specs=pl.BlockSpec((tm, D_MODEL), row),
        scratch_shapes=[
            pltpu.VMEM((TOP_K * tm * ROW_TILES, LANES), F32),
            pltpu.SemaphoreType.DMA,
        ],
    )
    return pl.pallas_call(
        _combine_kernel,
        grid_spec=grid_spec,
        out_shape=jax.ShapeDtypeStruct((t, D_MODEL), F32),
        compiler_params=pltpu.CompilerParams(
            dimension_semantics=("arbitrary",), vmem_limit_bytes=VMEM_LIMIT),
        name="combine_ple_final",
    )(dest_flat, x1, gates_tk, ys, p2, gple, wpg, wpp, gfin)


def kernel(x, p, rel_bias, norm_mix_g, w_in, pool_w, pool_scale, w_o_attn, w_o_pool, w_out,
           norm_ffn_g, router_w, router_b, w_gate_up, b_gate_up, w_down, b_down,
           norm_ple_g, w_ple_gate, w_ple_proj, norm_final_g):
    b, s, d = x.shape
    depth = w_in.shape[0]
    t = b * s
    assert d == D_MODEL and s % MOBA_BLOCK == 0 and t % IN_TM == 0 and s % MIX_TM == 0
    n_pad = t * TOP_K + N_EXPERTS * EXPERT_TM
    n_blk = n_pad // EXPERT_TM
    bown, bprev, bfar = _t5_bias_tables(rel_bias)

    x2 = x.reshape(t, d)
    for i in range(depth):
        q, k, v, u, gl = _in_proj(x2, norm_mix_g[i].reshape(1, d), w_in[i].astype(BF16))

        def heads(a):
            return a.reshape(b, s, N_HEADS, HEAD_DIM).transpose(0, 2, 1, 3).reshape(
                b * N_HEADS, s, HEAD_DIM)

        vth = v.reshape(b, s, N_HEADS, HEAD_DIM).transpose(0, 2, 3, 1).reshape(
            b * N_HEADS, HEAD_DIM, s)
        attn_t = _attention(heads(q), heads(k), vth, bown, bprev, bfar)
        attn = attn_t.reshape(b, N_HEADS, HEAD_DIM, s).transpose(0, 3, 1, 2).reshape(
            t, ATTN_WIDTH)

        x1, hf_rows, idx_kt, rank_kt, gate_kt, cnt = _mix(
            x2, attn, u, gl, pool_w[i].astype(BF16), pool_scale[i].reshape(1, POOL_WIDTH),
            w_o_attn[i].astype(BF16), w_o_pool[i].astype(BF16), w_out[i].astype(BF16),
            norm_ffn_g[i].reshape(1, d), router_w[i].T, router_b[i].reshape(N_EXPERTS, 1), s)

        counts = cnt[:, 0]
        padded = (counts + EXPERT_TM - 1) // EXPERT_TM * EXPERT_TM
        pend = jnp.cumsum(padded)
        pstart = pend - padded
        dest_flat = (pstart[idx_kt] + rank_kt).T.reshape(-1).astype(I32)
        blk_expert = jnp.minimum(
            jnp.searchsorted(pend, jnp.arange(n_blk, dtype=I32) * EXPERT_TM, side='right'),
            N_EXPERTS - 1).astype(I32)
        n_used = (pend[-1:] // EXPERT_TM).astype(I32)
        zstart = jnp.where(padded > 0, pend - EXPERT_TM, -1).astype(I32)

        xs = _dispatch(dest_flat, zstart, hf_rows, n_pad)
        wgu = w_gate_up[i]
        bgu = b_gate_up[i]
        ys = _experts(blk_expert, n_used, xs,
                      wgu[:, :, 0::2].astype(BF16), wgu[:, :, 1::2].astype(BF16),
                      w_down[i].astype(BF16),
                      bgu[:, None, 0::2], bgu[:, None, 1::2], b_down[i][:, None, :])

        gfin = norm_final_g if i == depth - 1 else jnp.ones_like(norm_final_g)
        x2 = _combine(dest_flat, x1, gate_kt.T, ys, p[i].reshape(t, -1),
                      norm_ple_g[i].reshape(1, d), w_ple_gate[i].astype(BF16),
                      w_ple_proj[i].astype(BF16), gfin.reshape(1, d))
    return x2.reshape(b, s, d)
```

```python
import functools
import math

import jax
import jax.numpy as jnp
from jax import lax
from jax.experimental import pallas as pl
from jax.experimental.pallas import tpu as pltpu

F32 = jnp.float32
BF16 = jnp.bfloat16
I32 = jnp.int32

D_MODEL = 1024
N_HEADS = 8
HEAD_DIM = 64
ATTN_WIDTH = N_HEADS * HEAD_DIM
MOBA_BLOCK = 256
MOBA_TOPK = 3
NUM_BUCKETS = 32
MAX_DISTANCE = 128
POOL_WINDOWS = (2, 4, 8, 16)
POOL_GROUPS = 4
POOL_GROUP_DIM = 128
POOL_WIDTH = POOL_GROUPS * POOL_GROUP_DIM
N_EXPERTS = 32
TOP_K = 4
SWIGLU_LIMIT = 7.0
SWIGLU_ALPHA = 1.702
RMS_EPS = 1e-6

LANES = 128
SUBLANES = 8
ROW_TILES = D_MODEL // LANES

IN_TM = 512
MIX_TM = 256
EXPERT_TM = 256
OUT_TM = 256
POOL_HALO = 16
MASKED = -1e30

VMEM_LIMIT = 56 * 1024 * 1024

_NT = (((1,), (1,)), ((), ()))


def _rms(x, g):
    ms = jnp.mean(x * x, axis=-1, keepdims=True)
    return x * lax.rsqrt(ms + RMS_EPS) * g


def _sigmoid(x):
    return 1.0 / (1.0 + jnp.exp(-x))


def _in_proj_kernel(x_ref, g_ref, w_ref, q_ref, k_ref, v_ref, u_ref, gl_ref):
    h = _rms(x_ref[...], g_ref[...]).astype(BF16)
    aw = ATTN_WIDTH

    def proj(lo, hi):
        return jnp.dot(h, w_ref[:, lo:hi], preferred_element_type=F32)

    q_ref[...] = proj(0, aw).astype(BF16)
    k_ref[...] = proj(aw, 2 * aw).astype(BF16)
    v_ref[...] = proj(2 * aw, 3 * aw).astype(BF16)
    u_ref[...] = proj(3 * aw, 3 * aw + POOL_WIDTH)
    base = 3 * aw + POOL_WIDTH
    for c in range(2):
        lo = base + c * D_MODEL
        gl_ref[:, c * D_MODEL:(c + 1) * D_MODEL] = proj(lo, lo + D_MODEL).astype(BF16)


def _in_proj(x2, g, w_bf16):
    t = x2.shape[0]
    in_cols = w_bf16.shape[1]
    row = lambda i: (i, 0)
    fixed = lambda i: (0, 0)
    return pl.pallas_call(
        _in_proj_kernel,
        grid=(t // IN_TM,),
        in_specs=[
            pl.BlockSpec((IN_TM, D_MODEL), row),
            pl.BlockSpec((1, D_MODEL), fixed),
            pl.BlockSpec((D_MODEL, in_cols), fixed),
        ],
        out_specs=[
            pl.BlockSpec((IN_TM, ATTN_WIDTH), row),
            pl.BlockSpec((IN_TM, ATTN_WIDTH), row),
            pl.BlockSpec((IN_TM, ATTN_WIDTH), row),
            pl.BlockSpec((IN_TM, POOL_WIDTH), row),
            pl.BlockSpec((IN_TM, 2 * D_MODEL), row),
        ],
        out_shape=[
            jax.ShapeDtypeStruct((t, ATTN_WIDTH), BF16),
            jax.ShapeDtypeStruct((t, ATTN_WIDTH), BF16),
            jax.ShapeDtypeStruct((t, ATTN_WIDTH), BF16),
            jax.ShapeDtypeStruct((t, POOL_WIDTH), F32),
            jax.ShapeDtypeStruct((t, 2 * D_MODEL), BF16),
        ],
        compiler_params=pltpu.CompilerParams(
            dimension_semantics=("parallel",), vmem_limit_bytes=VMEM_LIMIT),
        name="in_proj",
    )(x2, g, w_bf16)


def _attn_kernel(bfar_ref, q_ref, k_ref, vt_ref, bown_ref, bprev_ref, o_ref,
                 kmean_s, am_s, m_s, l_s, acc_s, *, n_blocks):
    bh = pl.program_id(0)
    j = pl.program_id(1)
    blk = MOBA_BLOCK

    @pl.when(j == 0)
    def _():
        for n in range(n_blocks):
            kb = k_ref[0, n * blk:(n + 1) * blk, :].astype(F32)
            kmean_s[n:n + 1, :] = jnp.mean(kb, axis=0, keepdims=True)

    q = q_ref[0] * jnp.asarray(HEAD_DIM ** -0.5, BF16)

    gate = lax.dot_general(kmean_s[...], q.astype(F32), _NT,
                           precision=lax.Precision.HIGHEST, preferred_element_type=F32)
    n_iota = lax.broadcasted_iota(I32, gate.shape, 0)
    past = n_iota < j
    gate = jnp.where(past, gate, jnp.finfo(F32).min)
    sel = jnp.zeros(gate.shape, jnp.bool_)
    for _ in range(MOBA_TOPK):
        mx = jnp.max(gate, axis=0, keepdims=True)
        idx = jnp.min(jnp.where(gate == mx, n_iota, n_blocks), axis=0, keepdims=True)
        pick = n_iota == idx
        sel = jnp.logical_or(sel, pick)
        gate = jnp.where(pick, -jnp.inf, gate)
    sel = jnp.logical_and(sel, past)
    bfar = bfar_ref[bh % N_HEADS]
    am_s[...] = jnp.where(sel, jnp.where(n_iota == j - 1, 0.0, bfar), MASKED)

    def scores(n):
        kb = k_ref[0, pl.ds(pl.multiple_of(n * blk, blk), blk), :]
        return lax.dot_general(kb, q, _NT, preferred_element_type=F32)

    def pv(n, p):
        vt = vt_ref[0, :, pl.ds(pl.multiple_of(n * blk, blk), blk)]
        return jnp.dot(vt, p.astype(BF16), preferred_element_type=F32)

    s = scores(j) + bown_ref[0]
    m = jnp.max(s, axis=0, keepdims=True)
    p = jnp.exp(s - m)
    m_s[...] = m
    l_s[...] = jnp.sum(p, axis=0, keepdims=True)
    acc_s[...] = pv(j, p)

    def accumulate(n, s):
        m_old = m_s[...]
        m_new = jnp.maximum(m_old, jnp.max(s, axis=0, keepdims=True))
        alpha = jnp.exp(m_old - m_new)
        p = jnp.exp(s - m_new)
        m_s[...] = m_new
        l_s[...] = alpha * l_s[...] + jnp.sum(p, axis=0, keepdims=True)
        acc_s[...] = alpha * acc_s[...] + pv(n, p)

    @pl.when(j >= 1)
    def _():
        n = j - 1
        accumulate(n, scores(n) + bprev_ref[0] + am_s[pl.ds(n, 1), :])

    def far_step(n, carry):
        accumulate(n, scores(n) + am_s[pl.ds(n, 1), :])
        return carry

    lax.fori_loop(0, j - 1, far_step, 0)

    o_ref[0] = (acc_s[...] / l_s[...]).astype(o_ref.dtype)


def _attention(qh, kh, vth, bown, bprev, bfar):
    bhn, s, dh = qh.shape
    nb = s // MOBA_BLOCK
    kern = functools.partial(_attn_kernel, n_blocks=nb)
    grid_spec = pltpu.PrefetchScalarGridSpec(
        num_scalar_prefetch=0,
        grid=(bhn, nb),
        in_specs=[
            pl.BlockSpec(memory_space=pltpu.SMEM),
            pl.BlockSpec((1, MOBA_BLOCK, dh), lambda b, j: (b, j, 0)),
            pl.BlockSpec((1, s, dh), lambda b, j: (b, 0, 0)),
            pl.BlockSpec((1, dh, s), lambda b, j: (b, 0, 0)),
            pl.BlockSpec((1, MOBA_BLOCK, MOBA_BLOCK), lambda b, j: (b % N_HEADS, 0, 0)),
            pl.BlockSpec((1, MOBA_BLOCK, MOBA_BLOCK), lambda b, j: (b % N_HEADS, 0, 0)),
        ],
        out_specs=pl.BlockSpec((1, dh, MOBA_BLOCK), lambda b, j: (b, 0, j)),
        scratch_shapes=[
            pltpu.VMEM((nb, dh), F32),
            pltpu.VMEM((nb, MOBA_BLOCK), F32),
            pltpu.VMEM((1, MOBA_BLOCK), F32),
            pltpu.VMEM((1, MOBA_BLOCK), F32),
            pltpu.VMEM((dh, MOBA_BLOCK), F32),
        ],
    )
    return pl.pallas_call(
        kern,
        grid_spec=grid_spec,
        out_shape=jax.ShapeDtypeStruct((bhn, dh, s), BF16),
        compiler_params=pltpu.CompilerParams(
            dimension_semantics=("parallel", "arbitrary"), vmem_limit_bytes=VMEM_LIMIT),
        name="moba_attention",
    )(bfar, qh, kh, vth, bown, bprev)


def _t5_bias_tables(rel_bias):
    blk = MOBA_BLOCK
    n = jnp.arange(2 * blk)
    max_exact = NUM_BUCKETS // 2
    nf = jnp.maximum(n, 1).astype(F32)
    large = max_exact + (jnp.log(nf / max_exact) / math.log(MAX_DISTANCE / max_exact)
                         * (NUM_BUCKETS - max_exact)).astype(I32)
    large = jnp.minimum(large, NUM_BUCKETS - 1)
    bucket = jnp.where(n < max_exact, n, large)
    tbl = rel_bias.astype(F32)[bucket].T
    key = jnp.arange(blk)[:, None]
    qry = jnp.arange(blk)[None, :]
    d = qry - key
    h = tbl.shape[0]
    wide = 2 * blk + 1
    skew = jnp.broadcast_to(jnp.pad(tbl, ((0, 0), (0, 1)))[:, None, :], (h, blk, wide))
    skew = skew.reshape(h, blk * wide)[:, :blk * 2 * blk].reshape(h, blk, 2 * blk)
    bown = jnp.where(d >= 0, skew[:, :, :blk], MASKED)
    bprev = skew[:, :, blk:]
    bfar = rel_bias.astype(F32)[NUM_BUCKETS - 1]
    return bown, bprev, bfar


def _mix_kernel(x_ref, attn_ref, u_ref, halo_ref, gl_ref, pw_ref, ps_ref, woa_ref, wop_ref,
                wout_ref, gffn_ref, rwt_ref, rb_ref,
                x1_ref, hf_ref, idx_ref, rank_ref, gate_ref, cnt_ref,
                ext_s, carry_s, *, tiles_per_seq):
    i = pl.program_id(0)
    tm = MIX_TM

    @pl.when(i == 0)
    def _():
        carry_s[...] = jnp.zeros_like(carry_s)

    first = (i % tiles_per_seq) == 0
    ext_s[0:POOL_HALO, :] = jnp.where(first, 0.0, halo_ref[...])
    ext_s[POOL_HALO:POOL_HALO + tm, :] = u_ref[...]
    pos = (i % tiles_per_seq) * tm + lax.broadcasted_iota(I32, (tm, 1), 0)
    pooled_parts = []
    for g, w in enumerate(POOL_WINDOWS):
        c0, c1 = g * POOL_GROUP_DIM, (g + 1) * POOL_GROUP_DIM
        win = ext_s[POOL_HALO:POOL_HALO + tm, c0:c1]
        for sft in range(1, w):
            win = win + ext_s[POOL_HALO - sft:POOL_HALO - sft + tm, c0:c1]
        cnt = jnp.minimum(pos + 1, w).astype(F32)
        pin = win / cnt - ext_s[POOL_HALO:POOL_HALO + tm, c0:c1]
        pooled_parts.append(jnp.dot(pin.astype(BF16), pw_ref[g], preferred_element_type=F32))
    pooled = jnp.concatenate(pooled_parts, axis=1) * ps_ref[...]

    a = jnp.dot(attn_ref[...], woa_ref[...], preferred_element_type=F32)
    pm = jnp.dot(pooled.astype(BF16), wop_ref[...], preferred_element_type=F32)
    g0 = _sigmoid(gl_ref[:, 0:D_MODEL].astype(F32))
    g1 = _sigmoid(gl_ref[:, D_MODEL:2 * D_MODEL].astype(F32))
    merged = g0 * a + g1 * pm
    x1 = x_ref[...] + jnp.dot(merged.astype(BF16), wout_ref[...], preferred_element_type=F32)
    x1_ref[...] = x1

    hf = _rms(x1, gffn_ref[...])
    hfb = hf.astype(BF16)
    for s in range(ROW_TILES):
        hf_ref[pl.ds(s, tm, stride=ROW_TILES), :] = hfb[:, s * LANES:(s + 1) * LANES].astype(F32)

    logits = lax.dot_general(rwt_ref[...], hfb.astype(F32), _NT,
                             precision=lax.Precision.HIGHEST,
                             preferred_element_type=F32) + rb_ref[...]
    e_iota = lax.broadcasted_iota(I32, logits.shape, 0)
    vals, picks = [], []
    for _ in range(TOP_K):
        mx = jnp.max(logits, axis=0, keepdims=True)
        idx = jnp.min(jnp.where(logits == mx, e_iota, N_EXPERTS), axis=0, keepdims=True)
        pick = e_iota == idx
        vals.append(mx)
        picks.append(pick)
        idx_ref[len(picks) - 1:len(picks), :] = idx
        logits = jnp.where(pick, -jnp.inf, logits)
    ex = [jnp.exp(v - vals[0]) for v in vals]
    den = ex[0] + ex[1] + ex[2] + ex[3]
    for k in range(TOP_K):
        gate_ref[k:k + 1, :] = ex[k] / den

    onehot = jnp.zeros(e_iota.shape, F32)
    for pick in picks:
        onehot = onehot + pick.astype(F32)
    ra = lax.broadcasted_iota(I32, (tm, tm), 0)
    rb = lax.broadcasted_iota(I32, (tm, tm), 1)
    upper = (ra < rb).astype(BF16)
    before = jnp.dot(onehot.astype(BF16), upper, preferred_element_type=F32) + carry_s[...]
    for k, pick in enumerate(picks):
        rank_ref[k:k + 1, :] = jnp.sum(jnp.where(pick, before, 0.0), axis=0,
                                       keepdims=True).astype(I32)
    carry_s[...] = carry_s[...] + jnp.sum(onehot, axis=1, keepdims=True)
    cnt_ref[...] = jnp.broadcast_to(carry_s[...], cnt_ref.shape).astype(I32)


def _mix(x2, attn, u, gl, pw, ps, woa, wop, wout, gffn, rwt, rb, seq):
    t = x2.shape[0]
    tm = MIX_TM
    tiles_per_seq = seq // tm
    halo_per_tile = tm // POOL_HALO
    row = lambda i: (i, 0)
    fixed2 = lambda i: (0, 0)
    fixed3 = lambda i: (0, 0, 0)
    col = lambda i: (0, i)
    kern = functools.partial(_mix_kernel, tiles_per_seq=tiles_per_seq)
    return pl.pallas_call(
        kern,
        grid=(t // tm,),
        in_specs=[
            pl.BlockSpec((tm, D_MODEL), row),
            pl.BlockSpec((tm, ATTN_WIDTH), row),
            pl.BlockSpec((tm, POOL_WIDTH), row),
            pl.BlockSpec((POOL_HALO, POOL_WIDTH),
                         lambda i: (jnp.maximum(i * halo_per_tile - 1, 0), 0)),
            pl.BlockSpec((tm, 2 * D_MODEL), row),
            pl.BlockSpec((POOL_GROUPS, POOL_GROUP_DIM, POOL_GROUP_DIM), fixed3),
            pl.BlockSpec((1, POOL_WIDTH), fixed2),
            pl.BlockSpec((ATTN_WIDTH, D_MODEL), fixed2),
            pl.BlockSpec((POOL_WIDTH, D_MODEL), fixed2),
            pl.BlockSpec((D_MODEL, D_MODEL), fixed2),
            pl.BlockSpec((1, D_MODEL), fixed2),
            pl.BlockSpec((N_EXPERTS, D_MODEL), fixed2),
            pl.BlockSpec((N_EXPERTS, 1), fixed2),
        ],
        out_specs=[
            pl.BlockSpec((tm, D_MODEL), row),
            pl.BlockSpec((tm * ROW_TILES, LANES), row),
            pl.BlockSpec((TOP_K, tm), col),
            pl.BlockSpec((TOP_K, tm), col),
            pl.BlockSpec((TOP_K, tm), col),
            pl.BlockSpec((N_EXPERTS, LANES), fixed2),
        ],
        out_shape=[
            jax.ShapeDtypeStruct((t, D_MODEL), F32),
            jax.ShapeDtypeStruct((t * ROW_TILES, LANES), F32),
            jax.ShapeDtypeStruct((TOP_K, t), I32),
            jax.ShapeDtypeStruct((TOP_K, t), I32),
            jax.ShapeDtypeStruct((TOP_K, t), F32),
            jax.ShapeDtypeStruct((N_EXPERTS, LANES), I32),
        ],
        scratch_shapes=[
            pltpu.VMEM((POOL_HALO + tm, POOL_WIDTH), F32),
            pltpu.VMEM((N_EXPERTS, 1), F32),
        ],
        compiler_params=pltpu.CompilerParams(
            dimension_semantics=("arbitrary",), vmem_limit_bytes=VMEM_LIMIT),
        name="mix_router",
    )(x2, attn, u, u, gl, pw, ps, woa, wop, wout, gffn, rwt, rb)


def _row(ref, r):
    return ref.at[pl.ds(pl.multiple_of(r * ROW_TILES, ROW_TILES), ROW_TILES)]


def _pair_split_matrix():
    n = 2 * LANES
    r = lax.broadcasted_iota(I32, (n, n), 0)
    c = lax.broadcasted_iota(I32, (n, n), 1)
    src = jnp.where(c < LANES, 2 * c, 2 * (c - LANES) + 1)
    return (r == src).astype(BF16)


def _expert_kernel(be_ref, nused_ref, rowtok_ref, hf_ref, wgu_ref, wd_ref, bgu_ref, bd_ref,
                   y_ref, xbuf, wgu_s, wd_s, sems):
    i = pl.program_id(0)
    tm = EXPERT_TM
    n_used = nused_ref[0]
    slot = i % 2

    def gather(blk, to_slot, start):
        if start:
            def row(r, carry):
                tok = rowtok_ref[blk * tm + r]
                pltpu.make_async_copy(_row(hf_ref, tok), _row(xbuf.at[to_slot], r),
                                      sems.at[to_slot]).start()
                return carry
            lax.fori_loop(0, tm, row, 0)
        else:
            pltpu.make_async_copy(hf_ref.at[pl.ds(0, tm * ROW_TILES)], xbuf.at[to_slot],
                                  sems.at[to_slot]).wait()

    @pl.when(i == 0)
    def _():
        gather(0, 0, True)

    @pl.when(i + 1 < n_used)
    def _():
        gather(i + 1, 1 - slot, True)

    changed = jnp.logical_or(i == 0, be_ref[i] != be_ref[jnp.maximum(i - 1, 0)])

    @pl.when(jnp.logical_and(changed, i < n_used))
    def _():
        split = _pair_split_matrix()
        for c in range(2 * D_MODEL // (2 * LANES)):
            lo, hi = c * 2 * LANES, (c + 1) * 2 * LANES
            wgu_s[:, lo:hi] = jnp.dot(wgu_ref[0, :, lo:hi].astype(BF16), split,
                                      preferred_element_type=F32).astype(BF16)
        wd_s[...] = wd_ref[0].astype(BF16)

    @pl.when(i >= n_used)
    def _():
        y_ref[...] = jnp.zeros_like(y_ref)

    @pl.when(i < n_used)
    def _():
        gather(i, slot, False)
        xb = jnp.concatenate(
            [xbuf[slot, pl.ds(s, tm, stride=ROW_TILES), :] for s in range(ROW_TILES)],
            axis=1).astype(BF16)
        hid_parts = []
        for c in range(D_MODEL // LANES):
            lo, hi = c * 2 * LANES, (c + 1) * 2 * LANES
            gu = jnp.dot(xb, wgu_s[:, lo:hi], preferred_element_type=F32) + bgu_ref[0, :, lo:hi]
            glu = jnp.minimum(gu[:, :LANES], SWIGLU_LIMIT)
            lin = jnp.clip(gu[:, LANES:], -SWIGLU_LIMIT, SWIGLU_LIMIT)
            hid_parts.append(((lin + 1.0) * (glu * _sigmoid(SWIGLU_ALPHA * glu))).astype(BF16))
        hid = jnp.concatenate(hid_parts, axis=1)
        y = jnp.dot(hid, wd_s[...], preferred_element_type=F32) + bd_ref[0]
        for s in range(ROW_TILES):
            y_ref[pl.ds(s, tm, stride=ROW_TILES), :] = y[:, s * LANES:(s + 1) * LANES]


def _experts(blk_expert, n_used, row_tok, hf_rows, wgu, wd, bgu_split, bd):
    n_blk = blk_expert.shape[0]
    tm = EXPERT_TM

    def ymap(i, be, nu, rt):
        return (i, 0)

    def wmap(i, be, nu, rt):
        return (be[jnp.minimum(i, nu[0] - 1)], 0, 0)

    grid_spec = pltpu.PrefetchScalarGridSpec(
        num_scalar_prefetch=3,
        grid=(n_blk,),
        in_specs=[
            pl.BlockSpec(memory_space=pl.ANY),
            pl.BlockSpec((1, D_MODEL, 2 * D_MODEL), wmap),
            pl.BlockSpec((1, D_MODEL, D_MODEL), wmap),
            pl.BlockSpec((1, 1, 2 * D_MODEL), wmap),
            pl.BlockSpec((1, 1, D_MODEL), wmap),
        ],
        out_specs=pl.BlockSpec((tm * ROW_TILES, LANES), ymap),
        scratch_shapes=[
            pltpu.VMEM((2, tm * ROW_TILES, LANES), F32),
            pltpu.VMEM((D_MODEL, 2 * D_MODEL), BF16),
            pltpu.VMEM((D_MODEL, D_MODEL), BF16),
            pltpu.SemaphoreType.DMA((2,)),
        ],
    )
    return pl.pallas_call(
        _expert_kernel,
        grid_spec=grid_spec,
        out_shape=jax.ShapeDtypeStruct((n_blk * tm * ROW_TILES, LANES), F32),
        compiler_params=pltpu.CompilerParams(
            dimension_semantics=("arbitrary",), vmem_limit_bytes=VMEM_LIMIT),
        name="expert_ffn",
    )(blk_expert, n_used, row_tok, hf_rows, wgu, wd, bgu_split, bd)


def _combine_kernel(dest_ref, x1_ref, gate_ref, ys_ref, p_ref, gple_ref, wpg_ref, wpp_ref,
                    gfin_ref, o_ref, ybuf, sem):
    i = pl.program_id(0)
    tm = OUT_TM

    def tok(r, carry):
        t = i * tm + r
        for k in range(TOP_K):
            pltpu.make_async_copy(_row(ys_ref, dest_ref[t * TOP_K + k]),
                                  _row(ybuf, k * tm + r), sem).start()
        return carry

    lax.fori_loop(0, tm, tok, 0)
    pltpu.make_async_copy(ys_ref.at[pl.ds(0, TOP_K * tm * ROW_TILES)], ybuf, sem).wait()

    moe = None
    for k in range(TOP_K):
        yk = jnp.concatenate(
            [ybuf[pl.ds(k * tm * ROW_TILES + s, tm, stride=ROW_TILES), :]
             for s in range(ROW_TILES)], axis=1)
        term = gate_ref[:, k:k + 1] * yk
        moe = term if moe is None else moe + term
    x2 = x1_ref[...] + moe
    hp = _rms(x2, gple_ref[...]).astype(BF16)
    pg = _sigmoid(jnp.dot(hp, wpg_ref[...], preferred_element_type=F32))
    proj = jnp.dot(p_ref[...].astype(BF16), wpp_ref[...], preferred_element_type=F32)
    x3 = x2 + pg * proj
    o_ref[...] = _rms(x3, gfin_ref[...])


def _combine(dest_flat, x1, gates_tk, ys, p2, gple, wpg, wpp, gfin):
    t = x1.shape[0]
    tm = OUT_TM
    ple = p2.shape[1]
    row = lambda i, d: (i, 0)
    fixed = lambda i, d: (0, 0)
    grid_spec = pltpu.PrefetchScalarGridSpec(
        num_scalar_prefetch=1,
        grid=(t // tm,),
        in_specs=[
            pl.BlockSpec((tm, D_MODEL), row),
            pl.BlockSpec((tm, TOP_K), row),
            pl.BlockSpec(memory_space=pl.ANY),
            pl.BlockSpec((tm, ple), row),
            pl.BlockSpec((1, D_MODEL), fixed),
            pl.BlockSpec((D_MODEL, D_MODEL), fixed),
            pl.BlockSpec((ple, D_MODEL), fixed),
            pl.BlockSpec((1, D_MODEL), fixed),
        ],
        out_specs=pl.BlockSpec((tm, D_MODEL), row),
        scratch_shapes=[
            pltpu.VMEM((TOP_K * tm * ROW_TILES, LANES), F32),
            pltpu.SemaphoreType.DMA,
        ],
    )
    return pl.pallas_call(
        _combine_kernel,
        grid_spec=grid_spec,
        out_shape=jax.ShapeDtypeStruct((t, D_MODEL), F32),
        compiler_params=pltpu.CompilerParams(
            dimension_semantics=("arbitrary",), vmem_limit_bytes=VMEM_LIMIT),
        name="combine_ple_final",
    )(dest_flat, x1, gates_tk, ys, p2, gple, wpg, wpp, gfin)


def kernel(x, p, rel_bias, norm_mix_g, w_in, pool_w, pool_scale, w_o_attn, w_o_pool, w_out,
           norm_ffn_g, router_w, router_b, w_gate_up, b_gate_up, w_down, b_down,
           norm_ple_g, w_ple_gate, w_ple_proj, norm_final_g):
    b, s, d = x.shape
    depth = w_in.shape[0]
    t = b * s
    assert d == D_MODEL and s % MOBA_BLOCK == 0 and t % IN_TM == 0 and s % MIX_TM == 0
    assert depth == 1, "the final norm is fused into the layer's last kernel"
    n_pad = t * TOP_K + N_EXPERTS * EXPERT_TM
    n_blk = n_pad // EXPERT_TM
    bown, bprev, bfar = _t5_bias_tables(rel_bias)

    x2 = x.reshape(t, d)
    for i in range(depth):
        q, k, v, u, gl = _in_proj(x2, norm_mix_g[i].reshape(1, d), w_in[i].astype(BF16))

        def heads(a):
            return a.reshape(b, s, N_HEADS, HEAD_DIM).transpose(0, 2, 1, 3).reshape(
                b * N_HEADS, s, HEAD_DIM)

        vth = v.reshape(b, s, N_HEADS, HEAD_DIM).transpose(0, 2, 3, 1).reshape(
            b * N_HEADS, HEAD_DIM, s)
        attn_t = _attention(heads(q), heads(k), vth, bown, bprev, bfar)
        attn = attn_t.reshape(b, N_HEADS, HEAD_DIM, s).transpose(0, 3, 1, 2).reshape(
            t, ATTN_WIDTH)

        x1, hf_rows, idx_kt, rank_kt, gate_kt, cnt = _mix(
            x2, attn, u, gl, pool_w[i].astype(BF16), pool_scale[i].reshape(1, POOL_WIDTH),
            w_o_attn[i].astype(BF16), w_o_pool[i].astype(BF16), w_out[i].astype(BF16),
            norm_ffn_g[i].reshape(1, d), router_w[i].T, router_b[i].reshape(N_EXPERTS, 1), s)

        counts = cnt[:, 0]
        padded = (counts + EXPERT_TM - 1) // EXPERT_TM * EXPERT_TM
        pend = jnp.cumsum(padded)
        pstart = pend - padded
        e_ids = jnp.arange(N_EXPERTS, dtype=I32)
        pstart_of = jnp.sum(jnp.where(idx_kt[..., None] == e_ids, pstart, 0), axis=-1)
        dest_flat = (pstart_of + rank_kt).T.reshape(-1).astype(I32)
        blk_row0 = jnp.arange(n_blk, dtype=I32) * EXPERT_TM
        blk_expert = jnp.minimum(jnp.sum(pend[None, :] <= blk_row0[:, None], axis=1),
                                 N_EXPERTS - 1).astype(I32)
        n_used = (pend[-1:] // EXPERT_TM).astype(I32)
        row_tok = jnp.zeros((n_pad,), I32).at[dest_flat].set(
            jnp.arange(t * TOP_K, dtype=I32) // TOP_K, unique_indices=True)

        bgu = b_gate_up[i].reshape(N_EXPERTS, D_MODEL // LANES, LANES, 2).transpose(
            0, 1, 3, 2).reshape(N_EXPERTS, 1, 2 * D_MODEL)
        ys = _experts(blk_expert, n_used, row_tok, hf_rows, w_gate_up[i], w_down[i], bgu,
                      b_down[i][:, None, :])

        x2 = _combine(dest_flat, x1, gate_kt.T, ys, p[i].reshape(t, -1),
                      norm_ple_g[i].reshape(1, d), w_ple_gate[i].astype(BF16),
                      w_ple_proj[i].astype(BF16), norm_final_g.reshape(1, d))
    return x2.reshape(b, s, d)
```

```python
import functools
import math

import jax
import jax.numpy as jnp
from jax import lax
from jax.experimental import pallas as pl
from jax.experimental.pallas import tpu as pltpu

F32 = jnp.float32
BF16 = jnp.bfloat16
I32 = jnp.int32

D_MODEL = 1024
N_HEADS = 8
HEAD_DIM = 64
ATTN_WIDTH = N_HEADS * HEAD_DIM
MOBA_BLOCK = 256
MOBA_TOPK = 3
NUM_BUCKETS = 32
MAX_DISTANCE = 128
POOL_WINDOWS = (2, 4, 8, 16)
POOL_GROUPS = 4
POOL_GROUP_DIM = 128
POOL_WIDTH = POOL_GROUPS * POOL_GROUP_DIM
N_EXPERTS = 32
TOP_K = 4
SWIGLU_LIMIT = 7.0
SWIGLU_ALPHA = 1.702
RMS_EPS = 1e-6

LANES = 128
SUBLANES = 8
ROW_TILES = D_MODEL // LANES

IN_TM = 512
MIX_TM = 256
EXPERT_TM = 256
OUT_TM = 256
ATTN_GROUP = 4
POOL_HALO = 16
MASKED = -1e30
LOG2E = math.log2(math.e)
Q_SCALE = HEAD_DIM ** -0.5 * LOG2E

VMEM_LIMIT = 56 * 1024 * 1024

_NT = (((1,), (1,)), ((), ()))


def _rms(x, g):
    ms = jnp.mean(x * x, axis=-1, keepdims=True)
    return x * lax.rsqrt(ms + RMS_EPS) * g


def _sigmoid(x):
    return 1.0 / (1.0 + jnp.exp(-x))


def _in_proj_kernel(x_ref, g_ref, w_ref, q_ref, k_ref, v_ref, u_ref, gl_ref):
    h = _rms(x_ref[...], g_ref[...]).astype(BF16)
    aw = ATTN_WIDTH

    def proj(lo, hi):
        return jnp.dot(h, w_ref[:, lo:hi], preferred_element_type=F32)

    q_ref[...] = (proj(0, aw) * Q_SCALE).astype(BF16)
    k_ref[...] = proj(aw, 2 * aw).astype(BF16)
    v_ref[...] = proj(2 * aw, 3 * aw).astype(BF16)
    u_ref[...] = proj(3 * aw, 3 * aw + POOL_WIDTH)
    base = 3 * aw + POOL_WIDTH
    for c in range(2):
        lo = base + c * D_MODEL
        gl_ref[:, c * D_MODEL:(c + 1) * D_MODEL] = proj(lo, lo + D_MODEL).astype(BF16)


def _in_proj(x2, g, w_bf16):
    t = x2.shape[0]
    in_cols = w_bf16.shape[1]
    row = lambda i: (i, 0)
    fixed = lambda i: (0, 0)
    return pl.pallas_call(
        _in_proj_kernel,
        grid=(t // IN_TM,),
        in_specs=[
            pl.BlockSpec((IN_TM, D_MODEL), row),
            pl.BlockSpec((1, D_MODEL), fixed),
            pl.BlockSpec((D_MODEL, in_cols), fixed),
        ],
        out_specs=[
            pl.BlockSpec((IN_TM, ATTN_WIDTH), row),
            pl.BlockSpec((IN_TM, ATTN_WIDTH), row),
            pl.BlockSpec((IN_TM, ATTN_WIDTH), row),
            pl.BlockSpec((IN_TM, POOL_WIDTH), row),
            pl.BlockSpec((IN_TM, 2 * D_MODEL), row),
        ],
        out_shape=[
            jax.ShapeDtypeStruct((t, ATTN_WIDTH), BF16),
            jax.ShapeDtypeStruct((t, ATTN_WIDTH), BF16),
            jax.ShapeDtypeStruct((t, ATTN_WIDTH), BF16),
            jax.ShapeDtypeStruct((t, POOL_WIDTH), F32),
            jax.ShapeDtypeStruct((t, 2 * D_MODEL), BF16),
        ],
        compiler_params=pltpu.CompilerParams(
            dimension_semantics=("parallel",), vmem_limit_bytes=VMEM_LIMIT),
        name="in_proj",
    )(x2, g, w_bf16)


def _attn_kernel(bfar_ref, q_ref, k_ref, vt_ref, bown_ref, bprev_ref, o_ref,
                 kmean_s, am_s, sa_s, sb_s, ga_s, gb_s, m_s, l_s, acc_s, *, n_blocks):
    bh = pl.program_id(0)
    j = pl.program_id(1)
    blk = MOBA_BLOCK

    @pl.when(j == 0)
    def _():
        for n in range(n_blocks):
            kb = k_ref[0, n * blk:(n + 1) * blk, :].astype(F32)
            kmean_s[n:n + 1, :] = jnp.mean(kb, axis=0, keepdims=True)

    q = q_ref[0]

    gate = lax.dot_general(kmean_s[...], q.astype(F32), _NT,
                           precision=lax.Precision.HIGHEST, preferred_element_type=F32)
    n_iota = lax.broadcasted_iota(I32, gate.shape, 0)
    past = n_iota < j
    gate = jnp.where(past, gate, jnp.finfo(F32).min)
    sel = jnp.zeros(gate.shape, jnp.bool_)
    for _ in range(MOBA_TOPK):
        mx = jnp.max(gate, axis=0, keepdims=True)
        idx = jnp.min(jnp.where(gate == mx, n_iota, n_blocks), axis=0, keepdims=True)
        pick = n_iota == idx
        sel = jnp.logical_or(sel, pick)
        gate = jnp.where(pick, -jnp.inf, gate)
    sel = jnp.logical_and(sel, past)
    bfar = bfar_ref[bh % N_HEADS]
    am_s[...] = jnp.where(jnp.logical_and(sel, n_iota < j - 1), bfar, MASKED)
    am_prev = jnp.max(jnp.where(jnp.logical_and(sel, n_iota == j - 1), 0.0, MASKED),
                      axis=0, keepdims=True)

    grp = ATTN_GROUP
    gk = grp * blk
    n_groups = n_blocks // grp
    n_far = jnp.maximum(j - 1, 0)
    ng = (n_far + grp - 1) // grp

    def produce(g, s_ref, gmax_ref):
        g = jnp.minimum(g, n_groups - 1)
        kb = k_ref[0, pl.ds(pl.multiple_of(g * gk, gk), gk), :]
        s = lax.dot_general(kb, q, _NT, preferred_element_type=F32)
        s_ref[...] = s
        gmax = None
        for b in range(grp):
            mb = (jnp.max(s[b * blk:(b + 1) * blk], axis=0, keepdims=True)
                  + am_s[pl.ds(g * grp + b, 1), :])
            gmax = mb if gmax is None else jnp.maximum(gmax, mb)
        gmax_ref[...] = gmax

    def consume(g, s_ref, gmax_ref):
        m_old = m_s[...]
        m_new = jnp.maximum(m_old, gmax_ref[...])
        alpha = jnp.exp2(m_old - m_new)
        parts, lsum = [], None
        for b in range(grp):
            shift = m_new - am_s[pl.ds(g * grp + b, 1), :]
            pb = jnp.exp2(s_ref[b * blk:(b + 1) * blk, :] - shift)
            sb = jnp.sum(pb, axis=0, keepdims=True)
            lsum = sb if lsum is None else lsum + sb
            parts.append(pb.astype(BF16))
        p = jnp.concatenate(parts, axis=0)
        vt = vt_ref[0, :, pl.ds(pl.multiple_of(g * gk, gk), gk)]
        pv = jnp.dot(vt, p, preferred_element_type=F32)
        m_s[...] = m_new
        l_s[...] = alpha * l_s[...] + lsum
        acc_s[...] = alpha * acc_s[...] + pv

    produce(0, sa_s, ga_s)

    def near(n, bias):
        kb = k_ref[0, pl.ds(pl.multiple_of(n * blk, blk), blk), :]
        return lax.dot_general(kb, q, _NT, preferred_element_type=F32) + bias

    jp = jnp.maximum(j - 1, 0)
    s_own = near(j, bown_ref[0])
    s_prev = near(jp, bprev_ref[0])
    m = jnp.maximum(jnp.max(s_own, axis=0, keepdims=True),
                    jnp.max(s_prev, axis=0, keepdims=True) + am_prev)
    p_own = jnp.exp2(s_own - m)
    p_prev = jnp.exp2(s_prev - (m - am_prev))
    m_s[...] = m
    l_s[...] = jnp.sum(p_own, axis=0, keepdims=True) + jnp.sum(p_prev, axis=0, keepdims=True)

    def near_pv(n, p):
        vt = vt_ref[0, :, pl.ds(pl.multiple_of(n * blk, blk), blk)]
        return jnp.dot(vt, p.astype(BF16), preferred_element_type=F32)

    acc_s[...] = near_pv(j, p_own) + near_pv(jp, p_prev)

    def pair(h, carry):
        consume(2 * h, sa_s, ga_s)
        produce(2 * h + 1, sb_s, gb_s)
        consume(2 * h + 1, sb_s, gb_s)
        produce(2 * h + 2, sa_s, ga_s)
        return carry

    lax.fori_loop(0, ng // 2, pair, 0)

    @pl.when(ng % 2 == 1)
    def _():
        consume(ng - 1, sa_s, ga_s)

    o_ref[0] = (acc_s[...] / l_s[...]).astype(o_ref.dtype)


def _attention(qh, kh, vth, bown, bprev, bfar):
    bhn, s, dh = qh.shape
    nb = s // MOBA_BLOCK
    kern = functools.partial(_attn_kernel, n_blocks=nb)
    grid_spec = pltpu.PrefetchScalarGridSpec(
        num_scalar_prefetch=0,
        grid=(bhn, nb),
        in_specs=[
            pl.BlockSpec(memory_space=pltpu.SMEM),
            pl.BlockSpec((1, MOBA_BLOCK, dh), lambda b, j: (b, j, 0)),
            pl.BlockSpec((1, s, dh), lambda b, j: (b, 0, 0)),
            pl.BlockSpec((1, dh, s), lambda b, j: (b, 0, 0)),
            pl.BlockSpec((1, MOBA_BLOCK, MOBA_BLOCK), lambda b, j: (b % N_HEADS, 0, 0)),
            pl.BlockSpec((1, MOBA_BLOCK, MOBA_BLOCK), lambda b, j: (b % N_HEADS, 0, 0)),
        ],
        out_specs=pl.BlockSpec((1, dh, MOBA_BLOCK), lambda b, j: (b, 0, j)),
        scratch_shapes=[
            pltpu.VMEM((nb, dh), F32),
            pltpu.VMEM((nb, MOBA_BLOCK), F32),
            pltpu.VMEM((ATTN_GROUP * MOBA_BLOCK, MOBA_BLOCK), F32),
            pltpu.VMEM((ATTN_GROUP * MOBA_BLOCK, MOBA_BLOCK), F32),
            pltpu.VMEM((1, MOBA_BLOCK), F32),
            pltpu.VMEM((1, MOBA_BLOCK), F32),
            pltpu.VMEM((1, MOBA_BLOCK), F32),
            pltpu.VMEM((1, MOBA_BLOCK), F32),
            pltpu.VMEM((dh, MOBA_BLOCK), F32),
        ],
    )
    return pl.pallas_call(
        kern,
        grid_spec=grid_spec,
        out_shape=jax.ShapeDtypeStruct((bhn, dh, s), BF16),
        compiler_params=pltpu.CompilerParams(
            dimension_semantics=("parallel", "arbitrary"), vmem_limit_bytes=VMEM_LIMIT),
        name="moba_attention",
    )(bfar, qh, kh, vth, bown, bprev)


def _t5_bias_tables(rel_bias):
    blk = MOBA_BLOCK
    n = jnp.arange(2 * blk)
    max_exact = NUM_BUCKETS // 2
    nf = jnp.maximum(n, 1).astype(F32)
    large = max_exact + (jnp.log(nf / max_exact) / math.log(MAX_DISTANCE / max_exact)
                         * (NUM_BUCKETS - max_exact)).astype(I32)
    large = jnp.minimum(large, NUM_BUCKETS - 1)
    bucket = jnp.where(n < max_exact, n, large)
    rel_bias = rel_bias.astype(F32) * LOG2E
    tbl = rel_bias[bucket].T
    key = jnp.arange(blk)[:, None]
    qry = jnp.arange(blk)[None, :]
    d = qry - key
    h = tbl.shape[0]
    wide = 2 * blk + 1
    skew = jnp.broadcast_to(jnp.pad(tbl, ((0, 0), (0, 1)))[:, None, :], (h, blk, wide))
    skew = skew.reshape(h, blk * wide)[:, :blk * 2 * blk].reshape(h, blk, 2 * blk)
    bown = jnp.where(d >= 0, skew[:, :, :blk], MASKED)
    bprev = skew[:, :, blk:]
    bfar = rel_bias.astype(F32)[NUM_BUCKETS - 1]
    return bown, bprev, bfar


def _mix_kernel(x_ref, attn_ref, u_ref, halo_ref, gl_ref, pw_ref, ps_ref, woa_ref, wop_ref,
                wout_ref, gffn_ref, rwt_ref, rb_ref,
                x1_ref, hf_ref, idx_ref, rank_ref, gate_ref, cnt_ref,
                ext_s, carry_s, *, tiles_per_seq):
    i = pl.program_id(0)
    tm = MIX_TM

    @pl.when(i == 0)
    def _():
        carry_s[...] = jnp.zeros_like(carry_s)

    first = (i % tiles_per_seq) == 0
    ext_s[0:POOL_HALO, :] = jnp.where(first, 0.0, halo_ref[...])
    ext_s[POOL_HALO:POOL_HALO + tm, :] = u_ref[...]
    pos = (i % tiles_per_seq) * tm + lax.broadcasted_iota(I32, (tm, 1), 0)
    pooled_parts = []
    for g, w in enumerate(POOL_WINDOWS):
        c0, c1 = g * POOL_GROUP_DIM, (g + 1) * POOL_GROUP_DIM
        win = ext_s[POOL_HALO:POOL_HALO + tm, c0:c1]
        for sft in range(1, w):
            win = win + ext_s[POOL_HALO - sft:POOL_HALO - sft + tm, c0:c1]
        cnt = jnp.minimum(pos + 1, w).astype(F32)
        pin = win / cnt - ext_s[POOL_HALO:POOL_HALO + tm, c0:c1]
        pooled_parts.append(jnp.dot(pin.astype(BF16), pw_ref[g], preferred_element_type=F32))
    pooled = jnp.concatenate(pooled_parts, axis=1) * ps_ref[...]

    a = jnp.dot(attn_ref[...], woa_ref[...], preferred_element_type=F32)
    pm = jnp.dot(pooled.astype(BF16), wop_ref[...], preferred_element_type=F32)
    g0 = _sigmoid(gl_ref[:, 0:D_MODEL].astype(F32))
    g1 = _sigmoid(gl_ref[:, D_MODEL:2 * D_MODEL].astype(F32))
    merged = g0 * a + g1 * pm
    x1 = x_ref[...] + jnp.dot(merged.astype(BF16), wout_ref[...], preferred_element_type=F32)
    x1_ref[...] = x1

    hf = _rms(x1, gffn_ref[...])
    hfb = hf.astype(BF16)
    for s in range(ROW_TILES):
        hf_ref[pl.ds(s, tm, stride=ROW_TILES), :] = hfb[:, s * LANES:(s + 1) * LANES].astype(F32)

    logits = lax.dot_general(rwt_ref[...], hfb.astype(F32), _NT,
                             precision=lax.Precision.HIGHEST,
                             preferred_element_type=F32) + rb_ref[...]
    e_iota = lax.broadcasted_iota(I32, logits.shape, 0)
    vals, picks = [], []
    for _ in range(TOP_K):
        mx = jnp.max(logits, axis=0, keepdims=True)
        idx = jnp.min(jnp.where(logits == mx, e_iota, N_EXPERTS), axis=0, keepdims=True)
        pick = e_iota == idx
        vals.append(mx)
        picks.append(pick)
        idx_ref[len(picks) - 1:len(picks), :] = idx
        logits = jnp.where(pick, -jnp.inf, logits)
    ex = [jnp.exp(v - vals[0]) for v in vals]
    den = ex[0] + ex[1] + ex[2] + ex[3]
    for k in range(TOP_K):
        gate_ref[k:k + 1, :] = ex[k] / den

    onehot = jnp.zeros(e_iota.shape, F32)
    for pick in picks:
        onehot = onehot + pick.astype(F32)
    ra = lax.broadcasted_iota(I32, (tm, tm), 0)
    rb = lax.broadcasted_iota(I32, (tm, tm), 1)
    upper = (ra < rb).astype(BF16)
    before = jnp.dot(onehot.astype(BF16), upper, preferred_element_type=F32) + carry_s[...]
    for k, pick in enumerate(picks):
        rank_ref[k:k + 1, :] = jnp.sum(jnp.where(pick, before, 0.0), axis=0,
                                       keepdims=True).astype(I32)
    carry_s[...] = carry_s[...] + jnp.sum(onehot, axis=1, keepdims=True)
    cnt_ref[...] = jnp.broadcast_to(carry_s[...], cnt_ref.shape).astype(I32)


def _mix(x2, attn, u, gl, pw, ps, woa, wop, wout, gffn, rwt, rb, seq):
    t = x2.shape[0]
    tm = MIX_TM
    tiles_per_seq = seq // tm
    halo_per_tile = tm // POOL_HALO
    row = lambda i: (i, 0)
    fixed2 = lambda i: (0, 0)
    fixed3 = lambda i: (0, 0, 0)
    col = lambda i: (0, i)
    kern = functools.partial(_mix_kernel, tiles_per_seq=tiles_per_seq)
    return pl.pallas_call(
        kern,
        grid=(t // tm,),
        in_specs=[
            pl.BlockSpec((tm, D_MODEL), row),
            pl.BlockSpec((tm, ATTN_WIDTH), row),
            pl.BlockSpec((tm, POOL_WIDTH), row),
            pl.BlockSpec((POOL_HALO, POOL_WIDTH),
                         lambda i: (jnp.maximum(i * halo_per_tile - 1, 0), 0)),
            pl.BlockSpec((tm, 2 * D_MODEL), row),
            pl.BlockSpec((POOL_GROUPS, POOL_GROUP_DIM, POOL_GROUP_DIM), fixed3),
            pl.BlockSpec((1, POOL_WIDTH), fixed2),
            pl.BlockSpec((ATTN_WIDTH, D_MODEL), fixed2),
            pl.BlockSpec((POOL_WIDTH, D_MODEL), fixed2),
            pl.BlockSpec((D_MODEL, D_MODEL), fixed2),
            pl.BlockSpec((1, D_MODEL), fixed2),
            pl.BlockSpec((N_EXPERTS, D_MODEL), fixed2),
            pl.BlockSpec((N_EXPERTS, 1), fixed2),
        ],
        out_specs=[
            pl.BlockSpec((tm, D_MODEL), row),
            pl.BlockSpec((tm * ROW_TILES, LANES), row),
            pl.BlockSpec((TOP_K, tm), col),
            pl.BlockSpec((TOP_K, tm), col),
            pl.BlockSpec((TOP_K, tm), col),
            pl.BlockSpec((N_EXPERTS, LANES), fixed2),
        ],
        out_shape=[
            jax.ShapeDtypeStruct((t, D_MODEL), F32),
            jax.ShapeDtypeStruct((t * ROW_TILES, LANES), F32),
            jax.ShapeDtypeStruct((TOP_K, t), I32),
            jax.ShapeDtypeStruct((TOP_K, t), I32),
            jax.ShapeDtypeStruct((TOP_K, t), F32),
            jax.ShapeDtypeStruct((N_EXPERTS, LANES), I32),
        ],
        scratch_shapes=[
            pltpu.VMEM((POOL_HALO + tm, POOL_WIDTH), F32),
            pltpu.VMEM((N_EXPERTS, 1), F32),
        ],
        compiler_params=pltpu.CompilerParams(
            dimension_semantics=("arbitrary",), vmem_limit_bytes=VMEM_LIMIT),
        name="mix_router",
    )(x2, attn, u, u, gl, pw, ps, woa, wop, wout, gffn, rwt, rb)


def _row(ref, r):
    return ref.at[pl.ds(pl.multiple_of(r * ROW_TILES, ROW_TILES), ROW_TILES)]


def _pair_split_matrix():
    n = 2 * LANES
    r = lax.broadcasted_iota(I32, (n, n), 0)
    c = lax.broadcasted_iota(I32, (n, n), 1)
    src = jnp.where(c < LANES, 2 * c, 2 * (c - LANES) + 1)
    return (r == src).astype(BF16)


def _expert_kernel(be_ref, nused_ref, rowtok_ref, hf_ref, wgu_ref, wd_ref, bgu_ref, bd_ref,
                   y_ref, xbuf, wgu_s, wd_s, sems):
    i = pl.program_id(0)
    tm = EXPERT_TM
    n_used = nused_ref[0]
    slot = i % 2

    def gather(blk, to_slot, start):
        if start:
            def row(r, carry):
                tok = rowtok_ref[blk * tm + r]
                pltpu.make_async_copy(_row(hf_ref, tok), _row(xbuf.at[to_slot], r),
                                      sems.at[to_slot]).start()
                return carry
            lax.fori_loop(0, tm, row, 0)
        else:
            pltpu.make_async_copy(hf_ref.at[pl.ds(0, tm * ROW_TILES)], xbuf.at[to_slot],
                                  sems.at[to_slot]).wait()

    @pl.when(i == 0)
    def _():
        gather(0, 0, True)

    @pl.when(i + 1 < n_used)
    def _():
        gather(i + 1, 1 - slot, True)

    changed = jnp.logical_or(i == 0, be_ref[i] != be_ref[jnp.maximum(i - 1, 0)])

    @pl.when(jnp.logical_and(changed, i < n_used))
    def _():
        split = _pair_split_matrix()
        for c in range(2 * D_MODEL // (2 * LANES)):
            lo, hi = c * 2 * LANES, (c + 1) * 2 * LANES
            wgu_s[:, lo:hi] = jnp.dot(wgu_ref[0, :, lo:hi].astype(BF16), split,
                                      preferred_element_type=F32).astype(BF16)
        wd_s[...] = wd_ref[0].astype(BF16)

    @pl.when(i >= n_used)
    def _():
        y_ref[...] = jnp.zeros_like(y_ref)

    @pl.when(i < n_used)
    def _():
        gather(i, slot, False)
        xb = jnp.concatenate(
            [xbuf[slot, pl.ds(s, tm, stride=ROW_TILES), :] for s in range(ROW_TILES)],
            axis=1).astype(BF16)
        hid_parts = []
        for c in range(D_MODEL // LANES):
            lo, hi = c * 2 * LANES, (c + 1) * 2 * LANES
            gu = jnp.dot(xb, wgu_s[:, lo:hi], preferred_element_type=F32) + bgu_ref[0, :, lo:hi]
            glu = jnp.minimum(gu[:, :LANES], SWIGLU_LIMIT)
            lin = jnp.clip(gu[:, LANES:], -SWIGLU_LIMIT, SWIGLU_LIMIT)
            hid_parts.append(((lin + 1.0) * (glu * _sigmoid(SWIGLU_ALPHA * glu))).astype(BF16))
        hid = jnp.concatenate(hid_parts, axis=1)
        y = jnp.dot(hid, wd_s[...], preferred_element_type=F32) + bd_ref[0]
        for s in range(ROW_TILES):
            y_ref[pl.ds(s, tm, stride=ROW_TILES), :] = y[:, s * LANES:(s + 1) * LANES]


def _experts(blk_expert, n_used, row_tok, hf_rows, wgu, wd, bgu_split, bd):
    n_blk = blk_expert.shape[0]
    tm = EXPERT_TM

    def ymap(i, be, nu, rt):
        return (i, 0)

    def wmap(i, be, nu, rt):
        return (be[jnp.minimum(i, nu[0] - 1)], 0, 0)

    grid_spec = pltpu.PrefetchScalarGridSpec(
        num_scalar_prefetch=3,
        grid=(n_blk,),
        in_specs=[
            pl.BlockSpec(memory_space=pl.ANY),
            pl.BlockSpec((1, D_MODEL, 2 * D_MODEL), wmap),
            pl.BlockSpec((1, D_MODEL, D_MODEL), wmap),
            pl.BlockSpec((1, 1, 2 * D_MODEL), wmap),
            pl.BlockSpec((1, 1, D_MODEL), wmap),
        ],
        out_specs=pl.BlockSpec((tm * ROW_TILES, LANES), ymap),
        scratch_shapes=[
            pltpu.VMEM((2, tm * ROW_TILES, LANES), F32),
            pltpu.VMEM((D_MODEL, 2 * D_MODEL), BF16),
            pltpu.VMEM((D_MODEL, D_MODEL), BF16),
            pltpu.SemaphoreType.DMA((2,)),
        ],
    )
    return pl.pallas_call(
        _expert_kernel,
        grid_spec=grid_spec,
        out_shape=jax.ShapeDtypeStruct((n_blk * tm * ROW_TILES, LANES), F32),
        compiler_params=pltpu.CompilerParams(
            dimension_semantics=("arbitrary",), vmem_limit_bytes=VMEM_LIMIT),
        name="expert_ffn",
    )(blk_expert, n_used, row_tok, hf_rows, wgu, wd, bgu_split, bd)


def _combine_kernel(dest_ref, x1_ref, gate_ref, ys_ref, p_ref, gple_ref, wpg_ref, wpp_ref,
                    gfin_ref, o_ref, ybuf, sem):
    i = pl.program_id(0)
    tm = OUT_TM

    def tok(r, carry):
        t = i * tm + r
        for k in range(TOP_K):
            pltpu.make_async_copy(_row(ys_ref, dest_ref[t * TOP_K + k]),
                                  _row(ybuf, k * tm + r), sem).start()
        return carry

    lax.fori_loop(0, tm, tok, 0)
    pltpu.make_async_copy(ys_ref.at[pl.ds(0, TOP_K * tm * ROW_TILES)], ybuf, sem).wait()

    moe = None
    for k in range(TOP_K):
        yk = jnp.concatenate(
            [ybuf[pl.ds(k * tm * ROW_TILES + s, tm, stride=ROW_TILES), :]
             for s in range(ROW_TILES)], axis=1)
        term = gate_ref[:, k:k + 1] * yk
        moe = term if moe is None else moe + term
    x2 = x1_ref[...] + moe
    hp = _rms(x2, gple_ref[...]).astype(BF16)
    pg = _sigmoid(jnp.dot(hp, wpg_ref[...], preferred_element_type=F32))
    proj = jnp.dot(p_ref[...].astype(BF16), wpp_ref[...], preferred_element_type=F32)
    x3 = x2 + pg * proj
    o_ref[...] = _rms(x3, gfin_ref[...])


def _combine(dest_flat, x1, gates_tk, ys, p2, gple, wpg, wpp, gfin):
    t = x1.shape[0]
    tm = OUT_TM
    ple = p2.shape[1]
    row = lambda i, d: (i, 0)
    fixed = lambda i, d: (0, 0)
    grid_spec = pltpu.PrefetchScalarGridSpec(
        num_scalar_prefetch=1,
        grid=(t // tm,),
        in_specs=[
            pl.BlockSpec((tm, D_MODEL), row),
            pl.BlockSpec((tm, TOP_K), row),
            pl.BlockSpec(memory_space=pl.ANY),
            pl.BlockSpec((tm, ple), row),
            pl.BlockSpec((1, D_MODEL), fixed),
            pl.BlockSpec((D_MODEL, D_MODEL), fixed),
            pl.BlockSpec((ple, D_MODEL), fixed),
            pl.BlockSpec((1, D_MODEL), fixed),
        ],
        out_specs=pl.BlockSpec((tm, D_MODEL), row),
        scratch_shapes=[
            pltpu.VMEM((TOP_K * tm * ROW_TILES, LANES), F32),
            pltpu.SemaphoreType.DMA,
        ],
    )
    return pl.pallas_call(
        _combine_kernel,
        grid_spec=grid_spec,
        out_shape=jax.ShapeDtypeStruct((t, D_MODEL), F32),
        compiler_params=pltpu.CompilerParams(
            dimension_semantics=("arbitrary",), vmem_limit_bytes=VMEM_LIMIT),
        name="combine_ple_final",
    )(dest_flat, x1, gates_tk, ys, p2, gple, wpg, wpp, gfin)


def kernel(x, p, rel_bias, norm_mix_g, w_in, pool_w, pool_scale, w_o_attn, w_o_pool, w_out,
           norm_ffn_g, router_w, router_b, w_gate_up, b_gate_up, w_down, b_down,
           norm_ple_g, w_ple_gate, w_ple_proj, norm_final_g):
    b, s, d = x.shape
    depth = w_in.shape[0]
    t = b * s
    assert d == D_MODEL and s % MOBA_BLOCK == 0 and t % IN_TM == 0 and s % MIX_TM == 0
    assert depth == 1, "the final norm is fused into the layer's last kernel"
    n_pad = t * TOP_K + N_EXPERTS * EXPERT_TM
    n_blk = n_pad // EXPERT_TM
    bown, bprev, bfar = _t5_bias_tables(rel_bias)

    x2 = x.reshape(t, d)
    for i in range(depth):
        q, k, v, u, gl = _in_proj(x2, norm_mix_g[i].reshape(1, d), w_in[i].astype(BF16))

        def heads(a):
            return a.reshape(b, s, N_HEADS, HEAD_DIM).transpose(0, 2, 1, 3).reshape(
                b * N_HEADS, s, HEAD_DIM)

        vth = v.reshape(b, s, N_HEADS, HEAD_DIM).transpose(0, 2, 3, 1).reshape(
            b * N_HEADS, HEAD_DIM, s)
        attn_t = _attention(heads(q), heads(k), vth, bown, bprev, bfar)
        attn = attn_t.reshape(b, N_HEADS, HEAD_DIM, s).transpose(0, 3, 1, 2).reshape(
            t, ATTN_WIDTH)

        x1, hf_rows, idx_kt, rank_kt, gate_kt, cnt = _mix(
            x2, attn, u, gl, pool_w[i].astype(BF16), pool_scale[i].reshape(1, POOL_WIDTH),
            w_o_attn[i].astype(BF16), w_o_pool[i].astype(BF16), w_out[i].astype(BF16),
            norm_ffn_g[i].reshape(1, d), router_w[i].T, router_b[i].reshape(N_EXPERTS, 1), s)

        counts = cnt[:, 0]
        padded = (counts + EXPERT_TM - 1) // EXPERT_TM * EXPERT_TM
        pend = jnp.cumsum(padded)
        pstart = pend - padded
        e_ids = jnp.arange(N_EXPERTS, dtype=I32)
        pstart_of = jnp.sum(jnp.where(idx_kt[..., None] == e_ids, pstart, 0), axis=-1)
        dest_flat = (pstart_of + rank_kt).T.reshape(-1).astype(I32)
        blk_row0 = jnp.arange(n_blk, dtype=I32) * EXPERT_TM
        blk_expert = jnp.minimum(jnp.sum(pend[None, :] <= blk_row0[:, None], axis=1),
                                 N_EXPERTS - 1).astype(I32)
        n_used = (pend[-1:] // EXPERT_TM).astype(I32)
        row_tok = jnp.zeros((n_pad,), I32).at[dest_flat].set(
            jnp.arange(t * TOP_K, dtype=I32) // TOP_K, unique_indices=True)

        bgu = b_gate_up[i].reshape(N_EXPERTS, D_MODEL // LANES, LANES, 2).transpose(
            0, 1, 3, 2).reshape(N_EXPERTS, 1, 2 * D_MODEL)
        ys = _experts(blk_expert, n_used, row_tok, hf_rows, w_gate_up[i], w_down[i], bgu,
                      b_down[i][:, None, :])

        x2 = _combine(dest_flat, x1, gate_kt.T, ys, p[i].reshape(t, -1),
                      norm_ple_g[i].reshape(1, d), w_ple_gate[i].astype(BF16),
                      w_ple_proj[i].astype(BF16), norm_final_g.reshape(1, d))
    return x2.reshape(b, s, d)
```

```python
import functools
import math

import jax
import jax.numpy as jnp
from jax import lax
from jax.experimental import pallas as pl
from jax.experimental.pallas import tpu as pltpu

F32 = jnp.float32
BF16 = jnp.bfloat16
I32 = jnp.int32

D_MODEL = 1024
N_HEADS = 8
HEAD_DIM = 64
ATTN_WIDTH = N_HEADS * HEAD_DIM
MOBA_BLOCK = 256
MOBA_TOPK = 3
NUM_BUCKETS = 32
MAX_DISTANCE = 128
POOL_WINDOWS = (2, 4, 8, 16)
POOL_GROUPS = 4
POOL_GROUP_DIM = 128
POOL_WIDTH = POOL_GROUPS * POOL_GROUP_DIM
N_EXPERTS = 32
TOP_K = 4
SWIGLU_LIMIT = 7.0
SWIGLU_ALPHA = 1.702
RMS_EPS = 1e-6

LANES = 128
SUBLANES = 8
ROW_TILES = D_MODEL // LANES

IN_TM = 512
MIX_TM = 256
EXPERT_TM = 256
OUT_TM = 256
ATTN_GROUP = 4
ATTN_HEADS = 2
GATHER_CHUNK = 32
POOL_HALO = 16
MASKED = -1e30
LOG2E = math.log2(math.e)
Q_SCALE = HEAD_DIM ** -0.5 * LOG2E

VMEM_LIMIT = 56 * 1024 * 1024

_NT = (((1,), (1,)), ((), ()))


def _rms(x, g):
    ms = jnp.mean(x * x, axis=-1, keepdims=True)
    return x * lax.rsqrt(ms + RMS_EPS) * g


def _sigmoid(x):
    return 1.0 / (1.0 + jnp.exp(-x))


def _in_proj_kernel(x_ref, g_ref, w_ref, q_ref, k_ref, v_ref, u_ref, gl_ref):
    h = _rms(x_ref[...], g_ref[...]).astype(BF16)
    aw = ATTN_WIDTH

    def proj(lo, hi):
        return jnp.dot(h, w_ref[:, lo:hi], preferred_element_type=F32)

    q_ref[...] = (proj(0, aw) * Q_SCALE).astype(BF16)
    k_ref[...] = proj(aw, 2 * aw).astype(BF16)
    v_ref[...] = proj(2 * aw, 3 * aw).astype(BF16)
    u_ref[...] = proj(3 * aw, 3 * aw + POOL_WIDTH)
    base = 3 * aw + POOL_WIDTH
    for c in range(2):
        lo = base + c * D_MODEL
        gl_ref[:, c * D_MODEL:(c + 1) * D_MODEL] = proj(lo, lo + D_MODEL).astype(BF16)


def _in_proj(x2, g, w_bf16):
    t = x2.shape[0]
    in_cols = w_bf16.shape[1]
    row = lambda i: (i, 0)
    fixed = lambda i: (0, 0)
    return pl.pallas_call(
        _in_proj_kernel,
        grid=(t // IN_TM,),
        in_specs=[
            pl.BlockSpec((IN_TM, D_MODEL), row),
            pl.BlockSpec((1, D_MODEL), fixed),
            pl.BlockSpec((D_MODEL, in_cols), fixed),
        ],
        out_specs=[
            pl.BlockSpec((IN_TM, ATTN_WIDTH), row),
            pl.BlockSpec((IN_TM, ATTN_WIDTH), row),
            pl.BlockSpec((IN_TM, ATTN_WIDTH), row),
            pl.BlockSpec((IN_TM, POOL_WIDTH), row),
            pl.BlockSpec((IN_TM, 2 * D_MODEL), row),
        ],
        out_shape=[
            jax.ShapeDtypeStruct((t, ATTN_WIDTH), BF16),
            jax.ShapeDtypeStruct((t, ATTN_WIDTH), BF16),
            jax.ShapeDtypeStruct((t, ATTN_WIDTH), BF16),
            jax.ShapeDtypeStruct((t, POOL_WIDTH), F32),
            jax.ShapeDtypeStruct((t, 2 * D_MODEL), BF16),
        ],
        compiler_params=pltpu.CompilerParams(
            dimension_semantics=("parallel",), vmem_limit_bytes=VMEM_LIMIT),
        name="in_proj",
    )(x2, g, w_bf16)


def _attn_kernel(bfar_ref, q_ref, k_ref, vt_ref, bown_ref, bprev_ref, o_ref,
                 kmean_s, am_s, sa_s, sb_s, ga_s, gb_s, m_s, l_s, acc_s, *, n_blocks):
    hp = pl.program_id(0)
    j = pl.program_id(1)
    blk = MOBA_BLOCK
    grp = ATTN_GROUP
    gk = grp * blk
    n_groups = n_blocks // grp
    n_far = jnp.maximum(j - 1, 0)
    ng = (n_far + grp - 1) // grp
    jp = jnp.maximum(j - 1, 0)
    heads = range(ATTN_HEADS)

    @pl.when(j == 0)
    def _():
        for hh in heads:
            for n in range(n_blocks):
                kb = k_ref[hh, n * blk:(n + 1) * blk, :].astype(F32)
                kmean_s[hh, n:n + 1, :] = jnp.mean(kb, axis=0, keepdims=True)

    qs = [q_ref[hh] for hh in heads]

    def select(hh):
        gate = lax.dot_general(kmean_s[hh], qs[hh].astype(F32), _NT,
                               precision=lax.Precision.HIGHEST, preferred_element_type=F32)
        n_iota = lax.broadcasted_iota(I32, gate.shape, 0)
        past = n_iota < j
        gate = jnp.where(past, gate, jnp.finfo(F32).min)
        sel = jnp.zeros(gate.shape, jnp.bool_)
        for _ in range(MOBA_TOPK):
            mx = jnp.max(gate, axis=0, keepdims=True)
            idx = jnp.min(jnp.where(gate == mx, n_iota, n_blocks), axis=0, keepdims=True)
            pick = n_iota == idx
            sel = jnp.logical_or(sel, pick)
            gate = jnp.where(pick, -jnp.inf, gate)
        sel = jnp.logical_and(sel, past)
        bfar = bfar_ref[(hp * ATTN_HEADS + hh) % N_HEADS]
        am_s[hh] = jnp.where(jnp.logical_and(sel, n_iota < j - 1), bfar, MASKED)
        return jnp.max(jnp.where(jnp.logical_and(sel, n_iota == j - 1), 0.0, MASKED),
                       axis=0, keepdims=True)

    def produce(hh, g, s_ref, gmax_ref):
        g = jnp.minimum(g, n_groups - 1)
        kb = k_ref[hh, pl.ds(pl.multiple_of(g * gk, gk), gk), :]
        s = lax.dot_general(kb, qs[hh], _NT, preferred_element_type=F32)
        s_ref[hh] = s
        gmax = None
        for b in range(grp):
            mb = (jnp.max(s[b * blk:(b + 1) * blk], axis=0, keepdims=True)
                  + am_s[hh, pl.ds(g * grp + b, 1), :])
            gmax = mb if gmax is None else jnp.maximum(gmax, mb)
        gmax_ref[hh] = gmax

    def consume(hh, g, s_ref, gmax_ref):
        m_old = m_s[hh]
        m_new = jnp.maximum(m_old, gmax_ref[hh])
        alpha = jnp.exp2(m_old - m_new)
        parts, lsum = [], None
        for b in range(grp):
            shift = m_new - am_s[hh, pl.ds(g * grp + b, 1), :]
            pb = jnp.exp2(s_ref[hh, b * blk:(b + 1) * blk, :] - shift)
            sb = jnp.sum(pb, axis=0, keepdims=True)
            lsum = sb if lsum is None else lsum + sb
            parts.append(pb.astype(BF16))
        p = jnp.concatenate(parts, axis=0)
        vt = vt_ref[hh, :, pl.ds(pl.multiple_of(g * gk, gk), gk)]
        pv = jnp.dot(vt, p, preferred_element_type=F32)
        m_s[hh] = m_new
        l_s[hh] = alpha * l_s[hh] + lsum
        acc_s[hh] = alpha * acc_s[hh] + pv

    def near_scores(hh, n, bias):
        kb = k_ref[hh, pl.ds(pl.multiple_of(n * blk, blk), blk), :]
        return lax.dot_general(kb, qs[hh], _NT, preferred_element_type=F32) + bias

    def near_pv(hh, n, p):
        vt = vt_ref[hh, :, pl.ds(pl.multiple_of(n * blk, blk), blk)]
        return jnp.dot(vt, p.astype(BF16), preferred_element_type=F32)

    for hh in heads:
        am_prev = select(hh)
        produce(hh, 0, sa_s, ga_s)
        s_own = near_scores(hh, j, bown_ref[hh])
        s_prev = near_scores(hh, jp, bprev_ref[hh])
        m = jnp.maximum(jnp.max(s_own, axis=0, keepdims=True),
                        jnp.max(s_prev, axis=0, keepdims=True) + am_prev)
        p_own = jnp.exp2(s_own - m)
        p_prev = jnp.exp2(s_prev - (m - am_prev))
        m_s[hh] = m
        l_s[hh] = jnp.sum(p_own, axis=0, keepdims=True) + jnp.sum(p_prev, axis=0, keepdims=True)
        acc_s[hh] = near_pv(hh, j, p_own) + near_pv(hh, jp, p_prev)

    def pair(h, carry):
        for hh in heads:
            consume(hh, 2 * h, sa_s, ga_s)
            produce(hh, 2 * h + 1, sb_s, gb_s)
        for hh in heads:
            consume(hh, 2 * h + 1, sb_s, gb_s)
            produce(hh, 2 * h + 2, sa_s, ga_s)
        return carry

    lax.fori_loop(0, ng // 2, pair, 0)

    @pl.when(ng % 2 == 1)
    def _():
        for hh in heads:
            consume(hh, ng - 1, sa_s, ga_s)

    for hh in heads:
        o_ref[hh] = (acc_s[hh] / l_s[hh]).astype(o_ref.dtype)


def _attention(qh, kh, vth, bown, bprev, bfar):
    bhn, s, dh = qh.shape
    nb = s // MOBA_BLOCK
    ah = ATTN_HEADS
    assert N_HEADS % ah == 0 and nb % ATTN_GROUP == 0
    kern = functools.partial(_attn_kernel, n_blocks=nb)
    head_blk = lambda b, j: (b % (N_HEADS // ah), 0, 0)
    grid_spec = pltpu.PrefetchScalarGridSpec(
        num_scalar_prefetch=0,
        grid=(bhn // ah, nb),
        in_specs=[
            pl.BlockSpec(memory_space=pltpu.SMEM),
            pl.BlockSpec((ah, MOBA_BLOCK, dh), lambda b, j: (b, j, 0)),
            pl.BlockSpec((ah, s, dh), lambda b, j: (b, 0, 0)),
            pl.BlockSpec((ah, dh, s), lambda b, j: (b, 0, 0)),
            pl.BlockSpec((ah, MOBA_BLOCK, MOBA_BLOCK), head_blk),
            pl.BlockSpec((ah, MOBA_BLOCK, MOBA_BLOCK), head_blk),
        ],
        out_specs=pl.BlockSpec((ah, dh, MOBA_BLOCK), lambda b, j: (b, 0, j)),
        scratch_shapes=[
            pltpu.VMEM((ah, nb, dh), F32),
            pltpu.VMEM((ah, nb, MOBA_BLOCK), F32),
            pltpu.VMEM((ah, ATTN_GROUP * MOBA_BLOCK, MOBA_BLOCK), F32),
            pltpu.VMEM((ah, ATTN_GROUP * MOBA_BLOCK, MOBA_BLOCK), F32),
            pltpu.VMEM((ah, 1, MOBA_BLOCK), F32),
            pltpu.VMEM((ah, 1, MOBA_BLOCK), F32),
            pltpu.VMEM((ah, 1, MOBA_BLOCK), F32),
            pltpu.VMEM((ah, 1, MOBA_BLOCK), F32),
            pltpu.VMEM((ah, dh, MOBA_BLOCK), F32),
        ],
    )
    return pl.pallas_call(
        kern,
        grid_spec=grid_spec,
        out_shape=jax.ShapeDtypeStruct((bhn, dh, s), BF16),
        compiler_params=pltpu.CompilerParams(
            dimension_semantics=("parallel", "arbitrary"), vmem_limit_bytes=VMEM_LIMIT),
        name="moba_attention",
    )(bfar, qh, kh, vth, bown, bprev)


def _t5_bias_tables(rel_bias):
    blk = MOBA_BLOCK
    n = jnp.arange(2 * blk)
    max_exact = NUM_BUCKETS // 2
    nf = jnp.maximum(n, 1).astype(F32)
    large = max_exact + (jnp.log(nf / max_exact) / math.log(MAX_DISTANCE / max_exact)
                         * (NUM_BUCKETS - max_exact)).astype(I32)
    large = jnp.minimum(large, NUM_BUCKETS - 1)
    bucket = jnp.where(n < max_exact, n, large)
    rel_bias = rel_bias.astype(F32) * LOG2E
    tbl = rel_bias[bucket].T
    key = jnp.arange(blk)[:, None]
    qry = jnp.arange(blk)[None, :]
    d = qry - key
    h = tbl.shape[0]
    wide = 2 * blk + 1
    skew = jnp.broadcast_to(jnp.pad(tbl, ((0, 0), (0, 1)))[:, None, :], (h, blk, wide))
    skew = skew.reshape(h, blk * wide)[:, :blk * 2 * blk].reshape(h, blk, 2 * blk)
    bown = jnp.where(d >= 0, skew[:, :, :blk], MASKED)
    bprev = skew[:, :, blk:]
    bfar = rel_bias[NUM_BUCKETS - 1]
    return bown, bprev, bfar


def _mix_kernel(x_ref, attn_ref, u_ref, halo_ref, gl_ref, pw_ref, ps_ref, woa_ref, wop_ref,
                wout_ref, gffn_ref, rwt_ref, rb_ref,
                x1_ref, hf_ref, idx_ref, rank_ref, gate_ref, cnt_ref,
                ext_s, carry_s, *, tiles_per_seq):
    i = pl.program_id(0)
    tm = MIX_TM

    @pl.when(i == 0)
    def _():
        carry_s[...] = jnp.zeros_like(carry_s)

    first = (i % tiles_per_seq) == 0
    ext_s[0:POOL_HALO, :] = jnp.where(first, 0.0, halo_ref[...])
    ext_s[POOL_HALO:POOL_HALO + tm, :] = u_ref[...]
    pos = (i % tiles_per_seq) * tm + lax.broadcasted_iota(I32, (tm, 1), 0)
    pooled_parts = []
    for g, w in enumerate(POOL_WINDOWS):
        c0, c1 = g * POOL_GROUP_DIM, (g + 1) * POOL_GROUP_DIM
        win = ext_s[POOL_HALO:POOL_HALO + tm, c0:c1]
        for sft in range(1, w):
            win = win + ext_s[POOL_HALO - sft:POOL_HALO - sft + tm, c0:c1]
        cnt = jnp.minimum(pos + 1, w).astype(F32)
        pin = win / cnt - ext_s[POOL_HALO:POOL_HALO + tm, c0:c1]
        pooled_parts.append(jnp.dot(pin.astype(BF16), pw_ref[g], preferred_element_type=F32))
    pooled = jnp.concatenate(pooled_parts, axis=1) * ps_ref[...]

    a = jnp.dot(attn_ref[...], woa_ref[...], preferred_element_type=F32)
    pm = jnp.dot(pooled.astype(BF16), wop_ref[...], preferred_element_type=F32)
    g0 = _sigmoid(gl_ref[:, 0:D_MODEL].astype(F32))
    g1 = _sigmoid(gl_ref[:, D_MODEL:2 * D_MODEL].astype(F32))
    merged = g0 * a + g1 * pm
    x1 = x_ref[...] + jnp.dot(merged.astype(BF16), wout_ref[...], preferred_element_type=F32)
    x1_ref[...] = x1

    hf = _rms(x1, gffn_ref[...])
    hfb = hf.astype(BF16)
    for s in range(ROW_TILES):
        hf_ref[pl.ds(s, tm, stride=ROW_TILES), :] = hfb[:, s * LANES:(s + 1) * LANES].astype(F32)

    logits = lax.dot_general(rwt_ref[...], hfb.astype(F32), _NT,
                             precision=lax.Precision.HIGHEST,
                             preferred_element_type=F32) + rb_ref[...]
    e_iota = lax.broadcasted_iota(I32, logits.shape, 0)
    vals, picks = [], []
    for _ in range(TOP_K):
        mx = jnp.max(logits, axis=0, keepdims=True)
        idx = jnp.min(jnp.where(logits == mx, e_iota, N_EXPERTS), axis=0, keepdims=True)
        pick = e_iota == idx
        vals.append(mx)
        picks.append(pick)
        idx_ref[len(picks) - 1:len(picks), :] = idx
        logits = jnp.where(pick, -jnp.inf, logits)
    ex = [jnp.exp(v - vals[0]) for v in vals]
    den = ex[0] + ex[1] + ex[2] + ex[3]
    for k in range(TOP_K):
        gate_ref[k:k + 1, :] = ex[k] / den

    onehot = jnp.zeros(e_iota.shape, F32)
    for pick in picks:
        onehot = onehot + pick.astype(F32)
    ra = lax.broadcasted_iota(I32, (tm, tm), 0)
    rb = lax.broadcasted_iota(I32, (tm, tm), 1)
    upper = (ra < rb).astype(BF16)
    before = jnp.dot(onehot.astype(BF16), upper, preferred_element_type=F32) + carry_s[...]
    for k, pick in enumerate(picks):
        rank_ref[k:k + 1, :] = jnp.sum(jnp.where(pick, before, 0.0), axis=0,
                                       keepdims=True).astype(I32)
    carry_s[...] = carry_s[...] + jnp.sum(onehot, axis=1, keepdims=True)
    cnt_ref[...] = jnp.broadcast_to(carry_s[...], cnt_ref.shape).astype(I32)


def _mix(x2, attn, u, gl, pw, ps, woa, wop, wout, gffn, rwt, rb, seq):
    t = x2.shape[0]
    tm = MIX_TM
    tiles_per_seq = seq // tm
    halo_per_tile = tm // POOL_HALO
    row = lambda i: (i, 0)
    fixed2 = lambda i: (0, 0)
    fixed3 = lambda i: (0, 0, 0)
    col = lambda i: (0, i)
    kern = functools.partial(_mix_kernel, tiles_per_seq=tiles_per_seq)
    return pl.pallas_call(
        kern,
        grid=(t // tm,),
        in_specs=[
            pl.BlockSpec((tm, D_MODEL), row),
            pl.BlockSpec((tm, ATTN_WIDTH), row),
            pl.BlockSpec((tm, POOL_WIDTH), row),
            pl.BlockSpec((POOL_HALO, POOL_WIDTH),
                         lambda i: (jnp.maximum(i * halo_per_tile - 1, 0), 0)),
            pl.BlockSpec((tm, 2 * D_MODEL), row),
            pl.BlockSpec((POOL_GROUPS, POOL_GROUP_DIM, POOL_GROUP_DIM), fixed3),
            pl.BlockSpec((1, POOL_WIDTH), fixed2),
            pl.BlockSpec((ATTN_WIDTH, D_MODEL), fixed2),
            pl.BlockSpec((POOL_WIDTH, D_MODEL), fixed2),
            pl.BlockSpec((D_MODEL, D_MODEL), fixed2),
            pl.BlockSpec((1, D_MODEL), fixed2),
            pl.BlockSpec((N_EXPERTS, D_MODEL), fixed2),
            pl.BlockSpec((N_EXPERTS, 1), fixed2),
        ],
        out_specs=[
            pl.BlockSpec((tm, D_MODEL), row),
            pl.BlockSpec((tm * ROW_TILES, LANES), row),
            pl.BlockSpec((TOP_K, tm), col),
            pl.BlockSpec((TOP_K, tm), col),
            pl.BlockSpec((TOP_K, tm), col),
            pl.BlockSpec((N_EXPERTS, LANES), fixed2),
        ],
        out_shape=[
            jax.ShapeDtypeStruct((t, D_MODEL), F32),
            jax.ShapeDtypeStruct((t * ROW_TILES, LANES), F32),
            jax.ShapeDtypeStruct((TOP_K, t), I32),
            jax.ShapeDtypeStruct((TOP_K, t), I32),
            jax.ShapeDtypeStruct((TOP_K, t), F32),
            jax.ShapeDtypeStruct((N_EXPERTS, LANES), I32),
        ],
        scratch_shapes=[
            pltpu.VMEM((POOL_HALO + tm, POOL_WIDTH), F32),
            pltpu.VMEM((N_EXPERTS, 1), F32),
        ],
        compiler_params=pltpu.CompilerParams(
            dimension_semantics=("arbitrary",), vmem_limit_bytes=VMEM_LIMIT),
        name="mix_router",
    )(x2, attn, u, u, gl, pw, ps, woa, wop, wout, gffn, rwt, rb)


def _row(ref, r):
    return ref.at[pl.ds(pl.multiple_of(r * ROW_TILES, ROW_TILES), ROW_TILES)]


def _pair_split_matrix():
    n = 2 * LANES
    r = lax.broadcasted_iota(I32, (n, n), 0)
    c = lax.broadcasted_iota(I32, (n, n), 1)
    src = jnp.where(c < LANES, 2 * c, 2 * (c - LANES) + 1)
    return (r == src).astype(BF16)


def _expert_kernel(be_ref, nused_ref, rowsrc_ref, hf_ref, wgu_ref, wd_ref, bgu_ref, bd_ref,
                   y_ref, xbuf, wgu_s, wd_s, sems):
    i = pl.program_id(0)
    tm = EXPERT_TM
    n_used = nused_ref[0]
    slot = i % 2

    def start_rows(blk, to_slot, r):
        src = pl.multiple_of(rowsrc_ref[blk * tm + r], ROW_TILES)
        pltpu.make_async_copy(hf_ref.at[pl.ds(src, ROW_TILES)], _row(xbuf.at[to_slot], r),
                              sems.at[to_slot]).start()

    def wait_rows(at_slot):
        pltpu.make_async_copy(hf_ref.at[pl.ds(0, tm * ROW_TILES)], xbuf.at[at_slot],
                              sems.at[at_slot]).wait()

    @pl.when(i == 0)
    def _():
        def row(r, carry):
            start_rows(0, 0, r)
            return carry
        lax.fori_loop(0, tm, row, 0)

    changed = jnp.logical_or(i == 0, be_ref[i] != be_ref[jnp.maximum(i - 1, 0)])

    @pl.when(jnp.logical_and(changed, i < n_used))
    def _():
        split = _pair_split_matrix()
        for c in range(2 * D_MODEL // (2 * LANES)):
            lo, hi = c * 2 * LANES, (c + 1) * 2 * LANES
            wgu_s[:, lo:hi] = jnp.dot(wgu_ref[0, :, lo:hi].astype(BF16), split,
                                      preferred_element_type=F32).astype(BF16)
        wd_s[...] = wd_ref[0].astype(BF16)

    @pl.when(i >= n_used)
    def _():
        y_ref[...] = jnp.zeros_like(y_ref)

    @pl.when(i == n_used)
    def _():
        wait_rows(slot)

    @pl.when(i < n_used)
    def _():
        wait_rows(slot)
        xb = jnp.concatenate(
            [xbuf[slot, pl.ds(s, tm, stride=ROW_TILES), :] for s in range(ROW_TILES)],
            axis=1).astype(BF16)
        hid_parts = []
        n_chunks = D_MODEL // LANES
        assert n_chunks * GATHER_CHUNK == tm
        for c in range(n_chunks):
            lo, hi = c * 2 * LANES, (c + 1) * 2 * LANES
            gu = jnp.dot(xb, wgu_s[:, lo:hi], preferred_element_type=F32) + bgu_ref[0, :, lo:hi]
            glu = jnp.minimum(gu[:, :LANES], SWIGLU_LIMIT)
            lin = jnp.clip(gu[:, LANES:], -SWIGLU_LIMIT, SWIGLU_LIMIT)
            hid_parts.append(((lin + 1.0) * (glu * _sigmoid(SWIGLU_ALPHA * glu))).astype(BF16))
            for r in range(c * GATHER_CHUNK, (c + 1) * GATHER_CHUNK):
                start_rows(i + 1, 1 - slot, r)
        hid = jnp.concatenate(hid_parts, axis=1)
        y = jnp.dot(hid, wd_s[...], preferred_element_type=F32) + bd_ref[0]
        for s in range(ROW_TILES):
            y_ref[pl.ds(s, tm, stride=ROW_TILES), :] = y[:, s * LANES:(s + 1) * LANES]


def _experts(blk_expert, n_used, row_src, hf_rows, wgu, wd, bgu_split, bd):
    n_blk = blk_expert.shape[0]
    tm = EXPERT_TM

    def ymap(i, be, nu, rt):
        return (i, 0)

    def wmap(i, be, nu, rt):
        return (be[jnp.minimum(i, nu[0] - 1)], 0, 0)

    grid_spec = pltpu.PrefetchScalarGridSpec(
        num_scalar_prefetch=3,
        grid=(n_blk,),
        in_specs=[
            pl.BlockSpec(memory_space=pl.ANY),
            pl.BlockSpec((1, D_MODEL, 2 * D_MODEL), wmap),
            pl.BlockSpec((1, D_MODEL, D_MODEL), wmap),
            pl.BlockSpec((1, 1, 2 * D_MODEL), wmap),
            pl.BlockSpec((1, 1, D_MODEL), wmap),
        ],
        out_specs=pl.BlockSpec((tm * ROW_TILES, LANES), ymap),
        scratch_shapes=[
            pltpu.VMEM((2, tm * ROW_TILES, LANES), F32),
            pltpu.VMEM((D_MODEL, 2 * D_MODEL), BF16),
            pltpu.VMEM((D_MODEL, D_MODEL), BF16),
            pltpu.SemaphoreType.DMA((2,)),
        ],
    )
    return pl.pallas_call(
        _expert_kernel,
        grid_spec=grid_spec,
        out_shape=jax.ShapeDtypeStruct((n_blk * tm * ROW_TILES, LANES), F32),
        compiler_params=pltpu.CompilerParams(
            dimension_semantics=("arbitrary",), vmem_limit_bytes=VMEM_LIMIT),
        name="expert_ffn",
    )(blk_expert, n_used, row_src, hf_rows, wgu, wd, bgu_split, bd)


def _combine_kernel(dest_ref, x1_ref, gate_ref, ys_ref, p_ref, gple_ref, wpg_ref, wpp_ref,
                    gfin_ref, o_ref, ybuf, sems):
    i = pl.program_id(0)
    n_steps = pl.num_programs(0)
    tm = OUT_TM
    slot = i % 2

    def gather(tile, to_slot):
        def tok(r, carry):
            t = tile * tm + r
            for k in range(TOP_K):
                pltpu.async_copy(_row(ys_ref, dest_ref[t * TOP_K + k]),
                                 _row(ybuf.at[to_slot], k * tm + r), sems.at[to_slot],
                                 priority=k % 2)
            return carry
        lax.fori_loop(0, tm, tok, 0)

    @pl.when(i == 0)
    def _():
        gather(0, 0)

    @pl.when(i + 1 < n_steps)
    def _():
        gather(i + 1, 1 - slot)

    pltpu.make_async_copy(ys_ref.at[pl.ds(0, TOP_K * tm * ROW_TILES)], ybuf.at[slot],
                          sems.at[slot]).wait()

    moe = None
    for k in range(TOP_K):
        yk = jnp.concatenate(
            [ybuf[slot, pl.ds(k * tm * ROW_TILES + s, tm, stride=ROW_TILES), :]
             for s in range(ROW_TILES)], axis=1)
        term = gate_ref[:, k:k + 1] * yk
        moe = term if moe is None else moe + term
    x2 = x1_ref[...] + moe
    hp = _rms(x2, gple_ref[...]).astype(BF16)
    pg = _sigmoid(jnp.dot(hp, wpg_ref[...], preferred_element_type=F32))
    proj = jnp.dot(p_ref[...].astype(BF16), wpp_ref[...], preferred_element_type=F32)
    x3 = x2 + pg * proj
    o_ref[...] = _rms(x3, gfin_ref[...])


def _combine(dest_flat, x1, gates_tk, ys, p2, gple, wpg, wpp, gfin):
    t = x1.shape[0]
    tm = OUT_TM
    ple = p2.shape[1]
    row = lambda i, d: (i, 0)
    fixed = lambda i, d: (0, 0)
    grid_spec = pltpu.PrefetchScalarGridSpec(
        num_scalar_prefetch=1,
        grid=(t // tm,),
        in_specs=[
            pl.BlockSpec((tm, D_MODEL), row),
            pl.BlockSpec((tm, TOP_K), row),
            pl.BlockSpec(memory_space=pl.ANY),
            pl.BlockSpec((tm, ple), row),
            pl.BlockSpec((1, D_MODEL), fixed),
            pl.BlockSpec((D_MODEL, D_MODEL), fixed),
            pl.BlockSpec((ple, D_MODEL), fixed),
            pl.BlockSpec((1, D_MODEL), fixed),
        ],
        out_specs=pl.BlockSpec((tm, D_MODEL), row),
        scratch_shapes=[
            pltpu.VMEM((2, TOP_K * tm * ROW_TILES, LANES), F32),
            pltpu.SemaphoreType.DMA((2,)),
        ],
    )
    return pl.pallas_call(
        _combine_kernel,
        grid_spec=grid_spec,
        out_shape=jax.ShapeDtypeStruct((t, D_MODEL), F32),
        compiler_params=pltpu.CompilerParams(
            dimension_semantics=("arbitrary",), vmem_limit_bytes=VMEM_LIMIT),
        name="combine_ple_final",
    )(dest_flat, x1, gates_tk, ys, p2, gple, wpg, wpp, gfin)


def kernel(x, p, rel_bias, norm_mix_g, w_in, pool_w, pool_scale, w_o_attn, w_o_pool, w_out,
           norm_ffn_g, router_w, router_b, w_gate_up, b_gate_up, w_down, b_down,
           norm_ple_g, w_ple_gate, w_ple_proj, norm_final_g):
    b, s, d = x.shape
    depth = w_in.shape[0]
    t = b * s
    assert d == D_MODEL and s % MOBA_BLOCK == 0 and t % IN_TM == 0 and s % MIX_TM == 0
    assert depth == 1, "the final norm is fused into the layer's last kernel"
    n_pad = t * TOP_K + N_EXPERTS * EXPERT_TM
    n_blk = n_pad // EXPERT_TM
    bown, bprev, bfar = _t5_bias_tables(rel_bias)

    x2 = x.reshape(t, d)
    for i in range(depth):
        q, k, v, u, gl = _in_proj(x2, norm_mix_g[i].reshape(1, d), w_in[i].astype(BF16))

        def heads(a):
            return a.reshape(b, s, N_HEADS, HEAD_DIM).transpose(0, 2, 1, 3).reshape(
                b * N_HEADS, s, HEAD_DIM)

        vth = v.reshape(b, s, N_HEADS, HEAD_DIM).transpose(0, 2, 3, 1).reshape(
            b * N_HEADS, HEAD_DIM, s)
        attn_t = _attention(heads(q), heads(k), vth, bown, bprev, bfar)
        attn = attn_t.reshape(b, N_HEADS, HEAD_DIM, s).transpose(0, 3, 1, 2).reshape(
            t, ATTN_WIDTH)

        x1, hf_rows, idx_kt, rank_kt, gate_kt, cnt = _mix(
            x2, attn, u, gl, pool_w[i].astype(BF16), pool_scale[i].reshape(1, POOL_WIDTH),
            w_o_attn[i].astype(BF16), w_o_pool[i].astype(BF16), w_out[i].astype(BF16),
            norm_ffn_g[i].reshape(1, d), router_w[i].T, router_b[i].reshape(N_EXPERTS, 1), s)

        counts = cnt[:, 0]
        padded = (counts + EXPERT_TM - 1) // EXPERT_TM * EXPERT_TM
        pend = jnp.cumsum(padded)
        pstart = pend - padded
        e_ids = jnp.arange(N_EXPERTS, dtype=I32)
        pstart_of = jnp.sum(jnp.where(idx_kt[..., None] == e_ids, pstart, 0), axis=-1)
        dest_flat = (pstart_of + rank_kt).T.reshape(-1).astype(I32)
        blk_row0 = jnp.arange(n_blk, dtype=I32) * EXPERT_TM
        blk_expert = jnp.minimum(jnp.sum(pend[None, :] <= blk_row0[:, None], axis=1),
                                 N_EXPERTS - 1).astype(I32)
        n_used = (pend[-1:] // EXPERT_TM).astype(I32)
        row_tok = jnp.zeros((n_pad,), I32).at[dest_flat].set(
            jnp.arange(t * TOP_K, dtype=I32) // TOP_K, unique_indices=True)

        bgu = b_gate_up[i].reshape(N_EXPERTS, D_MODEL // LANES, LANES, 2).transpose(
            0, 1, 3, 2).reshape(N_EXPERTS, 1, 2 * D_MODEL)
        ys = _experts(blk_expert, n_used, row_tok * ROW_TILES, hf_rows, w_gate_up[i], w_down[i],
                      bgu, b_down[i][:, None, :])

        x2 = _combine(dest_flat, x1, gate_kt.T, ys, p[i].reshape(t, -1),
                      norm_ple_g[i].reshape(1, d), w_ple_gate[i].astype(BF16),
                      w_ple_proj[i].astype(BF16), norm_final_g.reshape(1, d))
    return x2.reshape(b, s, d)
```

```python
import functools
import math

import jax
import jax.numpy as jnp
from jax import lax
from jax.experimental import pallas as pl
from jax.experimental.pallas import tpu as pltpu

F32 = jnp.float32
BF16 = jnp.bfloat16
I32 = jnp.int32

D_MODEL = 1024
N_HEADS = 8
HEAD_DIM = 64
ATTN_WIDTH = N_HEADS * HEAD_DIM
MOBA_BLOCK = 256
MOBA_TOPK = 3
NUM_BUCKETS = 32
MAX_DISTANCE = 128
POOL_WINDOWS = (2, 4, 8, 16)
POOL_GROUPS = 4
POOL_GROUP_DIM = 128
POOL_WIDTH = POOL_GROUPS * POOL_GROUP_DIM
N_EXPERTS = 32
TOP_K = 4
SWIGLU_LIMIT = 7.0
SWIGLU_ALPHA = 1.702
RMS_EPS = 1e-6

LANES = 128
SUBLANES = 8
ROW_TILES = D_MODEL // LANES

IN_TM = 512
MIX_TM = 256
EXPERT_TM = 256
OUT_TM = 256
ATTN_GROUP = 4
ATTN_HEADS = 2
GATHER_CHUNK = 32
WEIGHT_DMA_CHUNKS = 4
POOL_HALO = 16
MASKED = -1e30
LOG2E = math.log2(math.e)
Q_SCALE = HEAD_DIM ** -0.5 * LOG2E

VMEM_LIMIT = 56 * 1024 * 1024

_NT = (((1,), (1,)), ((), ()))


def _rms(x, g):
    ms = jnp.mean(x * x, axis=-1, keepdims=True)
    return x * lax.rsqrt(ms + RMS_EPS) * g


def _sigmoid(x):
    return 1.0 / (1.0 + jnp.exp(-x))


def _in_proj_kernel(x_ref, g_ref, w_ref, q_ref, k_ref, v_ref, u_ref, gl_ref):
    h = _rms(x_ref[...], g_ref[...]).astype(BF16)
    aw = ATTN_WIDTH

    def proj(lo, hi):
        return jnp.dot(h, w_ref[:, lo:hi], preferred_element_type=F32)

    q_ref[...] = (proj(0, aw) * Q_SCALE).astype(BF16)
    k_ref[...] = proj(aw, 2 * aw).astype(BF16)
    v_ref[...] = proj(2 * aw, 3 * aw).astype(BF16)
    u_ref[...] = proj(3 * aw, 3 * aw + POOL_WIDTH)
    base = 3 * aw + POOL_WIDTH
    for c in range(2):
        lo = base + c * D_MODEL
        gl_ref[:, c * D_MODEL:(c + 1) * D_MODEL] = proj(lo, lo + D_MODEL).astype(BF16)


def _in_proj(x2, g, w_bf16):
    t = x2.shape[0]
    in_cols = w_bf16.shape[1]
    row = lambda i: (i, 0)
    fixed = lambda i: (0, 0)
    return pl.pallas_call(
        _in_proj_kernel,
        grid=(t // IN_TM,),
        in_specs=[
            pl.BlockSpec((IN_TM, D_MODEL), row),
            pl.BlockSpec((1, D_MODEL), fixed),
            pl.BlockSpec((D_MODEL, in_cols), fixed),
        ],
        out_specs=[
            pl.BlockSpec((IN_TM, ATTN_WIDTH), row),
            pl.BlockSpec((IN_TM, ATTN_WIDTH), row),
            pl.BlockSpec((IN_TM, ATTN_WIDTH), row),
            pl.BlockSpec((IN_TM, POOL_WIDTH), row),
            pl.BlockSpec((IN_TM, 2 * D_MODEL), row),
        ],
        out_shape=[
            jax.ShapeDtypeStruct((t, ATTN_WIDTH), BF16),
            jax.ShapeDtypeStruct((t, ATTN_WIDTH), BF16),
            jax.ShapeDtypeStruct((t, ATTN_WIDTH), BF16),
            jax.ShapeDtypeStruct((t, POOL_WIDTH), F32),
            jax.ShapeDtypeStruct((t, 2 * D_MODEL), BF16),
        ],
        compiler_params=pltpu.CompilerParams(
            dimension_semantics=("parallel",), vmem_limit_bytes=VMEM_LIMIT),
        name="in_proj",
    )(x2, g, w_bf16)


def _attn_kernel(bfar_ref, q_ref, k_ref, vt_ref, bown_ref, bprev_ref, o_ref,
                 kmean_s, am_s, sa_s, sb_s, ga_s, gb_s, m_s, l_s, acc_s, *, n_blocks):
    hp = pl.program_id(0)
    j = pl.program_id(1)
    blk = MOBA_BLOCK
    grp = ATTN_GROUP
    gk = grp * blk
    n_groups = n_blocks // grp
    n_far = jnp.maximum(j - 1, 0)
    ng = (n_far + grp - 1) // grp
    jp = jnp.maximum(j - 1, 0)
    heads = range(ATTN_HEADS)

    @pl.when(j == 0)
    def _():
        for hh in heads:
            for n in range(n_blocks):
                kb = k_ref[hh, n * blk:(n + 1) * blk, :].astype(F32)
                kmean_s[hh, n:n + 1, :] = jnp.mean(kb, axis=0, keepdims=True)

    qs = [q_ref[hh] for hh in heads]

    def select(hh):
        gate = lax.dot_general(kmean_s[hh], qs[hh].astype(F32), _NT,
                               precision=lax.Precision.HIGHEST, preferred_element_type=F32)
        n_iota = lax.broadcasted_iota(I32, gate.shape, 0)
        past = n_iota < j
        gate = jnp.where(past, gate, jnp.finfo(F32).min)
        sel = jnp.zeros(gate.shape, jnp.bool_)
        for _ in range(MOBA_TOPK):
            mx = jnp.max(gate, axis=0, keepdims=True)
            idx = jnp.min(jnp.where(gate == mx, n_iota, n_blocks), axis=0, keepdims=True)
            pick = n_iota == idx
            sel = jnp.logical_or(sel, pick)
            gate = jnp.where(pick, -jnp.inf, gate)
        sel = jnp.logical_and(sel, past)
        bfar = bfar_ref[(hp * ATTN_HEADS + hh) % N_HEADS]
        am_s[hh] = jnp.where(jnp.logical_and(sel, n_iota < j - 1), bfar, MASKED)
        return jnp.max(jnp.where(jnp.logical_and(sel, n_iota == j - 1), 0.0, MASKED),
                       axis=0, keepdims=True)

    def produce(hh, g, s_ref, gmax_ref):
        g = jnp.minimum(g, n_groups - 1)
        kb = k_ref[hh, pl.ds(pl.multiple_of(g * gk, gk), gk), :]
        s = lax.dot_general(kb, qs[hh], _NT, preferred_element_type=F32)
        s_ref[hh] = s
        gmax = None
        for b in range(grp):
            mb = (jnp.max(s[b * blk:(b + 1) * blk], axis=0, keepdims=True)
                  + am_s[hh, pl.ds(g * grp + b, 1), :])
            gmax = mb if gmax is None else jnp.maximum(gmax, mb)
        gmax_ref[hh] = gmax

    def consume(hh, g, s_ref, gmax_ref):
        m_old = m_s[hh]
        m_new = jnp.maximum(m_old, gmax_ref[hh])
        alpha = jnp.exp2(m_old - m_new)
        parts, lsum = [], None
        for b in range(grp):
            shift = m_new - am_s[hh, pl.ds(g * grp + b, 1), :]
            pb = jnp.exp2(s_ref[hh, b * blk:(b + 1) * blk, :] - shift)
            sb = jnp.sum(pb, axis=0, keepdims=True)
            lsum = sb if lsum is None else lsum + sb
            parts.append(pb.astype(BF16))
        p = jnp.concatenate(parts, axis=0)
        vt = vt_ref[hh, :, pl.ds(pl.multiple_of(g * gk, gk), gk)]
        pv = jnp.dot(vt, p, preferred_element_type=F32)
        m_s[hh] = m_new
        l_s[hh] = alpha * l_s[hh] + lsum
        acc_s[hh] = alpha * acc_s[hh] + pv

    def near_scores(hh, n, bias):
        kb = k_ref[hh, pl.ds(pl.multiple_of(n * blk, blk), blk), :]
        return lax.dot_general(kb, qs[hh], _NT, preferred_element_type=F32) + bias

    def near_pv(hh, n, p):
        vt = vt_ref[hh, :, pl.ds(pl.multiple_of(n * blk, blk), blk)]
        return jnp.dot(vt, p.astype(BF16), preferred_element_type=F32)

    for hh in heads:
        am_prev = select(hh)
        produce(hh, 0, sa_s, ga_s)
        s_own = near_scores(hh, j, bown_ref[hh])
        s_prev = near_scores(hh, jp, bprev_ref[hh])
        m = jnp.maximum(jnp.max(s_own, axis=0, keepdims=True),
                        jnp.max(s_prev, axis=0, keepdims=True) + am_prev)
        p_own = jnp.exp2(s_own - m)
        p_prev = jnp.exp2(s_prev - (m - am_prev))
        m_s[hh] = m
        l_s[hh] = jnp.sum(p_own, axis=0, keepdims=True) + jnp.sum(p_prev, axis=0, keepdims=True)
        acc_s[hh] = near_pv(hh, j, p_own) + near_pv(hh, jp, p_prev)

    def pair(h, carry):
        for hh in heads:
            consume(hh, 2 * h, sa_s, ga_s)
            produce(hh, 2 * h + 1, sb_s, gb_s)
        for hh in heads:
            consume(hh, 2 * h + 1, sb_s, gb_s)
            produce(hh, 2 * h + 2, sa_s, ga_s)
        return carry

    lax.fori_loop(0, ng // 2, pair, 0)

    @pl.when(ng % 2 == 1)
    def _():
        for hh in heads:
            consume(hh, ng - 1, sa_s, ga_s)

    for hh in heads:
        o_ref[hh] = (acc_s[hh] / l_s[hh]).astype(o_ref.dtype)


def _attention(qh, kh, vth, bown, bprev, bfar):
    bhn, s, dh = qh.shape
    nb = s // MOBA_BLOCK
    ah = ATTN_HEADS
    assert N_HEADS % ah == 0 and nb % ATTN_GROUP == 0
    kern = functools.partial(_attn_kernel, n_blocks=nb)
    head_blk = lambda b, j: (b % (N_HEADS // ah), 0, 0)
    grid_spec = pltpu.PrefetchScalarGridSpec(
        num_scalar_prefetch=0,
        grid=(bhn // ah, nb),
        in_specs=[
            pl.BlockSpec(memory_space=pltpu.SMEM),
            pl.BlockSpec((ah, MOBA_BLOCK, dh), lambda b, j: (b, j, 0)),
            pl.BlockSpec((ah, s, dh), lambda b, j: (b, 0, 0)),
            pl.BlockSpec((ah, dh, s), lambda b, j: (b, 0, 0)),
            pl.BlockSpec((ah, MOBA_BLOCK, MOBA_BLOCK), head_blk),
            pl.BlockSpec((ah, MOBA_BLOCK, MOBA_BLOCK), head_blk),
        ],
        out_specs=pl.BlockSpec((ah, dh, MOBA_BLOCK), lambda b, j: (b, 0, j)),
        scratch_shapes=[
            pltpu.VMEM((ah, nb, dh), F32),
            pltpu.VMEM((ah, nb, MOBA_BLOCK), F32),
            pltpu.VMEM((ah, ATTN_GROUP * MOBA_BLOCK, MOBA_BLOCK), F32),
            pltpu.VMEM((ah, ATTN_GROUP * MOBA_BLOCK, MOBA_BLOCK), F32),
            pltpu.VMEM((ah, 1, MOBA_BLOCK), F32),
            pltpu.VMEM((ah, 1, MOBA_BLOCK), F32),
            pltpu.VMEM((ah, 1, MOBA_BLOCK), F32),
            pltpu.VMEM((ah, 1, MOBA_BLOCK), F32),
            pltpu.VMEM((ah, dh, MOBA_BLOCK), F32),
        ],
    )
    return pl.pallas_call(
        kern,
        grid_spec=grid_spec,
        out_shape=jax.ShapeDtypeStruct((bhn, dh, s), BF16),
        compiler_params=pltpu.CompilerParams(
            dimension_semantics=("parallel", "arbitrary"), vmem_limit_bytes=VMEM_LIMIT),
        name="moba_attention",
    )(bfar, qh, kh, vth, bown, bprev)


def _t5_bias_tables(rel_bias):
    blk = MOBA_BLOCK
    n = jnp.arange(2 * blk)
    max_exact = NUM_BUCKETS // 2
    nf = jnp.maximum(n, 1).astype(F32)
    large = max_exact + (jnp.log(nf / max_exact) / math.log(MAX_DISTANCE / max_exact)
                         * (NUM_BUCKETS - max_exact)).astype(I32)
    large = jnp.minimum(large, NUM_BUCKETS - 1)
    bucket = jnp.where(n < max_exact, n, large)
    rel_bias = rel_bias.astype(F32) * LOG2E
    tbl = rel_bias[bucket].T
    key = jnp.arange(blk)[:, None]
    qry = jnp.arange(blk)[None, :]
    d = qry - key
    h = tbl.shape[0]
    wide = 2 * blk + 1
    skew = jnp.broadcast_to(jnp.pad(tbl, ((0, 0), (0, 1)))[:, None, :], (h, blk, wide))
    skew = skew.reshape(h, blk * wide)[:, :blk * 2 * blk].reshape(h, blk, 2 * blk)
    bown = jnp.where(d >= 0, skew[:, :, :blk], MASKED)
    bprev = skew[:, :, blk:]
    bfar = rel_bias[NUM_BUCKETS - 1]
    return bown, bprev, bfar


def _mix_kernel(x_ref, attn_ref, u_ref, halo_ref, gl_ref, pw_ref, ps_ref, woa_ref, wop_ref,
                wout_ref, gffn_ref, rwt_ref, rb_ref,
                x1_ref, hf_ref, idx_ref, rank_ref, gate_ref, cnt_ref,
                ext_s, carry_s, *, tiles_per_seq):
    i = pl.program_id(0)
    tm = MIX_TM

    @pl.when(i == 0)
    def _():
        carry_s[...] = jnp.zeros_like(carry_s)

    first = (i % tiles_per_seq) == 0
    ext_s[0:POOL_HALO, :] = jnp.where(first, 0.0, halo_ref[...])
    ext_s[POOL_HALO:POOL_HALO + tm, :] = u_ref[...]
    pos = (i % tiles_per_seq) * tm + lax.broadcasted_iota(I32, (tm, 1), 0)
    pooled_parts = []
    for g, w in enumerate(POOL_WINDOWS):
        c0, c1 = g * POOL_GROUP_DIM, (g + 1) * POOL_GROUP_DIM
        win = ext_s[POOL_HALO:POOL_HALO + tm, c0:c1]
        for sft in range(1, w):
            win = win + ext_s[POOL_HALO - sft:POOL_HALO - sft + tm, c0:c1]
        cnt = jnp.minimum(pos + 1, w).astype(F32)
        pin = win / cnt - ext_s[POOL_HALO:POOL_HALO + tm, c0:c1]
        pooled_parts.append(jnp.dot(pin.astype(BF16), pw_ref[g], preferred_element_type=F32))
    pooled = jnp.concatenate(pooled_parts, axis=1) * ps_ref[...]

    a = jnp.dot(attn_ref[...], woa_ref[...], preferred_element_type=F32)
    pm = jnp.dot(pooled.astype(BF16), wop_ref[...], preferred_element_type=F32)
    g0 = _sigmoid(gl_ref[:, 0:D_MODEL].astype(F32))
    g1 = _sigmoid(gl_ref[:, D_MODEL:2 * D_MODEL].astype(F32))
    merged = g0 * a + g1 * pm
    x1 = x_ref[...] + jnp.dot(merged.astype(BF16), wout_ref[...], preferred_element_type=F32)
    x1_ref[...] = x1

    hf = _rms(x1, gffn_ref[...])
    hfb = hf.astype(BF16)
    for s in range(ROW_TILES):
        hf_ref[pl.ds(s, tm, stride=ROW_TILES), :] = hfb[:, s * LANES:(s + 1) * LANES].astype(F32)

    logits = lax.dot_general(rwt_ref[...], hfb.astype(F32), _NT,
                             precision=lax.Precision.HIGHEST,
                             preferred_element_type=F32) + rb_ref[...]
    e_iota = lax.broadcasted_iota(I32, logits.shape, 0)
    vals, picks = [], []
    for _ in range(TOP_K):
        mx = jnp.max(logits, axis=0, keepdims=True)
        idx = jnp.min(jnp.where(logits == mx, e_iota, N_EXPERTS), axis=0, keepdims=True)
        pick = e_iota == idx
        vals.append(mx)
        picks.append(pick)
        idx_ref[len(picks) - 1:len(picks), :] = idx
        logits = jnp.where(pick, -jnp.inf, logits)
    ex = [jnp.exp(v - vals[0]) for v in vals]
    den = ex[0] + ex[1] + ex[2] + ex[3]
    for k in range(TOP_K):
        gate_ref[k:k + 1, :] = ex[k] / den

    onehot = jnp.zeros(e_iota.shape, F32)
    for pick in picks:
        onehot = onehot + pick.astype(F32)
    ra = lax.broadcasted_iota(I32, (tm, tm), 0)
    rb = lax.broadcasted_iota(I32, (tm, tm), 1)
    upper = (ra < rb).astype(BF16)
    before = jnp.dot(onehot.astype(BF16), upper, preferred_element_type=F32) + carry_s[...]
    for k, pick in enumerate(picks):
        rank_ref[k:k + 1, :] = jnp.sum(jnp.where(pick, before, 0.0), axis=0,
                                       keepdims=True).astype(I32)
    carry_s[...] = carry_s[...] + jnp.sum(onehot, axis=1, keepdims=True)
    cnt_ref[...] = jnp.broadcast_to(carry_s[...], cnt_ref.shape).astype(I32)


def _mix(x2, attn, u, gl, pw, ps, woa, wop, wout, gffn, rwt, rb, seq):
    t = x2.shape[0]
    tm = MIX_TM
    tiles_per_seq = seq // tm
    halo_per_tile = tm // POOL_HALO
    row = lambda i: (i, 0)
    fixed2 = lambda i: (0, 0)
    fixed3 = lambda i: (0, 0, 0)
    col = lambda i: (0, i)
    kern = functools.partial(_mix_kernel, tiles_per_seq=tiles_per_seq)
    return pl.pallas_call(
        kern,
        grid=(t // tm,),
        in_specs=[
            pl.BlockSpec((tm, D_MODEL), row),
            pl.BlockSpec((tm, ATTN_WIDTH), row),
            pl.BlockSpec((tm, POOL_WIDTH), row),
            pl.BlockSpec((POOL_HALO, POOL_WIDTH),
                         lambda i: (jnp.maximum(i * halo_per_tile - 1, 0), 0)),
            pl.BlockSpec((tm, 2 * D_MODEL), row),
            pl.BlockSpec((POOL_GROUPS, POOL_GROUP_DIM, POOL_GROUP_DIM), fixed3),
            pl.BlockSpec((1, POOL_WIDTH), fixed2),
            pl.BlockSpec((ATTN_WIDTH, D_MODEL), fixed2),
            pl.BlockSpec((POOL_WIDTH, D_MODEL), fixed2),
            pl.BlockSpec((D_MODEL, D_MODEL), fixed2),
            pl.BlockSpec((1, D_MODEL), fixed2),
            pl.BlockSpec((N_EXPERTS, D_MODEL), fixed2),
            pl.BlockSpec((N_EXPERTS, 1), fixed2),
        ],
        out_specs=[
            pl.BlockSpec((tm, D_MODEL), row),
            pl.BlockSpec((tm * ROW_TILES, LANES), row),
            pl.BlockSpec((TOP_K, tm), col),
            pl.BlockSpec((TOP_K, tm), col),
            pl.BlockSpec((TOP_K, tm), col),
            pl.BlockSpec((N_EXPERTS, LANES), fixed2),
        ],
        out_shape=[
            jax.ShapeDtypeStruct((t, D_MODEL), F32),
            jax.ShapeDtypeStruct((t * ROW_TILES, LANES), F32),
            jax.ShapeDtypeStruct((TOP_K, t), I32),
            jax.ShapeDtypeStruct((TOP_K, t), I32),
            jax.ShapeDtypeStruct((TOP_K, t), F32),
            jax.ShapeDtypeStruct((N_EXPERTS, LANES), I32),
        ],
        scratch_shapes=[
            pltpu.VMEM((POOL_HALO + tm, POOL_WIDTH), F32),
            pltpu.VMEM((N_EXPERTS, 1), F32),
        ],
        compiler_params=pltpu.CompilerParams(
            dimension_semantics=("arbitrary",), vmem_limit_bytes=VMEM_LIMIT),
        name="mix_router",
    )(x2, attn, u, u, gl, pw, ps, woa, wop, wout, gffn, rwt, rb)


def _row(ref, r):
    return ref.at[pl.ds(pl.multiple_of(r * ROW_TILES, ROW_TILES), ROW_TILES)]


def _pair_split_matrix():
    n = 2 * LANES
    r = lax.broadcasted_iota(I32, (n, n), 0)
    c = lax.broadcasted_iota(I32, (n, n), 1)
    src = jnp.where(c < LANES, 2 * c, 2 * (c - LANES) + 1)
    return (r == src).astype(BF16)


def _expert_kernel(be_ref, nused_ref, rowsrc_ref, first_ref, par_ref, next_ref,
                   hf_ref, wgu_hbm, wd_hbm, bgu_ref, bd_ref,
                   y_ref, xbuf, wgu_f, wd_f, wgu_s, wd_s, sems, wsems):
    i = pl.program_id(0)
    tm = EXPERT_TM
    n_used = nused_ref[0]
    slot = i % 2

    def start_rows(blk, to_slot, r):
        src = pl.multiple_of(rowsrc_ref[blk * tm + r], ROW_TILES)
        pltpu.make_async_copy(hf_ref.at[pl.ds(src, ROW_TILES)], _row(xbuf.at[to_slot], r),
                              sems.at[to_slot]).start()

    def wait_rows(at_slot):
        pltpu.make_async_copy(hf_ref.at[pl.ds(0, tm * ROW_TILES)], xbuf.at[at_slot],
                              sems.at[at_slot]).wait()

    def weight_copies(e, w_slot):
        rows = D_MODEL // WEIGHT_DMA_CHUNKS
        out = []
        for c in range(WEIGHT_DMA_CHUNKS):
            sl = pl.ds(c * rows, rows)
            out.append(pltpu.make_async_copy(wgu_hbm.at[e, sl], wgu_f.at[w_slot, sl],
                                             wsems.at[w_slot]))
            out.append(pltpu.make_async_copy(wd_hbm.at[e, sl], wd_f.at[w_slot, sl],
                                             wsems.at[w_slot]))
        return out

    @pl.when(i == 0)
    def _():
        for cp in weight_copies(be_ref[0], 0):
            cp.start()

        def row(r, carry):
            start_rows(0, 0, r)
            return carry
        lax.fori_loop(0, tm, row, 0)

    @pl.when(jnp.logical_and(first_ref[i] == 1, i < n_used))
    def _():
        par = par_ref[i]
        for cp in weight_copies(be_ref[i], par):
            cp.wait()

        @pl.when(next_ref[i] >= 0)
        def _():
            for cp in weight_copies(next_ref[i], 1 - par):
                cp.start()

        split = _pair_split_matrix()
        for c in range(2 * D_MODEL // (2 * LANES)):
            lo, hi = c * 2 * LANES, (c + 1) * 2 * LANES
            wgu_s[:, lo:hi] = jnp.dot(wgu_f[par, :, lo:hi].astype(BF16), split,
                                      preferred_element_type=F32).astype(BF16)
        wd_s[...] = wd_f[par].astype(BF16)

    @pl.when(i >= n_used)
    def _():
        y_ref[...] = jnp.zeros_like(y_ref)

    @pl.when(i == n_used)
    def _():
        wait_rows(slot)

    @pl.when(i < n_used)
    def _():
        wait_rows(slot)
        xb = jnp.concatenate(
            [xbuf[slot, pl.ds(s, tm, stride=ROW_TILES), :] for s in range(ROW_TILES)],
            axis=1).astype(BF16)
        hid_parts = []
        n_chunks = D_MODEL // LANES
        assert n_chunks * GATHER_CHUNK == tm
        for c in range(n_chunks):
            lo, hi = c * 2 * LANES, (c + 1) * 2 * LANES
            gu = jnp.dot(xb, wgu_s[:, lo:hi], preferred_element_type=F32) + bgu_ref[0, :, lo:hi]
            glu = jnp.minimum(gu[:, :LANES], SWIGLU_LIMIT)
            lin = jnp.clip(gu[:, LANES:], -SWIGLU_LIMIT, SWIGLU_LIMIT)
            hid_parts.append(((lin + 1.0) * (glu * _sigmoid(SWIGLU_ALPHA * glu))).astype(BF16))
            for r in range(c * GATHER_CHUNK, (c + 1) * GATHER_CHUNK):
                start_rows(i + 1, 1 - slot, r)
        hid = jnp.concatenate(hid_parts, axis=1)
        y = jnp.dot(hid, wd_s[...], preferred_element_type=F32) + bd_ref[0]
        for s in range(ROW_TILES):
            y_ref[pl.ds(s, tm, stride=ROW_TILES), :] = y[:, s * LANES:(s + 1) * LANES]


def _experts(blk_expert, n_used, row_src, first, parity, next_expert, hf_rows, wgu, wd,
             bgu_split, bd):
    n_blk = blk_expert.shape[0]
    tm = EXPERT_TM

    def ymap(i, be, nu, *_):
        return (i, 0)

    def bmap(i, be, nu, *_):
        return (be[jnp.minimum(i, nu[0] - 1)], 0, 0)

    grid_spec = pltpu.PrefetchScalarGridSpec(
        num_scalar_prefetch=6,
        grid=(n_blk,),
        in_specs=[
            pl.BlockSpec(memory_space=pl.ANY),
            pl.BlockSpec(memory_space=pl.ANY),
            pl.BlockSpec(memory_space=pl.ANY),
            pl.BlockSpec((1, 1, 2 * D_MODEL), bmap),
            pl.BlockSpec((1, 1, D_MODEL), bmap),
        ],
        out_specs=pl.BlockSpec((tm * ROW_TILES, LANES), ymap),
        scratch_shapes=[
            pltpu.VMEM((2, tm * ROW_TILES, LANES), F32),
            pltpu.VMEM((2, D_MODEL, 2 * D_MODEL), F32),
            pltpu.VMEM((2, D_MODEL, D_MODEL), F32),
            pltpu.VMEM((D_MODEL, 2 * D_MODEL), BF16),
            pltpu.VMEM((D_MODEL, D_MODEL), BF16),
            pltpu.SemaphoreType.DMA((2,)),
            pltpu.SemaphoreType.DMA((2,)),
        ],
    )
    return pl.pallas_call(
        _expert_kernel,
        grid_spec=grid_spec,
        out_shape=jax.ShapeDtypeStruct((n_blk * tm * ROW_TILES, LANES), F32),
        compiler_params=pltpu.CompilerParams(
            dimension_semantics=("arbitrary",), vmem_limit_bytes=VMEM_LIMIT),
        name="expert_ffn",
    )(blk_expert, n_used, row_src, first, parity, next_expert, hf_rows, wgu, wd, bgu_split, bd)


def _combine_kernel(dest_ref, x1_ref, gate_ref, ys_ref, p_ref, gple_ref, wpg_ref, wpp_ref,
                    gfin_ref, o_ref, ybuf, sems):
    i = pl.program_id(0)
    n_steps = pl.num_programs(0)
    tm = OUT_TM
    slot = i % 2

    def gather(tile, to_slot):
        def tok(r, carry):
            t = tile * tm + r
            for k in range(TOP_K):
                pltpu.async_copy(_row(ys_ref, dest_ref[t * TOP_K + k]),
                                 _row(ybuf.at[to_slot], k * tm + r), sems.at[to_slot],
                                 priority=k % 2)
            return carry
        lax.fori_loop(0, tm, tok, 0)

    @pl.when(i == 0)
    def _():
        gather(0, 0)

    @pl.when(i + 1 < n_steps)
    def _():
        gather(i + 1, 1 - slot)

    pltpu.make_async_copy(ys_ref.at[pl.ds(0, TOP_K * tm * ROW_TILES)], ybuf.at[slot],
                          sems.at[slot]).wait()

    moe = None
    for k in range(TOP_K):
        yk = jnp.concatenate(
            [ybuf[slot, pl.ds(k * tm * ROW_TILES + s, tm, stride=ROW_TILES), :]
             for s in range(ROW_TILES)], axis=1)
        term = gate_ref[:, k:k + 1] * yk
        moe = term if moe is None else moe + term
    x2 = x1_ref[...] + moe
    hp = _rms(x2, gple_ref[...]).astype(BF16)
    pg = _sigmoid(jnp.dot(hp, wpg_ref[...], preferred_element_type=F32))
    proj = jnp.dot(p_ref[...].astype(BF16), wpp_ref[...], preferred_element_type=F32)
    x3 = x2 + pg * proj
    o_ref[...] = _rms(x3, gfin_ref[...])


def _combine(dest_flat, x1, gates_tk, ys, p2, gple, wpg, wpp, gfin):
    t = x1.shape[0]
    tm = OUT_TM
    ple = p2.shape[1]
    row = lambda i, d: (i, 0)
    fixed = lambda i, d: (0, 0)
    grid_spec = pltpu.PrefetchScalarGridSpec(
        num_scalar_prefetch=1,
        grid=(t // tm,),
        in_specs=[
            pl.BlockSpec((tm, D_MODEL), row),
            pl.BlockSpec((tm, TOP_K), row),
            pl.BlockSpec(memory_space=pl.ANY),
            pl.BlockSpec((tm, ple), row),
            pl.BlockSpec((1, D_MODEL), fixed),
            pl.BlockSpec((D_MODEL, D_MODEL), fixed),
            pl.BlockSpec((ple, D_MODEL), fixed),
            pl.BlockSpec((1, D_MODEL), fixed),
        ],
        out_specs=pl.BlockSpec((tm, D_MODEL), row),
        scratch_shapes=[
            pltpu.VMEM((2, TOP_K * tm * ROW_TILES, LANES), F32),
            pltpu.SemaphoreType.DMA((2,)),
        ],
    )
    return pl.pallas_call(
        _combine_kernel,
        grid_spec=grid_spec,
        out_shape=jax.ShapeDtypeStruct((t, D_MODEL), F32),
        compiler_params=pltpu.CompilerParams(
            dimension_semantics=("arbitrary",), vmem_limit_bytes=VMEM_LIMIT),
        name="combine_ple_final",
    )(dest_flat, x1, gates_tk, ys, p2, gple, wpg, wpp, gfin)


def kernel(x, p, rel_bias, norm_mix_g, w_in, pool_w, pool_scale, w_o_attn, w_o_pool, w_out,
           norm_ffn_g, router_w, router_b, w_gate_up, b_gate_up, w_down, b_down,
           norm_ple_g, w_ple_gate, w_ple_proj, norm_final_g):
    b, s, d = x.shape
    depth = w_in.shape[0]
    t = b * s
    assert d == D_MODEL and s % MOBA_BLOCK == 0 and t % IN_TM == 0 and s % MIX_TM == 0
    assert depth == 1, "the final norm is fused into the layer's last kernel"
    n_pad = t * TOP_K + N_EXPERTS * EXPERT_TM
    n_blk = n_pad // EXPERT_TM
    bown, bprev, bfar = _t5_bias_tables(rel_bias)

    x2 = x.reshape(t, d)
    for i in range(depth):
        q, k, v, u, gl = _in_proj(x2, norm_mix_g[i].reshape(1, d), w_in[i].astype(BF16))

        def heads(a):
            return a.reshape(b, s, N_HEADS, HEAD_DIM).transpose(0, 2, 1, 3).reshape(
                b * N_HEADS, s, HEAD_DIM)

        vth = v.reshape(b, s, N_HEADS, HEAD_DIM).transpose(0, 2, 3, 1).reshape(
            b * N_HEADS, HEAD_DIM, s)
        attn_t = _attention(heads(q), heads(k), vth, bown, bprev, bfar)
        attn = attn_t.reshape(b, N_HEADS, HEAD_DIM, s).transpose(0, 3, 1, 2).reshape(
            t, ATTN_WIDTH)

        x1, hf_rows, idx_kt, rank_kt, gate_kt, cnt = _mix(
            x2, attn, u, gl, pool_w[i].astype(BF16), pool_scale[i].reshape(1, POOL_WIDTH),
            w_o_attn[i].astype(BF16), w_o_pool[i].astype(BF16), w_out[i].astype(BF16),
            norm_ffn_g[i].reshape(1, d), router_w[i].T, router_b[i].reshape(N_EXPERTS, 1), s)

        counts = cnt[:, 0]
        padded = (counts + EXPERT_TM - 1) // EXPERT_TM * EXPERT_TM
        pend = jnp.cumsum(padded)
        pstart = pend - padded
        e_ids = jnp.arange(N_EXPERTS, dtype=I32)
        pstart_of = jnp.sum(jnp.where(idx_kt[..., None] == e_ids, pstart, 0), axis=-1)
        dest_flat = (pstart_of + rank_kt).T.reshape(-1).astype(I32)
        blk_row0 = jnp.arange(n_blk, dtype=I32) * EXPERT_TM
        blk_expert = jnp.minimum(jnp.sum(pend[None, :] <= blk_row0[:, None], axis=1),
                                 N_EXPERTS - 1).astype(I32)
        n_used = (pend[-1:] // EXPERT_TM).astype(I32)
        row_tok = jnp.zeros((n_pad,), I32).at[dest_flat].set(
            jnp.arange(t * TOP_K, dtype=I32) // TOP_K, unique_indices=True)

        bgu = b_gate_up[i].reshape(N_EXPERTS, D_MODEL // LANES, LANES, 2).transpose(
            0, 1, 3, 2).reshape(N_EXPERTS, 1, 2 * D_MODEL)
        first = jnp.concatenate([jnp.ones((1,), I32),
                                 (blk_expert[1:] != blk_expert[:-1]).astype(I32)])
        parity = ((jnp.cumsum(first) - 1) % 2).astype(I32)
        later = jnp.logical_and(e_ids[None, :] > e_ids[:, None], (padded > 0)[None, :])
        next_e = jnp.min(jnp.where(later, e_ids[None, :], N_EXPERTS), axis=1)
        next_e = jnp.where(next_e == N_EXPERTS, -1, next_e).astype(I32)
        next_expert = jnp.sum(jnp.where(blk_expert[:, None] == e_ids, next_e, 0),
                              axis=1).astype(I32)
        ys = _experts(blk_expert, n_used, row_tok * ROW_TILES, first, parity, next_expert,
                      hf_rows, w_gate_up[i], w_down[i], bgu, b_down[i][:, None, :])

        x2 = _combine(dest_flat, x1, gate_kt.T, ys, p[i].reshape(t, -1),
                      norm_ple_g[i].reshape(1, d), w_ple_gate[i].astype(BF16),
                      w_ple_proj[i].astype(BF16), norm_final_g.reshape(1, d))
    return x2.reshape(b, s, d)
```

```python
import functools
import math

import jax
import jax.numpy as jnp
from jax import lax
from jax.experimental import pallas as pl
from jax.experimental.pallas import tpu as pltpu

F32 = jnp.float32
BF16 = jnp.bfloat16
I32 = jnp.int32

D_MODEL = 1024
N_HEADS = 8
HEAD_DIM = 64
ATTN_WIDTH = N_HEADS * HEAD_DIM
MOBA_BLOCK = 256
MOBA_TOPK = 3
NUM_BUCKETS = 32
MAX_DISTANCE = 128
POOL_WINDOWS = (2, 4, 8, 16)
POOL_GROUPS = 4
POOL_GROUP_DIM = 128
POOL_WIDTH = POOL_GROUPS * POOL_GROUP_DIM
N_EXPERTS = 32
TOP_K = 4
SWIGLU_LIMIT = 7.0
SWIGLU_ALPHA = 1.702
RMS_EPS = 1e-6

LANES = 128
SUBLANES = 8
ROW_TILES = D_MODEL // LANES

IN_TM = 512
MIX_TM = 256
EXPERT_TM = 256
OUT_TM = 256
ATTN_GROUP = 4
ATTN_HEADS = 2
GATHER_CHUNK = 32
WEIGHT_DMA_CHUNKS = 4
GATHER_DEPTH = 4
POOL_HALO = 16
MASKED = -1e30
LOG2E = math.log2(math.e)
Q_SCALE = HEAD_DIM ** -0.5 * LOG2E

VMEM_LIMIT = 56 * 1024 * 1024

_NT = (((1,), (1,)), ((), ()))


def _rms(x, g):
    ms = jnp.mean(x * x, axis=-1, keepdims=True)
    return x * lax.rsqrt(ms + RMS_EPS) * g


def _sigmoid(x):
    return 1.0 / (1.0 + jnp.exp(-x))


def _in_proj_kernel(x_ref, g_ref, w_ref, q_ref, k_ref, v_ref, u_ref, gl_ref):
    h = _rms(x_ref[...], g_ref[...]).astype(BF16)
    aw = ATTN_WIDTH

    def proj(lo, hi):
        return jnp.dot(h, w_ref[:, lo:hi], preferred_element_type=F32)

    q_ref[...] = (proj(0, aw) * Q_SCALE).astype(BF16)
    k_ref[...] = proj(aw, 2 * aw).astype(BF16)
    v_ref[...] = proj(2 * aw, 3 * aw).astype(BF16)
    u_ref[...] = proj(3 * aw, 3 * aw + POOL_WIDTH)
    base = 3 * aw + POOL_WIDTH
    for c in range(2):
        lo = base + c * D_MODEL
        gl_ref[:, c * D_MODEL:(c + 1) * D_MODEL] = proj(lo, lo + D_MODEL).astype(BF16)


def _in_proj(x2, g, w_bf16):
    t = x2.shape[0]
    in_cols = w_bf16.shape[1]
    row = lambda i: (i, 0)
    fixed = lambda i: (0, 0)
    return pl.pallas_call(
        _in_proj_kernel,
        grid=(t // IN_TM,),
        in_specs=[
            pl.BlockSpec((IN_TM, D_MODEL), row),
            pl.BlockSpec((1, D_MODEL), fixed),
            pl.BlockSpec((D_MODEL, in_cols), fixed),
        ],
        out_specs=[
            pl.BlockSpec((IN_TM, ATTN_WIDTH), row),
            pl.BlockSpec((IN_TM, ATTN_WIDTH), row),
            pl.BlockSpec((IN_TM, ATTN_WIDTH), row),
            pl.BlockSpec((IN_TM, POOL_WIDTH), row),
            pl.BlockSpec((IN_TM, 2 * D_MODEL), row),
        ],
        out_shape=[
            jax.ShapeDtypeStruct((t, ATTN_WIDTH), BF16),
            jax.ShapeDtypeStruct((t, ATTN_WIDTH), BF16),
            jax.ShapeDtypeStruct((t, ATTN_WIDTH), BF16),
            jax.ShapeDtypeStruct((t, POOL_WIDTH), F32),
            jax.ShapeDtypeStruct((t, 2 * D_MODEL), BF16),
        ],
        compiler_params=pltpu.CompilerParams(
            dimension_semantics=("parallel",), vmem_limit_bytes=VMEM_LIMIT),
        name="in_proj",
    )(x2, g, w_bf16)


def _attn_kernel(bfar_ref, q_ref, k_ref, vt_ref, bown_ref, bprev_ref, o_ref,
                 kmean_s, am_s, sa_s, sb_s, ga_s, gb_s, m_s, l_s, acc_s, *, n_blocks):
    hp = pl.program_id(0)
    j = pl.program_id(1)
    blk = MOBA_BLOCK
    grp = ATTN_GROUP
    gk = grp * blk
    n_groups = n_blocks // grp
    n_far = jnp.maximum(j - 1, 0)
    ng = (n_far + grp - 1) // grp
    jp = jnp.maximum(j - 1, 0)
    heads = range(ATTN_HEADS)

    @pl.when(j == 0)
    def _():
        for hh in heads:
            for n in range(n_blocks):
                kb = k_ref[hh, n * blk:(n + 1) * blk, :].astype(F32)
                kmean_s[hh, n:n + 1, :] = jnp.mean(kb, axis=0, keepdims=True)

    qs = [q_ref[hh] for hh in heads]

    def select(hh):
        gate = lax.dot_general(kmean_s[hh], qs[hh].astype(F32), _NT,
                               precision=lax.Precision.HIGHEST, preferred_element_type=F32)
        n_iota = lax.broadcasted_iota(I32, gate.shape, 0)
        past = n_iota < j
        gate = jnp.where(past, gate, jnp.finfo(F32).min)
        sel = jnp.zeros(gate.shape, jnp.bool_)
        for _ in range(MOBA_TOPK):
            mx = jnp.max(gate, axis=0, keepdims=True)
            idx = jnp.min(jnp.where(gate == mx, n_iota, n_blocks), axis=0, keepdims=True)
            pick = n_iota == idx
            sel = jnp.logical_or(sel, pick)
            gate = jnp.where(pick, -jnp.inf, gate)
        sel = jnp.logical_and(sel, past)
        bfar = bfar_ref[(hp * ATTN_HEADS + hh) % N_HEADS]
        am_s[hh] = jnp.where(jnp.logical_and(sel, n_iota < j - 1), bfar, MASKED)
        return jnp.max(jnp.where(jnp.logical_and(sel, n_iota == j - 1), 0.0, MASKED),
                       axis=0, keepdims=True)

    def produce(hh, g, s_ref, gmax_ref):
        g = jnp.minimum(g, n_groups - 1)
        kb = k_ref[hh, pl.ds(pl.multiple_of(g * gk, gk), gk), :]
        s = lax.dot_general(kb, qs[hh], _NT, preferred_element_type=F32)
        s_ref[hh] = s
        gmax = None
        for b in range(grp):
            mb = (jnp.max(s[b * blk:(b + 1) * blk], axis=0, keepdims=True)
                  + am_s[hh, pl.ds(g * grp + b, 1), :])
            gmax = mb if gmax is None else jnp.maximum(gmax, mb)
        gmax_ref[hh] = gmax

    def consume(hh, g, s_ref, gmax_ref):
        m_old = m_s[hh]
        m_new = jnp.maximum(m_old, gmax_ref[hh])
        alpha = jnp.exp2(m_old - m_new)
        parts, lsum = [], None
        for b in range(grp):
            shift = m_new - am_s[hh, pl.ds(g * grp + b, 1), :]
            pb = jnp.exp2(s_ref[hh, b * blk:(b + 1) * blk, :] - shift)
            sb = jnp.sum(pb, axis=0, keepdims=True)
            lsum = sb if lsum is None else lsum + sb
            parts.append(pb.astype(BF16))
        p = jnp.concatenate(parts, axis=0)
        vt = vt_ref[hh, :, pl.ds(pl.multiple_of(g * gk, gk), gk)]
        pv = jnp.dot(vt, p, preferred_element_type=F32)
        m_s[hh] = m_new
        l_s[hh] = alpha * l_s[hh] + lsum
        acc_s[hh] = alpha * acc_s[hh] + pv

    def near_scores(hh, n, bias):
        kb = k_ref[hh, pl.ds(pl.multiple_of(n * blk, blk), blk), :]
        return lax.dot_general(kb, qs[hh], _NT, preferred_element_type=F32) + bias

    def near_pv(hh, n, p):
        vt = vt_ref[hh, :, pl.ds(pl.multiple_of(n * blk, blk), blk)]
        return jnp.dot(vt, p.astype(BF16), preferred_element_type=F32)

    for hh in heads:
        am_prev = select(hh)
        produce(hh, 0, sa_s, ga_s)
        s_own = near_scores(hh, j, bown_ref[hh])
        s_prev = near_scores(hh, jp, bprev_ref[hh])
        m = jnp.maximum(jnp.max(s_own, axis=0, keepdims=True),
                        jnp.max(s_prev, axis=0, keepdims=True) + am_prev)
        p_own = jnp.exp2(s_own - m)
        p_prev = jnp.exp2(s_prev - (m - am_prev))
        m_s[hh] = m
        l_s[hh] = jnp.sum(p_own, axis=0, keepdims=True) + jnp.sum(p_prev, axis=0, keepdims=True)
        acc_s[hh] = near_pv(hh, j, p_own) + near_pv(hh, jp, p_prev)

    def pair(h, carry):
        for hh in heads:
            consume(hh, 2 * h, sa_s, ga_s)
            produce(hh, 2 * h + 1, sb_s, gb_s)
        for hh in heads:
            consume(hh, 2 * h + 1, sb_s, gb_s)
            produce(hh, 2 * h + 2, sa_s, ga_s)
        return carry

    lax.fori_loop(0, ng // 2, pair, 0)

    @pl.when(ng % 2 == 1)
    def _():
        for hh in heads:
            consume(hh, ng - 1, sa_s, ga_s)

    for hh in heads:
        o_ref[hh] = (acc_s[hh] / l_s[hh]).astype(o_ref.dtype)


def _attention(qh, kh, vth, bown, bprev, bfar):
    bhn, s, dh = qh.shape
    nb = s // MOBA_BLOCK
    ah = ATTN_HEADS
    assert N_HEADS % ah == 0 and nb % ATTN_GROUP == 0
    kern = functools.partial(_attn_kernel, n_blocks=nb)
    head_blk = lambda b, j: (b % (N_HEADS // ah), 0, 0)
    grid_spec = pltpu.PrefetchScalarGridSpec(
        num_scalar_prefetch=0,
        grid=(bhn // ah, nb),
        in_specs=[
            pl.BlockSpec(memory_space=pltpu.SMEM),
            pl.BlockSpec((ah, MOBA_BLOCK, dh), lambda b, j: (b, j, 0)),
            pl.BlockSpec((ah, s, dh), lambda b, j: (b, 0, 0)),
            pl.BlockSpec((ah, dh, s), lambda b, j: (b, 0, 0)),
            pl.BlockSpec((ah, MOBA_BLOCK, MOBA_BLOCK), head_blk),
            pl.BlockSpec((ah, MOBA_BLOCK, MOBA_BLOCK), head_blk),
        ],
        out_specs=pl.BlockSpec((ah, dh, MOBA_BLOCK), lambda b, j: (b, 0, j)),
        scratch_shapes=[
            pltpu.VMEM((ah, nb, dh), F32),
            pltpu.VMEM((ah, nb, MOBA_BLOCK), F32),
            pltpu.VMEM((ah, ATTN_GROUP * MOBA_BLOCK, MOBA_BLOCK), F32),
            pltpu.VMEM((ah, ATTN_GROUP * MOBA_BLOCK, MOBA_BLOCK), F32),
            pltpu.VMEM((ah, 1, MOBA_BLOCK), F32),
            pltpu.VMEM((ah, 1, MOBA_BLOCK), F32),
            pltpu.VMEM((ah, 1, MOBA_BLOCK), F32),
            pltpu.VMEM((ah, 1, MOBA_BLOCK), F32),
            pltpu.VMEM((ah, dh, MOBA_BLOCK), F32),
        ],
    )
    return pl.pallas_call(
        kern,
        grid_spec=grid_spec,
        out_shape=jax.ShapeDtypeStruct((bhn, dh, s), BF16),
        compiler_params=pltpu.CompilerParams(
            dimension_semantics=("parallel", "arbitrary"), vmem_limit_bytes=VMEM_LIMIT),
        name="moba_attention",
    )(bfar, qh, kh, vth, bown, bprev)


def _t5_bias_tables(rel_bias):
    blk = MOBA_BLOCK
    n = jnp.arange(2 * blk)
    max_exact = NUM_BUCKETS // 2
    nf = jnp.maximum(n, 1).astype(F32)
    large = max_exact + (jnp.log(nf / max_exact) / math.log(MAX_DISTANCE / max_exact)
                         * (NUM_BUCKETS - max_exact)).astype(I32)
    large = jnp.minimum(large, NUM_BUCKETS - 1)
    bucket = jnp.where(n < max_exact, n, large)
    rel_bias = rel_bias.astype(F32) * LOG2E
    tbl = rel_bias[bucket].T
    key = jnp.arange(blk)[:, None]
    qry = jnp.arange(blk)[None, :]
    d = qry - key
    h = tbl.shape[0]
    wide = 2 * blk + 1
    skew = jnp.broadcast_to(jnp.pad(tbl, ((0, 0), (0, 1)))[:, None, :], (h, blk, wide))
    skew = skew.reshape(h, blk * wide)[:, :blk * 2 * blk].reshape(h, blk, 2 * blk)
    bown = jnp.where(d >= 0, skew[:, :, :blk], MASKED)
    bprev = skew[:, :, blk:]
    bfar = rel_bias[NUM_BUCKETS - 1]
    return bown, bprev, bfar


def _mix_kernel(x_ref, attn_ref, u_ref, halo_ref, gl_ref, pw_ref, ps_ref, woa_ref, wop_ref,
                wout_ref, gffn_ref, rwt_ref, rb_ref,
                x1_ref, hf_ref, idx_ref, rank_ref, gate_ref, cnt_ref,
                ext_s, carry_s, *, tiles_per_seq):
    i = pl.program_id(0)
    tm = MIX_TM

    @pl.when(i == 0)
    def _():
        carry_s[...] = jnp.zeros_like(carry_s)

    first = (i % tiles_per_seq) == 0
    ext_s[0:POOL_HALO, :] = jnp.where(first, 0.0, halo_ref[...])
    ext_s[POOL_HALO:POOL_HALO + tm, :] = u_ref[...]
    pos = (i % tiles_per_seq) * tm + lax.broadcasted_iota(I32, (tm, 1), 0)
    pooled_parts = []
    for g, w in enumerate(POOL_WINDOWS):
        c0, c1 = g * POOL_GROUP_DIM, (g + 1) * POOL_GROUP_DIM
        win = ext_s[POOL_HALO:POOL_HALO + tm, c0:c1]
        for sft in range(1, w):
            win = win + ext_s[POOL_HALO - sft:POOL_HALO - sft + tm, c0:c1]
        cnt = jnp.minimum(pos + 1, w).astype(F32)
        pin = win / cnt - ext_s[POOL_HALO:POOL_HALO + tm, c0:c1]
        pooled_parts.append(jnp.dot(pin.astype(BF16), pw_ref[g], preferred_element_type=F32))
    pooled = jnp.concatenate(pooled_parts, axis=1) * ps_ref[...]

    a = jnp.dot(attn_ref[...], woa_ref[...], preferred_element_type=F32)
    pm = jnp.dot(pooled.astype(BF16), wop_ref[...], preferred_element_type=F32)
    g0 = _sigmoid(gl_ref[:, 0:D_MODEL].astype(F32))
    g1 = _sigmoid(gl_ref[:, D_MODEL:2 * D_MODEL].astype(F32))
    merged = g0 * a + g1 * pm
    x1 = x_ref[...] + jnp.dot(merged.astype(BF16), wout_ref[...], preferred_element_type=F32)
    x1_ref[...] = x1

    hf = _rms(x1, gffn_ref[...])
    hfb = hf.astype(BF16)
    for s in range(ROW_TILES):
        hf_ref[pl.ds(s, tm, stride=ROW_TILES), :] = hfb[:, s * LANES:(s + 1) * LANES].astype(F32)

    logits = lax.dot_general(rwt_ref[...], hfb.astype(F32), _NT,
                             precision=lax.Precision.HIGHEST,
                             preferred_element_type=F32) + rb_ref[...]
    e_iota = lax.broadcasted_iota(I32, logits.shape, 0)
    vals, picks = [], []
    for _ in range(TOP_K):
        mx = jnp.max(logits, axis=0, keepdims=True)
        idx = jnp.min(jnp.where(logits == mx, e_iota, N_EXPERTS), axis=0, keepdims=True)
        pick = e_iota == idx
        vals.append(mx)
        picks.append(pick)
        idx_ref[len(picks) - 1:len(picks), :] = idx
        logits = jnp.where(pick, -jnp.inf, logits)
    ex = [jnp.exp(v - vals[0]) for v in vals]
    den = ex[0] + ex[1] + ex[2] + ex[3]
    for k in range(TOP_K):
        gate_ref[k:k + 1, :] = ex[k] / den

    onehot = jnp.zeros(e_iota.shape, F32)
    for pick in picks:
        onehot = onehot + pick.astype(F32)
    ra = lax.broadcasted_iota(I32, (tm, tm), 0)
    rb = lax.broadcasted_iota(I32, (tm, tm), 1)
    upper = (ra < rb).astype(BF16)
    before = jnp.dot(onehot.astype(BF16), upper, preferred_element_type=F32) + carry_s[...]
    for k, pick in enumerate(picks):
        rank_ref[k:k + 1, :] = jnp.sum(jnp.where(pick, before, 0.0), axis=0,
                                       keepdims=True).astype(I32)
    carry_s[...] = carry_s[...] + jnp.sum(onehot, axis=1, keepdims=True)
    cnt_ref[...] = jnp.broadcast_to(carry_s[...], cnt_ref.shape).astype(I32)


def _mix(x2, attn, u, gl, pw, ps, woa, wop, wout, gffn, rwt, rb, seq):
    t = x2.shape[0]
    tm = MIX_TM
    tiles_per_seq = seq // tm
    halo_per_tile = tm // POOL_HALO
    row = lambda i: (i, 0)
    fixed2 = lambda i: (0, 0)
    fixed3 = lambda i: (0, 0, 0)
    col = lambda i: (0, i)
    kern = functools.partial(_mix_kernel, tiles_per_seq=tiles_per_seq)
    return pl.pallas_call(
        kern,
        grid=(t // tm,),
        in_specs=[
            pl.BlockSpec((tm, D_MODEL), row),
            pl.BlockSpec((tm, ATTN_WIDTH), row),
            pl.BlockSpec((tm, POOL_WIDTH), row),
            pl.BlockSpec((POOL_HALO, POOL_WIDTH),
                         lambda i: (jnp.maximum(i * halo_per_tile - 1, 0), 0)),
            pl.BlockSpec((tm, 2 * D_MODEL), row),
            pl.BlockSpec((POOL_GROUPS, POOL_GROUP_DIM, POOL_GROUP_DIM), fixed3),
            pl.BlockSpec((1, POOL_WIDTH), fixed2),
            pl.BlockSpec((ATTN_WIDTH, D_MODEL), fixed2),
            pl.BlockSpec((POOL_WIDTH, D_MODEL), fixed2),
            pl.BlockSpec((D_MODEL, D_MODEL), fixed2),
            pl.BlockSpec((1, D_MODEL), fixed2),
            pl.BlockSpec((N_EXPERTS, D_MODEL), fixed2),
            pl.BlockSpec((N_EXPERTS, 1), fixed2),
        ],
        out_specs=[
            pl.BlockSpec((tm, D_MODEL), row),
            pl.BlockSpec((tm * ROW_TILES, LANES), row),
            pl.BlockSpec((TOP_K, tm), col),
            pl.BlockSpec((TOP_K, tm), col),
            pl.BlockSpec((TOP_K, tm), col),
            pl.BlockSpec((N_EXPERTS, LANES), fixed2),
        ],
        out_shape=[
            jax.ShapeDtypeStruct((t, D_MODEL), F32),
            jax.ShapeDtypeStruct((t * ROW_TILES, LANES), F32),
            jax.ShapeDtypeStruct((TOP_K, t), I32),
            jax.ShapeDtypeStruct((TOP_K, t), I32),
            jax.ShapeDtypeStruct((TOP_K, t), F32),
            jax.ShapeDtypeStruct((N_EXPERTS, LANES), I32),
        ],
        scratch_shapes=[
            pltpu.VMEM((POOL_HALO + tm, POOL_WIDTH), F32),
            pltpu.VMEM((N_EXPERTS, 1), F32),
        ],
        compiler_params=pltpu.CompilerParams(
            dimension_semantics=("arbitrary",), vmem_limit_bytes=VMEM_LIMIT),
        name="mix_router",
    )(x2, attn, u, u, gl, pw, ps, woa, wop, wout, gffn, rwt, rb)


def _row(ref, r):
    return ref.at[pl.ds(pl.multiple_of(r * ROW_TILES, ROW_TILES), ROW_TILES)]


def _pair_split_matrix():
    n = 2 * LANES
    r = lax.broadcasted_iota(I32, (n, n), 0)
    c = lax.broadcasted_iota(I32, (n, n), 1)
    src = jnp.where(c < LANES, 2 * c, 2 * (c - LANES) + 1)
    return (r == src).astype(BF16)


def _expert_kernel(be_ref, nused_ref, rowsrc_ref, first_ref, par_ref, next_ref,
                   hf_ref, wgu_hbm, wd_hbm, bgu_ref, bd_ref,
                   y_ref, xbuf, wgu_f, wd_f, wgu_s, wd_s, sems, wsems):
    i = pl.program_id(0)
    tm = EXPERT_TM
    n_used = nused_ref[0]
    slot = i % GATHER_DEPTH
    ahead = GATHER_DEPTH - 1

    def start_rows(blk, to_slot, r):
        src = pl.multiple_of(rowsrc_ref[blk * tm + r], ROW_TILES)
        pltpu.make_async_copy(hf_ref.at[pl.ds(src, ROW_TILES)], _row(xbuf.at[to_slot], r),
                              sems.at[to_slot]).start()

    def wait_rows(at_slot):
        pltpu.make_async_copy(hf_ref.at[pl.ds(0, tm * ROW_TILES)], xbuf.at[at_slot],
                              sems.at[at_slot]).wait()

    def weight_copies(e, w_slot, start):
        rows = D_MODEL // WEIGHT_DMA_CHUNKS
        for c in range(WEIGHT_DMA_CHUNKS):
            sl = pl.ds(c * rows, rows)
            for src, dst in ((wgu_hbm, wgu_f), (wd_hbm, wd_f)):
                if start:
                    pltpu.async_copy(src.at[e, sl], dst.at[w_slot, sl], wsems.at[w_slot],
                                     priority=1)
                else:
                    pltpu.make_async_copy(src.at[e, sl], dst.at[w_slot, sl],
                                          wsems.at[w_slot]).wait()

    @pl.when(i == 0)
    def _():
        weight_copies(be_ref[0], 0, True)

        def row(r, carry):
            for blk in range(ahead):
                start_rows(blk, blk, r)
            return carry
        lax.fori_loop(0, tm, row, 0)

    @pl.when(jnp.logical_and(first_ref[i] == 1, i < n_used))
    def _():
        par = par_ref[i]
        weight_copies(be_ref[i], par, False)

        @pl.when(next_ref[i] >= 0)
        def _():
            weight_copies(next_ref[i], 1 - par, True)

        split = _pair_split_matrix()
        for c in range(2 * D_MODEL // (2 * LANES)):
            lo, hi = c * 2 * LANES, (c + 1) * 2 * LANES
            wgu_s[:, lo:hi] = jnp.dot(wgu_f[par, :, lo:hi].astype(BF16), split,
                                      preferred_element_type=F32).astype(BF16)
        wd_s[...] = wd_f[par].astype(BF16)

    @pl.when(i >= n_used)
    def _():
        y_ref[...] = jnp.zeros_like(y_ref)

    @pl.when(jnp.logical_and(i >= n_used, i < n_used + ahead))
    def _():
        wait_rows(slot)

    @pl.when(i < n_used)
    def _():
        wait_rows(slot)
        xb = jnp.concatenate(
            [xbuf[slot, pl.ds(s, tm, stride=ROW_TILES), :] for s in range(ROW_TILES)],
            axis=1).astype(BF16)
        hid_parts = []
        n_chunks = D_MODEL // LANES
        assert n_chunks * GATHER_CHUNK == tm
        for c in range(n_chunks):
            lo, hi = c * 2 * LANES, (c + 1) * 2 * LANES
            gu = jnp.dot(xb, wgu_s[:, lo:hi], preferred_element_type=F32) + bgu_ref[0, :, lo:hi]
            glu = jnp.minimum(gu[:, :LANES], SWIGLU_LIMIT)
            lin = jnp.clip(gu[:, LANES:], -SWIGLU_LIMIT, SWIGLU_LIMIT)
            hid_parts.append(((lin + 1.0) * (glu * _sigmoid(SWIGLU_ALPHA * glu))).astype(BF16))
            for r in range(c * GATHER_CHUNK, (c + 1) * GATHER_CHUNK):
                start_rows(i + ahead, (i + ahead) % GATHER_DEPTH, r)
        hid = jnp.concatenate(hid_parts, axis=1)
        y = jnp.dot(hid, wd_s[...], preferred_element_type=F32) + bd_ref[0]
        for s in range(ROW_TILES):
            y_ref[pl.ds(s, tm, stride=ROW_TILES), :] = y[:, s * LANES:(s + 1) * LANES]


def _experts(blk_expert, n_used, row_src, first, parity, next_expert, hf_rows, wgu, wd,
             bgu_split, bd):
    n_blk = blk_expert.shape[0]
    tm = EXPERT_TM

    def ymap(i, be, nu, *_):
        return (i, 0)

    def bmap(i, be, nu, *_):
        return (be[jnp.minimum(i, nu[0] - 1)], 0, 0)

    grid_spec = pltpu.PrefetchScalarGridSpec(
        num_scalar_prefetch=6,
        grid=(n_blk,),
        in_specs=[
            pl.BlockSpec(memory_space=pl.ANY),
            pl.BlockSpec(memory_space=pl.ANY),
            pl.BlockSpec(memory_space=pl.ANY),
            pl.BlockSpec((1, 1, 2 * D_MODEL), bmap),
            pl.BlockSpec((1, 1, D_MODEL), bmap),
        ],
        out_specs=pl.BlockSpec((tm * ROW_TILES, LANES), ymap),
        scratch_shapes=[
            pltpu.VMEM((GATHER_DEPTH, tm * ROW_TILES, LANES), F32),
            pltpu.VMEM((2, D_MODEL, 2 * D_MODEL), F32),
            pltpu.VMEM((2, D_MODEL, D_MODEL), F32),
            pltpu.VMEM((D_MODEL, 2 * D_MODEL), BF16),
            pltpu.VMEM((D_MODEL, D_MODEL), BF16),
            pltpu.SemaphoreType.DMA((GATHER_DEPTH,)),
            pltpu.SemaphoreType.DMA((2,)),
        ],
    )
    return pl.pallas_call(
        _expert_kernel,
        grid_spec=grid_spec,
        out_shape=jax.ShapeDtypeStruct((n_blk * tm * ROW_TILES, LANES), F32),
        compiler_params=pltpu.CompilerParams(
            dimension_semantics=("arbitrary",), vmem_limit_bytes=VMEM_LIMIT),
        name="expert_ffn",
    )(blk_expert, n_used, row_src, first, parity, next_expert, hf_rows, wgu, wd, bgu_split, bd)


def _combine_kernel(dest_ref, x1_ref, gate_ref, ys_ref, p_ref, gple_ref, wpg_ref, wpp_ref,
                    gfin_ref, o_ref, ybuf, sems):
    i = pl.program_id(0)
    n_steps = pl.num_programs(0)
    tm = OUT_TM
    slot = i % 2

    def gather(tile, to_slot):
        def tok(r, carry):
            t = tile * tm + r
            for k in range(TOP_K):
                pltpu.async_copy(_row(ys_ref, dest_ref[t * TOP_K + k]),
                                 _row(ybuf.at[to_slot], k * tm + r), sems.at[to_slot],
                                 priority=k % 2)
            return carry
        lax.fori_loop(0, tm, tok, 0)

    @pl.when(i == 0)
    def _():
        gather(0, 0)

    @pl.when(i + 1 < n_steps)
    def _():
        gather(i + 1, 1 - slot)

    pltpu.make_async_copy(ys_ref.at[pl.ds(0, TOP_K * tm * ROW_TILES)], ybuf.at[slot],
                          sems.at[slot]).wait()

    moe = None
    for k in range(TOP_K):
        yk = jnp.concatenate(
            [ybuf[slot, pl.ds(k * tm * ROW_TILES + s, tm, stride=ROW_TILES), :]
             for s in range(ROW_TILES)], axis=1)
        term = gate_ref[:, k:k + 1] * yk
        moe = term if moe is None else moe + term
    x2 = x1_ref[...] + moe
    hp = _rms(x2, gple_ref[...]).astype(BF16)
    pg = _sigmoid(jnp.dot(hp, wpg_ref[...], preferred_element_type=F32))
    proj = jnp.dot(p_ref[...].astype(BF16), wpp_ref[...], preferred_element_type=F32)
    x3 = x2 + pg * proj
    o_ref[...] = _rms(x3, gfin_ref[...])


def _combine(dest_flat, x1, gates_tk, ys, p2, gple, wpg, wpp, gfin):
    t = x1.shape[0]
    tm = OUT_TM
    ple = p2.shape[1]
    row = lambda i, d: (i, 0)
    fixed = lambda i, d: (0, 0)
    grid_spec = pltpu.PrefetchScalarGridSpec(
        num_scalar_prefetch=1,
        grid=(t // tm,),
        in_specs=[
            pl.BlockSpec((tm, D_MODEL), row),
            pl.BlockSpec((tm, TOP_K), row),
            pl.BlockSpec(memory_space=pl.ANY),
            pl.BlockSpec((tm, ple), row),
            pl.BlockSpec((1, D_MODEL), fixed),
            pl.BlockSpec((D_MODEL, D_MODEL), fixed),
            pl.BlockSpec((ple, D_MODEL), fixed),
            pl.BlockSpec((1, D_MODEL), fixed),
        ],
        out_specs=pl.BlockSpec((tm, D_MODEL), row),
        scratch_shapes=[
            pltpu.VMEM((2, TOP_K * tm * ROW_TILES, LANES), F32),
            pltpu.SemaphoreType.DMA((2,)),
        ],
    )
    return pl.pallas_call(
        _combine_kernel,
        grid_spec=grid_spec,
        out_shape=jax.ShapeDtypeStruct((t, D_MODEL), F32),
        compiler_params=pltpu.CompilerParams(
            dimension_semantics=("arbitrary",), vmem_limit_bytes=VMEM_LIMIT),
        name="combine_ple_final",
    )(dest_flat, x1, gates_tk, ys, p2, gple, wpg, wpp, gfin)


def kernel(x, p, rel_bias, norm_mix_g, w_in, pool_w, pool_scale, w_o_attn, w_o_pool, w_out,
           norm_ffn_g, router_w, router_b, w_gate_up, b_gate_up, w_down, b_down,
           norm_ple_g, w_ple_gate, w_ple_proj, norm_final_g):
    b, s, d = x.shape
    depth = w_in.shape[0]
    t = b * s
    assert d == D_MODEL and s % MOBA_BLOCK == 0 and t % IN_TM == 0 and s % MIX_TM == 0
    assert depth == 1, "the final norm is fused into the layer's last kernel"
    n_pad = t * TOP_K + (N_EXPERTS + GATHER_DEPTH - 2) * EXPERT_TM
    n_blk = n_pad // EXPERT_TM
    bown, bprev, bfar = _t5_bias_tables(rel_bias)

    x2 = x.reshape(t, d)
    for i in range(depth):
        q, k, v, u, gl = _in_proj(x2, norm_mix_g[i].reshape(1, d), w_in[i].astype(BF16))

        def heads(a):
            return a.reshape(b, s, N_HEADS, HEAD_DIM).transpose(0, 2, 1, 3).reshape(
                b * N_HEADS, s, HEAD_DIM)

        vth = v.reshape(b, s, N_HEADS, HEAD_DIM).transpose(0, 2, 3, 1).reshape(
            b * N_HEADS, HEAD_DIM, s)
        attn_t = _attention(heads(q), heads(k), vth, bown, bprev, bfar)
        attn = attn_t.reshape(b, N_HEADS, HEAD_DIM, s).transpose(0, 3, 1, 2).reshape(
            t, ATTN_WIDTH)

        x1, hf_rows, idx_kt, rank_kt, gate_kt, cnt = _mix(
            x2, attn, u, gl, pool_w[i].astype(BF16), pool_scale[i].reshape(1, POOL_WIDTH),
            w_o_attn[i].astype(BF16), w_o_pool[i].astype(BF16), w_out[i].astype(BF16),
            norm_ffn_g[i].reshape(1, d), router_w[i].T, router_b[i].reshape(N_EXPERTS, 1), s)

        counts = cnt[:, 0]
        padded = (counts + EXPERT_TM - 1) // EXPERT_TM * EXPERT_TM
        pend = jnp.cumsum(padded)
        pstart = pend - padded
        e_ids = jnp.arange(N_EXPERTS, dtype=I32)
        pstart_of = jnp.sum(jnp.where(idx_kt[..., None] == e_ids, pstart, 0), axis=-1)
        dest_flat = (pstart_of + rank_kt).T.reshape(-1).astype(I32)
        blk_row0 = jnp.arange(n_blk, dtype=I32) * EXPERT_TM
        blk_expert = jnp.minimum(jnp.sum(pend[None, :] <= blk_row0[:, None], axis=1),
                                 N_EXPERTS - 1).astype(I32)
        n_used = (pend[-1:] // EXPERT_TM).astype(I32)
        row_tok = jnp.zeros((n_pad,), I32).at[dest_flat].set(
            jnp.arange(t * TOP_K, dtype=I32) // TOP_K, unique_indices=True)

        bgu = b_gate_up[i].reshape(N_EXPERTS, D_MODEL // LANES, LANES, 2).transpose(
            0, 1, 3, 2).reshape(N_EXPERTS, 1, 2 * D_MODEL)
        first = jnp.concatenate([jnp.ones((1,), I32),
                                 (blk_expert[1:] != blk_expert[:-1]).astype(I32)])
        parity = ((jnp.cumsum(first) - 1) % 2).astype(I32)
        later = jnp.logical_and(e_ids[None, :] > e_ids[:, None], (padded > 0)[None, :])
        next_e = jnp.min(jnp.where(later, e_ids[None, :], N_EXPERTS), axis=1)
        next_e = jnp.where(next_e == N_EXPERTS, -1, next_e).astype(I32)
        next_expert = jnp.sum(jnp.where(blk_expert[:, None] == e_ids, next_e, 0),
                              axis=1).astype(I32)
        ys = _experts(blk_expert, n_used, row_tok * ROW_TILES, first, parity, next_expert,
                      hf_rows, w_gate_up[i], w_down[i], bgu, b_down[i][:, None, :])

        x2 = _combine(dest_flat, x1, gate_kt.T, ys, p[i].reshape(t, -1),
                      norm_ple_g[i].reshape(1, d), w_ple_gate[i].astype(BF16),
                      w_ple_proj[i].astype(BF16), norm_final_g.reshape(1, d))
    return x2.reshape(b, s, d)
```

```python
import functools
import math

import jax
import jax.numpy as jnp
from jax import lax
from jax.experimental import pallas as pl
from jax.experimental.pallas import tpu as pltpu

F32 = jnp.float32
BF16 = jnp.bfloat16
I32 = jnp.int32

D_MODEL = 1024
N_HEADS = 8
HEAD_DIM = 64
ATTN_WIDTH = N_HEADS * HEAD_DIM
MOBA_BLOCK = 256
MOBA_TOPK = 3
NUM_BUCKETS = 32
MAX_DISTANCE = 128
POOL_WINDOWS = (2, 4, 8, 16)
POOL_GROUPS = 4
POOL_GROUP_DIM = 128
POOL_WIDTH = POOL_GROUPS * POOL_GROUP_DIM
N_EXPERTS = 32
TOP_K = 4
SWIGLU_LIMIT = 7.0
SWIGLU_ALPHA = 1.702
RMS_EPS = 1e-6

LANES = 128
SUBLANES = 8
ROW_TILES = D_MODEL // LANES

IN_TM = 512
MIX_TM = 256
EXPERT_TM = 256
OUT_TM = 256
ATTN_GROUP = 4
ATTN_HEADS = 4
GATHER_CHUNK = 32
WEIGHT_DMA_CHUNKS = 4
COMBINE_PIECES = 8
GATHER_DEPTH = 4
POOL_HALO = 16
MASKED = -1e30
LOG2E = math.log2(math.e)
Q_SCALE = HEAD_DIM ** -0.5 * LOG2E

VMEM_LIMIT = 56 * 1024 * 1024

_NT = (((1,), (1,)), ((), ()))


def _rms(x, g):
    ms = jnp.mean(x * x, axis=-1, keepdims=True)
    return x * lax.rsqrt(ms + RMS_EPS) * g


def _sigmoid(x):
    return 1.0 / (1.0 + jnp.exp(-x))


def _in_proj_kernel(x_ref, g_ref, w_ref, q_ref, k_ref, v_ref, u_ref, gl_ref):
    h = _rms(x_ref[...], g_ref[...]).astype(BF16)
    aw = ATTN_WIDTH

    def proj(lo, hi):
        return jnp.dot(h, w_ref[:, lo:hi], preferred_element_type=F32)

    q_ref[...] = (proj(0, aw) * Q_SCALE).astype(BF16)
    k_ref[...] = proj(aw, 2 * aw).astype(BF16)
    v_ref[...] = proj(2 * aw, 3 * aw).astype(BF16)
    u_ref[...] = proj(3 * aw, 3 * aw + POOL_WIDTH)
    base = 3 * aw + POOL_WIDTH
    for c in range(2):
        lo = base + c * D_MODEL
        gl_ref[:, c * D_MODEL:(c + 1) * D_MODEL] = proj(lo, lo + D_MODEL).astype(BF16)


def _in_proj(x2, g, w_bf16):
    t = x2.shape[0]
    in_cols = w_bf16.shape[1]
    row = lambda i: (i, 0)
    fixed = lambda i: (0, 0)
    return pl.pallas_call(
        _in_proj_kernel,
        grid=(t // IN_TM,),
        in_specs=[
            pl.BlockSpec((IN_TM, D_MODEL), row),
            pl.BlockSpec((1, D_MODEL), fixed),
            pl.BlockSpec((D_MODEL, in_cols), fixed),
        ],
        out_specs=[
            pl.BlockSpec((IN_TM, ATTN_WIDTH), row),
            pl.BlockSpec((IN_TM, ATTN_WIDTH), row),
            pl.BlockSpec((IN_TM, ATTN_WIDTH), row),
            pl.BlockSpec((IN_TM, POOL_WIDTH), row),
            pl.BlockSpec((IN_TM, 2 * D_MODEL), row),
        ],
        out_shape=[
            jax.ShapeDtypeStruct((t, ATTN_WIDTH), BF16),
            jax.ShapeDtypeStruct((t, ATTN_WIDTH), BF16),
            jax.ShapeDtypeStruct((t, ATTN_WIDTH), BF16),
            jax.ShapeDtypeStruct((t, POOL_WIDTH), F32),
            jax.ShapeDtypeStruct((t, 2 * D_MODEL), BF16),
        ],
        compiler_params=pltpu.CompilerParams(
            dimension_semantics=("parallel",), vmem_limit_bytes=VMEM_LIMIT),
        name="in_proj",
    )(x2, g, w_bf16)


def _attn_kernel(bfar_ref, q_ref, k_ref, vt_ref, bown_ref, bprev_ref, o_ref,
                 kmean_s, am_s, sa_s, sb_s, ga_s, gb_s, m_s, l_s, acc_s, *, n_blocks):
    hp = pl.program_id(0)
    j = pl.program_id(1)
    blk = MOBA_BLOCK
    grp = ATTN_GROUP
    gk = grp * blk
    n_groups = n_blocks // grp
    n_far = jnp.maximum(j - 1, 0)
    ng = (n_far + grp - 1) // grp
    jp = jnp.maximum(j - 1, 0)
    heads = range(ATTN_HEADS)

    @pl.when(j == 0)
    def _():
        for hh in heads:
            for n in range(n_blocks):
                kb = k_ref[hh, n * blk:(n + 1) * blk, :].astype(F32)
                kmean_s[hh, n:n + 1, :] = jnp.mean(kb, axis=0, keepdims=True)

    qs = [q_ref[hh] for hh in heads]

    def select(hh):
        gate = lax.dot_general(kmean_s[hh], qs[hh].astype(F32), _NT,
                               precision=lax.Precision.HIGHEST, preferred_element_type=F32)
        n_iota = lax.broadcasted_iota(I32, gate.shape, 0)
        past = n_iota < j
        gate = jnp.where(past, gate, jnp.finfo(F32).min)
        sel = jnp.zeros(gate.shape, jnp.bool_)
        for _ in range(MOBA_TOPK):
            mx = jnp.max(gate, axis=0, keepdims=True)
            idx = jnp.min(jnp.where(gate == mx, n_iota, n_blocks), axis=0, keepdims=True)
            pick = n_iota == idx
            sel = jnp.logical_or(sel, pick)
            gate = jnp.where(pick, -jnp.inf, gate)
        sel = jnp.logical_and(sel, past)
        bfar = bfar_ref[(hp * ATTN_HEADS + hh) % N_HEADS]
        am_s[hh] = jnp.where(jnp.logical_and(sel, n_iota < j - 1), bfar, MASKED)
        return jnp.max(jnp.where(jnp.logical_and(sel, n_iota == j - 1), 0.0, MASKED),
                       axis=0, keepdims=True)

    def produce(hh, g, s_ref, gmax_ref):
        g = jnp.minimum(g, n_groups - 1)
        kb = k_ref[hh, pl.ds(pl.multiple_of(g * gk, gk), gk), :]
        s = lax.dot_general(kb, qs[hh], _NT, preferred_element_type=F32)
        s_ref[hh] = s
        gmax = None
        for b in range(grp):
            mb = (jnp.max(s[b * blk:(b + 1) * blk], axis=0, keepdims=True)
                  + am_s[hh, pl.ds(g * grp + b, 1), :])
            gmax = mb if gmax is None else jnp.maximum(gmax, mb)
        gmax_ref[hh] = gmax

    def consume(hh, g, s_ref, gmax_ref):
        m_old = m_s[hh]
        m_new = jnp.maximum(m_old, gmax_ref[hh])
        alpha = jnp.exp2(m_old - m_new)
        parts, lsum = [], None
        for b in range(grp):
            shift = m_new - am_s[hh, pl.ds(g * grp + b, 1), :]
            pb = jnp.exp2(s_ref[hh, b * blk:(b + 1) * blk, :] - shift)
            sb = jnp.sum(pb, axis=0, keepdims=True)
            lsum = sb if lsum is None else lsum + sb
            parts.append(pb.astype(BF16))
        p = jnp.concatenate(parts, axis=0)
        vt = vt_ref[hh, :, pl.ds(pl.multiple_of(g * gk, gk), gk)]
        pv = jnp.dot(vt, p, preferred_element_type=F32)
        m_s[hh] = m_new
        l_s[hh] = alpha * l_s[hh] + lsum
        acc_s[hh] = alpha * acc_s[hh] + pv

    def near_scores(hh, n, bias):
        kb = k_ref[hh, pl.ds(pl.multiple_of(n * blk, blk), blk), :]
        return lax.dot_general(kb, qs[hh], _NT, preferred_element_type=F32) + bias

    def near_pv(hh, n, p):
        vt = vt_ref[hh, :, pl.ds(pl.multiple_of(n * blk, blk), blk)]
        return jnp.dot(vt, p.astype(BF16), preferred_element_type=F32)

    for hh in heads:
        am_prev = select(hh)
        produce(hh, 0, sa_s, ga_s)
        s_own = near_scores(hh, j, bown_ref[hh])
        s_prev = near_scores(hh, jp, bprev_ref[hh])
        m = jnp.maximum(jnp.max(s_own, axis=0, keepdims=True),
                        jnp.max(s_prev, axis=0, keepdims=True) + am_prev)
        p_own = jnp.exp2(s_own - m)
        p_prev = jnp.exp2(s_prev - (m - am_prev))
        m_s[hh] = m
        l_s[hh] = jnp.sum(p_own, axis=0, keepdims=True) + jnp.sum(p_prev, axis=0, keepdims=True)
        acc_s[hh] = near_pv(hh, j, p_own) + near_pv(hh, jp, p_prev)

    def pair(h, carry):
        for hh in heads:
            consume(hh, 2 * h, sa_s, ga_s)
            produce(hh, 2 * h + 1, sb_s, gb_s)
        for hh in heads:
            consume(hh, 2 * h + 1, sb_s, gb_s)
            produce(hh, 2 * h + 2, sa_s, ga_s)
        return carry

    lax.fori_loop(0, ng // 2, pair, 0)

    @pl.when(ng % 2 == 1)
    def _():
        for hh in heads:
            consume(hh, ng - 1, sa_s, ga_s)

    for hh in heads:
        o_ref[hh] = (acc_s[hh] / l_s[hh]).astype(o_ref.dtype)


def _attention(qh, kh, vth, bown, bprev, bfar):
    bhn, s, dh = qh.shape
    nb = s // MOBA_BLOCK
    ah = ATTN_HEADS
    assert N_HEADS % ah == 0 and nb % ATTN_GROUP == 0
    kern = functools.partial(_attn_kernel, n_blocks=nb)
    head_blk = lambda b, j: (b % (N_HEADS // ah), 0, 0)
    grid_spec = pltpu.PrefetchScalarGridSpec(
        num_scalar_prefetch=0,
        grid=(bhn // ah, nb),
        in_specs=[
            pl.BlockSpec(memory_space=pltpu.SMEM),
            pl.BlockSpec((ah, MOBA_BLOCK, dh), lambda b, j: (b, j, 0)),
            pl.BlockSpec((ah, s, dh), lambda b, j: (b, 0, 0)),
            pl.BlockSpec((ah, dh, s), lambda b, j: (b, 0, 0)),
            pl.BlockSpec((ah, MOBA_BLOCK, MOBA_BLOCK), head_blk),
            pl.BlockSpec((ah, MOBA_BLOCK, MOBA_BLOCK), head_blk),
        ],
        out_specs=pl.BlockSpec((ah, dh, MOBA_BLOCK), lambda b, j: (b, 0, j)),
        scratch_shapes=[
            pltpu.VMEM((ah, nb, dh), F32),
            pltpu.VMEM((ah, nb, MOBA_BLOCK), F32),
            pltpu.VMEM((ah, ATTN_GROUP * MOBA_BLOCK, MOBA_BLOCK), F32),
            pltpu.VMEM((ah, ATTN_GROUP * MOBA_BLOCK, MOBA_BLOCK), F32),
            pltpu.VMEM((ah, 1, MOBA_BLOCK), F32),
            pltpu.VMEM((ah, 1, MOBA_BLOCK), F32),
            pltpu.VMEM((ah, 1, MOBA_BLOCK), F32),
            pltpu.VMEM((ah, 1, MOBA_BLOCK), F32),
            pltpu.VMEM((ah, dh, MOBA_BLOCK), F32),
        ],
    )
    return pl.pallas_call(
        kern,
        grid_spec=grid_spec,
        out_shape=jax.ShapeDtypeStruct((bhn, dh, s), BF16),
        compiler_params=pltpu.CompilerParams(
            dimension_semantics=("parallel", "arbitrary"), vmem_limit_bytes=VMEM_LIMIT),
        name="moba_attention",
    )(bfar, qh, kh, vth, bown, bprev)


def _t5_bias_tables(rel_bias):
    blk = MOBA_BLOCK
    n = jnp.arange(2 * blk)
    max_exact = NUM_BUCKETS // 2
    nf = jnp.maximum(n, 1).astype(F32)
    large = max_exact + (jnp.log(nf / max_exact) / math.log(MAX_DISTANCE / max_exact)
                         * (NUM_BUCKETS - max_exact)).astype(I32)
    large = jnp.minimum(large, NUM_BUCKETS - 1)
    bucket = jnp.where(n < max_exact, n, large)
    rel_bias = rel_bias.astype(F32) * LOG2E
    tbl = rel_bias[bucket].T
    key = jnp.arange(blk)[:, None]
    qry = jnp.arange(blk)[None, :]
    d = qry - key
    h = tbl.shape[0]
    wide = 2 * blk + 1
    skew = jnp.broadcast_to(jnp.pad(tbl, ((0, 0), (0, 1)))[:, None, :], (h, blk, wide))
    skew = skew.reshape(h, blk * wide)[:, :blk * 2 * blk].reshape(h, blk, 2 * blk)
    bown = jnp.where(d >= 0, skew[:, :, :blk], MASKED)
    bprev = skew[:, :, blk:]
    bfar = rel_bias[NUM_BUCKETS - 1]
    return bown, bprev, bfar


def _mix_kernel(x_ref, attn_ref, u_ref, halo_ref, gl_ref, pw_ref, ps_ref, woa_ref, wop_ref,
                wout_ref, gffn_ref, rwt_ref, rb_ref,
                x1_ref, hf_ref, idx_ref, rank_ref, gate_ref, cnt_ref,
                ext_s, carry_s, *, tiles_per_seq):
    i = pl.program_id(0)
    tm = MIX_TM

    @pl.when(i == 0)
    def _():
        carry_s[...] = jnp.zeros_like(carry_s)

    first = (i % tiles_per_seq) == 0
    ext_s[0:POOL_HALO, :] = jnp.where(first, 0.0, halo_ref[...])
    ext_s[POOL_HALO:POOL_HALO + tm, :] = u_ref[...]
    pos = (i % tiles_per_seq) * tm + lax.broadcasted_iota(I32, (tm, 1), 0)
    pooled_parts = []
    for g, w in enumerate(POOL_WINDOWS):
        c0, c1 = g * POOL_GROUP_DIM, (g + 1) * POOL_GROUP_DIM
        win = ext_s[POOL_HALO:POOL_HALO + tm, c0:c1]
        for sft in range(1, w):
            win = win + ext_s[POOL_HALO - sft:POOL_HALO - sft + tm, c0:c1]
        cnt = jnp.minimum(pos + 1, w).astype(F32)
        pin = win / cnt - ext_s[POOL_HALO:POOL_HALO + tm, c0:c1]
        pooled_parts.append(jnp.dot(pin.astype(BF16), pw_ref[g], preferred_element_type=F32))
    pooled = jnp.concatenate(pooled_parts, axis=1) * ps_ref[...]

    a = jnp.dot(attn_ref[...], woa_ref[...], preferred_element_type=F32)
    pm = jnp.dot(pooled.astype(BF16), wop_ref[...], preferred_element_type=F32)
    g0 = _sigmoid(gl_ref[:, 0:D_MODEL].astype(F32))
    g1 = _sigmoid(gl_ref[:, D_MODEL:2 * D_MODEL].astype(F32))
    merged = g0 * a + g1 * pm
    x1 = x_ref[...] + jnp.dot(merged.astype(BF16), wout_ref[...], preferred_element_type=F32)
    x1_ref[...] = x1

    hf = _rms(x1, gffn_ref[...])
    hfb = hf.astype(BF16)
    for s in range(ROW_TILES):
        hf_ref[pl.ds(s, tm, stride=ROW_TILES), :] = hfb[:, s * LANES:(s + 1) * LANES].astype(F32)

    logits = lax.dot_general(rwt_ref[...], hfb.astype(F32), _NT,
                             precision=lax.Precision.HIGHEST,
                             preferred_element_type=F32) + rb_ref[...]
    e_iota = lax.broadcasted_iota(I32, logits.shape, 0)
    vals, picks = [], []
    for _ in range(TOP_K):
        mx = jnp.max(logits, axis=0, keepdims=True)
        idx = jnp.min(jnp.where(logits == mx, e_iota, N_EXPERTS), axis=0, keepdims=True)
        pick = e_iota == idx
        vals.append(mx)
        picks.append(pick)
        idx_ref[len(picks) - 1:len(picks), :] = idx
        logits = jnp.where(pick, -jnp.inf, logits)
    ex = [jnp.exp(v - vals[0]) for v in vals]
    den = ex[0] + ex[1] + ex[2] + ex[3]
    for k in range(TOP_K):
        gate_ref[k:k + 1, :] = ex[k] / den

    onehot = jnp.zeros(e_iota.shape, F32)
    for pick in picks:
        onehot = onehot + pick.astype(F32)
    ra = lax.broadcasted_iota(I32, (tm, tm), 0)
    rb = lax.broadcasted_iota(I32, (tm, tm), 1)
    upper = (ra < rb).astype(BF16)
    before = jnp.dot(onehot.astype(BF16), upper, preferred_element_type=F32) + carry_s[...]
    for k, pick in enumerate(picks):
        rank_ref[k:k + 1, :] = jnp.sum(jnp.where(pick, before, 0.0), axis=0,
                                       keepdims=True).astype(I32)
    carry_s[...] = carry_s[...] + jnp.sum(onehot, axis=1, keepdims=True)
    cnt_ref[...] = jnp.broadcast_to(carry_s[...], cnt_ref.shape).astype(I32)


def _mix(x2, attn, u, gl, pw, ps, woa, wop, wout, gffn, rwt, rb, seq):
    t = x2.shape[0]
    tm = MIX_TM
    tiles_per_seq = seq // tm
    halo_per_tile = tm // POOL_HALO
    row = lambda i: (i, 0)
    fixed2 = lambda i: (0, 0)
    fixed3 = lambda i: (0, 0, 0)
    col = lambda i: (0, i)
    kern = functools.partial(_mix_kernel, tiles_per_seq=tiles_per_seq)
    return pl.pallas_call(
        kern,
        grid=(t // tm,),
        in_specs=[
            pl.BlockSpec((tm, D_MODEL), row),
            pl.BlockSpec((tm, ATTN_WIDTH), row),
            pl.BlockSpec((tm, POOL_WIDTH), row),
            pl.BlockSpec((POOL_HALO, POOL_WIDTH),
                         lambda i: (jnp.maximum(i * halo_per_tile - 1, 0), 0)),
            pl.BlockSpec((tm, 2 * D_MODEL), row),
            pl.BlockSpec((POOL_GROUPS, POOL_GROUP_DIM, POOL_GROUP_DIM), fixed3),
            pl.BlockSpec((1, POOL_WIDTH), fixed2),
            pl.BlockSpec((ATTN_WIDTH, D_MODEL), fixed2),
            pl.BlockSpec((POOL_WIDTH, D_MODEL), fixed2),
            pl.BlockSpec((D_MODEL, D_MODEL), fixed2),
            pl.BlockSpec((1, D_MODEL), fixed2),
            pl.BlockSpec((N_EXPERTS, D_MODEL), fixed2),
            pl.BlockSpec((N_EXPERTS, 1), fixed2),
        ],
        out_specs=[
            pl.BlockSpec((tm, D_MODEL), row),
            pl.BlockSpec((tm * ROW_TILES, LANES), row),
            pl.BlockSpec((TOP_K, tm), col),
            pl.BlockSpec((TOP_K, tm), col),
            pl.BlockSpec((TOP_K, tm), col),
            pl.BlockSpec((N_EXPERTS, LANES), fixed2),
        ],
        out_shape=[
            jax.ShapeDtypeStruct((t, D_MODEL), F32),
            jax.ShapeDtypeStruct((t * ROW_TILES, LANES), F32),
            jax.ShapeDtypeStruct((TOP_K, t), I32),
            jax.ShapeDtypeStruct((TOP_K, t), I32),
            jax.ShapeDtypeStruct((TOP_K, t), F32),
            jax.ShapeDtypeStruct((N_EXPERTS, LANES), I32),
        ],
        scratch_shapes=[
            pltpu.VMEM((POOL_HALO + tm, POOL_WIDTH), F32),
            pltpu.VMEM((N_EXPERTS, 1), F32),
        ],
        compiler_params=pltpu.CompilerParams(
            dimension_semantics=("arbitrary",), vmem_limit_bytes=VMEM_LIMIT),
        name="mix_router",
    )(x2, attn, u, u, gl, pw, ps, woa, wop, wout, gffn, rwt, rb)


def _row(ref, r):
    return ref.at[pl.ds(pl.multiple_of(r * ROW_TILES, ROW_TILES), ROW_TILES)]


def _pair_split_matrix():
    n = 2 * LANES
    r = lax.broadcasted_iota(I32, (n, n), 0)
    c = lax.broadcasted_iota(I32, (n, n), 1)
    src = jnp.where(c < LANES, 2 * c, 2 * (c - LANES) + 1)
    return (r == src).astype(BF16)


def _expert_kernel(be_ref, nused_ref, rowsrc_ref, first_ref, par_ref, next_ref,
                   hf_ref, wgu_hbm, wd_hbm, bgu_ref, bd_ref,
                   y_ref, xbuf, wgu_f, wd_f, wgu_s, wd_s, sems, wsems):
    i = pl.program_id(0)
    tm = EXPERT_TM
    n_used = nused_ref[0]
    slot = i % GATHER_DEPTH
    ahead = GATHER_DEPTH - 1

    def start_rows(blk, to_slot, r):
        src = pl.multiple_of(rowsrc_ref[blk * tm + r], ROW_TILES)
        pltpu.make_async_copy(hf_ref.at[pl.ds(src, ROW_TILES)], _row(xbuf.at[to_slot], r),
                              sems.at[to_slot]).start()

    def wait_rows(at_slot):
        pltpu.make_async_copy(hf_ref.at[pl.ds(0, tm * ROW_TILES)], xbuf.at[at_slot],
                              sems.at[at_slot]).wait()

    def weight_copies(e, w_slot, start):
        rows = D_MODEL // WEIGHT_DMA_CHUNKS
        for c in range(WEIGHT_DMA_CHUNKS):
            sl = pl.ds(c * rows, rows)
            for src, dst in ((wgu_hbm, wgu_f), (wd_hbm, wd_f)):
                if start:
                    pltpu.async_copy(src.at[e, sl], dst.at[w_slot, sl], wsems.at[w_slot],
                                     priority=1)
                else:
                    pltpu.make_async_copy(src.at[e, sl], dst.at[w_slot, sl],
                                          wsems.at[w_slot]).wait()

    @pl.when(i == 0)
    def _():
        weight_copies(be_ref[0], 0, True)

        def row(r, carry):
            for blk in range(ahead):
                start_rows(blk, blk, r)
            return carry
        lax.fori_loop(0, tm, row, 0)

    @pl.when(jnp.logical_and(first_ref[i] == 1, i < n_used))
    def _():
        par = par_ref[i]
        weight_copies(be_ref[i], par, False)

        @pl.when(next_ref[i] >= 0)
        def _():
            weight_copies(next_ref[i], 1 - par, True)

        split = _pair_split_matrix()
        for c in range(2 * D_MODEL // (2 * LANES)):
            lo, hi = c * 2 * LANES, (c + 1) * 2 * LANES
            wgu_s[:, lo:hi] = jnp.dot(wgu_f[par, :, lo:hi].astype(BF16), split,
                                      preferred_element_type=F32).astype(BF16)
        wd_s[...] = wd_f[par].astype(BF16)

    @pl.when(i >= n_used)
    def _():
        y_ref[...] = jnp.zeros_like(y_ref)

    @pl.when(jnp.logical_and(i >= n_used, i < n_used + ahead))
    def _():
        wait_rows(slot)

    @pl.when(i < n_used)
    def _():
        wait_rows(slot)
        xb = jnp.concatenate(
            [xbuf[slot, pl.ds(s, tm, stride=ROW_TILES), :] for s in range(ROW_TILES)],
            axis=1).astype(BF16)
        hid_parts = []
        n_chunks = D_MODEL // LANES
        assert n_chunks * GATHER_CHUNK == tm
        for c in range(n_chunks):
            lo, hi = c * 2 * LANES, (c + 1) * 2 * LANES
            gu = jnp.dot(xb, wgu_s[:, lo:hi], preferred_element_type=F32) + bgu_ref[0, :, lo:hi]
            glu = jnp.minimum(gu[:, :LANES], SWIGLU_LIMIT)
            lin = jnp.clip(gu[:, LANES:], -SWIGLU_LIMIT, SWIGLU_LIMIT)
            hid_parts.append(((lin + 1.0) * (glu * _sigmoid(SWIGLU_ALPHA * glu))).astype(BF16))
            for r in range(c * GATHER_CHUNK, (c + 1) * GATHER_CHUNK):
                start_rows(i + ahead, (i + ahead) % GATHER_DEPTH, r)
        hid = jnp.concatenate(hid_parts, axis=1)
        y = jnp.dot(hid, wd_s[...], preferred_element_type=F32) + bd_ref[0]
        for s in range(ROW_TILES):
            y_ref[pl.ds(s, tm, stride=ROW_TILES), :] = y[:, s * LANES:(s + 1) * LANES]


def _experts(blk_expert, n_used, row_src, first, parity, next_expert, hf_rows, wgu, wd,
             bgu_split, bd):
    n_blk = blk_expert.shape[0]
    tm = EXPERT_TM

    def ymap(i, be, nu, *_):
        return (i, 0)

    def bmap(i, be, nu, *_):
        return (be[jnp.minimum(i, nu[0] - 1)], 0, 0)

    grid_spec = pltpu.PrefetchScalarGridSpec(
        num_scalar_prefetch=6,
        grid=(n_blk,),
        in_specs=[
            pl.BlockSpec(memory_space=pl.ANY),
            pl.BlockSpec(memory_space=pl.ANY),
            pl.BlockSpec(memory_space=pl.ANY),
            pl.BlockSpec((1, 1, 2 * D_MODEL), bmap),
            pl.BlockSpec((1, 1, D_MODEL), bmap),
        ],
        out_specs=pl.BlockSpec((tm * ROW_TILES, LANES), ymap),
        scratch_shapes=[
            pltpu.VMEM((GATHER_DEPTH, tm * ROW_TILES, LANES), F32),
            pltpu.VMEM((2, D_MODEL, 2 * D_MODEL), F32),
            pltpu.VMEM((2, D_MODEL, D_MODEL), F32),
            pltpu.VMEM((D_MODEL, 2 * D_MODEL), BF16),
            pltpu.VMEM((D_MODEL, D_MODEL), BF16),
            pltpu.SemaphoreType.DMA((GATHER_DEPTH,)),
            pltpu.SemaphoreType.DMA((2,)),
        ],
    )
    return pl.pallas_call(
        _expert_kernel,
        grid_spec=grid_spec,
        out_shape=jax.ShapeDtypeStruct((n_blk * tm * ROW_TILES, LANES), F32),
        compiler_params=pltpu.CompilerParams(
            dimension_semantics=("arbitrary",), vmem_limit_bytes=VMEM_LIMIT),
        name="expert_ffn",
    )(blk_expert, n_used, row_src, first, parity, next_expert, hf_rows, wgu, wd, bgu_split, bd)


def _combine_kernel(dest_ref, x1_ref, gate_ref, ys_ref, p_ref, gple_ref, wpg_ref, wpp_ref,
                    gfin_ref, o_ref, ybuf, sems):
    i = pl.program_id(0)
    n_steps = pl.num_programs(0)
    tm = OUT_TM
    slot = i % 2

    def start_row(tile, to_slot, r, k):
        src = pl.multiple_of(dest_ref[(tile * tm + r) * TOP_K + k], ROW_TILES)
        pltpu.make_async_copy(ys_ref.at[pl.ds(src, ROW_TILES)], _row(ybuf.at[to_slot], k * tm + r),
                              sems.at[to_slot]).start()

    def wait_tile(at_slot):
        pltpu.make_async_copy(ys_ref.at[pl.ds(0, TOP_K * tm * ROW_TILES)], ybuf.at[at_slot],
                              sems.at[at_slot]).wait()

    @pl.when(i == 0)
    def _():
        def tok(r, carry):
            for k in range(TOP_K):
                start_row(0, 0, r, k)
            return carry
        lax.fori_loop(0, tm, tok, 0)

    nxt = jnp.minimum(i + 1, n_steps - 1)
    per_piece = tm // COMBINE_PIECES

    def fetch_piece(piece):
        for r in range(piece * per_piece, (piece + 1) * per_piece):
            for k in range(TOP_K):
                start_row(nxt, 1 - slot, r, k)

    wait_tile(slot)
    moe = None
    for k in range(TOP_K):
        yk = jnp.concatenate(
            [ybuf[slot, pl.ds(k * tm * ROW_TILES + s, tm, stride=ROW_TILES), :]
             for s in range(ROW_TILES)], axis=1)
        term = gate_ref[:, k:k + 1] * yk
        moe = term if moe is None else moe + term
        fetch_piece(k)
    x2 = x1_ref[...] + moe
    hp = _rms(x2, gple_ref[...]).astype(BF16)
    fetch_piece(4)
    pg = _sigmoid(jnp.dot(hp, wpg_ref[...], preferred_element_type=F32))
    fetch_piece(5)
    proj = jnp.dot(p_ref[...].astype(BF16), wpp_ref[...], preferred_element_type=F32)
    fetch_piece(6)
    x3 = x2 + pg * proj
    o_ref[...] = _rms(x3, gfin_ref[...])
    fetch_piece(7)

    @pl.when(i == n_steps - 1)
    def _():
        wait_tile(1 - slot)


def _combine(dest_flat, x1, gates_tk, ys, p2, gple, wpg, wpp, gfin):
    t = x1.shape[0]
    tm = OUT_TM
    ple = p2.shape[1]
    row = lambda i, d: (i, 0)
    fixed = lambda i, d: (0, 0)
    grid_spec = pltpu.PrefetchScalarGridSpec(
        num_scalar_prefetch=1,
        grid=(t // tm,),
        in_specs=[
            pl.BlockSpec((tm, D_MODEL), row),
            pl.BlockSpec((tm, TOP_K), row),
            pl.BlockSpec(memory_space=pl.ANY),
            pl.BlockSpec((tm, ple), row),
            pl.BlockSpec((1, D_MODEL), fixed),
            pl.BlockSpec((D_MODEL, D_MODEL), fixed),
            pl.BlockSpec((ple, D_MODEL), fixed),
            pl.BlockSpec((1, D_MODEL), fixed),
        ],
        out_specs=pl.BlockSpec((tm, D_MODEL), row),
        scratch_shapes=[
            pltpu.VMEM((2, TOP_K * tm * ROW_TILES, LANES), F32),
            pltpu.SemaphoreType.DMA((2,)),
        ],
    )
    return pl.pallas_call(
        _combine_kernel,
        grid_spec=grid_spec,
        out_shape=jax.ShapeDtypeStruct((t, D_MODEL), F32),
        compiler_params=pltpu.CompilerParams(
            dimension_semantics=("arbitrary",), vmem_limit_bytes=VMEM_LIMIT),
        name="combine_ple_final",
    )(dest_flat, x1, gates_tk, ys, p2, gple, wpg, wpp, gfin)


def kernel(x, p, rel_bias, norm_mix_g, w_in, pool_w, pool_scale, w_o_attn, w_o_pool, w_out,
           norm_ffn_g, router_w, router_b, w_gate_up, b_gate_up, w_down, b_down,
           norm_ple_g, w_ple_gate, w_ple_proj, norm_final_g):
    b, s, d = x.shape
    depth = w_in.shape[0]
    t = b * s
    assert d == D_MODEL and s % MOBA_BLOCK == 0 and t % IN_TM == 0 and s % MIX_TM == 0
    assert depth == 1, "the final norm is fused into the layer's last kernel"
    n_pad = t * TOP_K + (N_EXPERTS + GATHER_DEPTH - 2) * EXPERT_TM
    n_blk = n_pad // EXPERT_TM
    bown, bprev, bfar = _t5_bias_tables(rel_bias)

    x2 = x.reshape(t, d)
    for i in range(depth):
        q, k, v, u, gl = _in_proj(x2, norm_mix_g[i].reshape(1, d), w_in[i].astype(BF16))

        def heads(a):
            return a.reshape(b, s, N_HEADS, HEAD_DIM).transpose(0, 2, 1, 3).reshape(
                b * N_HEADS, s, HEAD_DIM)

        vth = v.reshape(b, s, N_HEADS, HEAD_DIM).transpose(0, 2, 3, 1).reshape(
            b * N_HEADS, HEAD_DIM, s)
        attn_t = _attention(heads(q), heads(k), vth, bown, bprev, bfar)
        attn = attn_t.reshape(b, N_HEADS, HEAD_DIM, s).transpose(0, 3, 1, 2).reshape(
            t, ATTN_WIDTH)

        x1, hf_rows, idx_kt, rank_kt, gate_kt, cnt = _mix(
            x2, attn, u, gl, pool_w[i].astype(BF16), pool_scale[i].reshape(1, POOL_WIDTH),
            w_o_attn[i].astype(BF16), w_o_pool[i].astype(BF16), w_out[i].astype(BF16),
            norm_ffn_g[i].reshape(1, d), router_w[i].T, router_b[i].reshape(N_EXPERTS, 1), s)

        counts = cnt[:, 0]
        padded = (counts + EXPERT_TM - 1) // EXPERT_TM * EXPERT_TM
        pend = jnp.cumsum(padded)
        pstart = pend - padded
        e_ids = jnp.arange(N_EXPERTS, dtype=I32)
        pstart_of = jnp.sum(jnp.where(idx_kt[..., None] == e_ids, pstart, 0), axis=-1)
        dest_flat = (pstart_of + rank_kt).T.reshape(-1).astype(I32)
        blk_row0 = jnp.arange(n_blk, dtype=I32) * EXPERT_TM
        blk_expert = jnp.minimum(jnp.sum(pend[None, :] <= blk_row0[:, None], axis=1),
                                 N_EXPERTS - 1).astype(I32)
        n_used = (pend[-1:] // EXPERT_TM).astype(I32)
        row_tok = jnp.zeros((n_pad,), I32).at[dest_flat].set(
            jnp.arange(t * TOP_K, dtype=I32) // TOP_K, unique_indices=True)

        bgu = b_gate_up[i].reshape(N_EXPERTS, D_MODEL // LANES, LANES, 2).transpose(
            0, 1, 3, 2).reshape(N_EXPERTS, 1, 2 * D_MODEL)
        first = jnp.concatenate([jnp.ones((1,), I32),
                                 (blk_expert[1:] != blk_expert[:-1]).astype(I32)])
        parity = ((jnp.cumsum(first) - 1) % 2).astype(I32)
        later = jnp.logical_and(e_ids[None, :] > e_ids[:, None], (padded > 0)[None, :])
        next_e = jnp.min(jnp.where(later, e_ids[None, :], N_EXPERTS), axis=1)
        next_e = jnp.where(next_e == N_EXPERTS, -1, next_e).astype(I32)
        next_expert = jnp.sum(jnp.where(blk_expert[:, None] == e_ids, next_e, 0),
                              axis=1).astype(I32)
        ys = _experts(blk_expert, n_used, row_tok * ROW_TILES, first, parity, next_expert,
                      hf_rows, w_gate_up[i], w_down[i], bgu, b_down[i][:, None, :])

        x2 = _combine(dest_flat * ROW_TILES, x1, gate_kt.T, ys, p[i].reshape(t, -1),
                      norm_ple_g[i].reshape(1, d), w_ple_gate[i].astype(BF16),
                      w_ple_proj[i].astype(BF16), norm_final_g.reshape(1, d))
    return x2.reshape(b, s, d)
```

```python
import functools
import math

import jax
import jax.numpy as jnp
from jax import lax
from jax.experimental import pallas as pl
from jax.experimental.pallas import tpu as pltpu

F32 = jnp.float32
BF16 = jnp.bfloat16
I32 = jnp.int32

D_MODEL = 1024
N_HEADS = 8
HEAD_DIM = 64
ATTN_WIDTH = N_HEADS * HEAD_DIM
MOBA_BLOCK = 256
MOBA_TOPK = 3
NUM_BUCKETS = 32
MAX_DISTANCE = 128
POOL_WINDOWS = (2, 4, 8, 16)
POOL_GROUPS = 4
POOL_GROUP_DIM = 128
POOL_WIDTH = POOL_GROUPS * POOL_GROUP_DIM
N_EXPERTS = 32
TOP_K = 4
SWIGLU_LIMIT = 7.0
SWIGLU_ALPHA = 1.702
RMS_EPS = 1e-6

LANES = 128
SUBLANES = 8
ROW_TILES = D_MODEL // LANES

IN_TM = 512
MIX_TM = 256
EXPERT_TM = 256
OUT_TM = 256
ATTN_GROUP = 4
ATTN_HEADS = 4
GATHER_CHUNK = 32
WEIGHT_DMA_CHUNKS = 4
COMBINE_PIECES = 8
GATHER_DEPTH = 4
POOL_HALO = 16
MASKED = -1e30
LOG2E = math.log2(math.e)
Q_SCALE = HEAD_DIM ** -0.5 * LOG2E

VMEM_LIMIT = 56 * 1024 * 1024

_NT = (((1,), (1,)), ((), ()))


def _rms(x, g):
    ms = jnp.mean(x * x, axis=-1, keepdims=True)
    return x * lax.rsqrt(ms + RMS_EPS) * g


def _sigmoid(x):
    return 1.0 / (1.0 + jnp.exp(-x))


def _in_proj_kernel(x_ref, g_ref, w_ref, wvt_ref, q_ref, k_ref, vt_ref, u_ref, gl_ref):
    h = _rms(x_ref[...], g_ref[...]).astype(BF16)
    aw = ATTN_WIDTH

    def proj(lo, hi):
        return jnp.dot(h, w_ref[:, lo:hi], preferred_element_type=F32)

    qv = (proj(0, aw) * Q_SCALE).astype(BF16)
    kv = proj(aw, 2 * aw).astype(BF16)
    for hd in range(N_HEADS):
        q_ref[0, hd] = qv[:, hd * HEAD_DIM:(hd + 1) * HEAD_DIM]
        k_ref[0, hd] = kv[:, hd * HEAD_DIM:(hd + 1) * HEAD_DIM]
    vt = lax.dot_general(wvt_ref[...], h, _NT, preferred_element_type=F32).astype(BF16)
    vt_ref[0] = vt.reshape(N_HEADS, HEAD_DIM, vt.shape[1])
    u_ref[...] = proj(3 * aw, 3 * aw + POOL_WIDTH)
    base = 3 * aw + POOL_WIDTH
    for c in range(2):
        lo = base + c * D_MODEL
        gl_ref[:, c * D_MODEL:(c + 1) * D_MODEL] = proj(lo, lo + D_MODEL).astype(BF16)


def _in_proj(x2, g, w_bf16, wvt_bf16, batch, seq):
    t = x2.shape[0]
    in_cols = w_bf16.shape[1]
    tps = seq // IN_TM
    row = lambda i: (i, 0)
    fixed = lambda i: (0, 0)
    head_rows = lambda i: (i // tps, 0, i % tps, 0)
    head_cols = lambda i: (i // tps, 0, 0, i % tps)
    return pl.pallas_call(
        _in_proj_kernel,
        grid=(t // IN_TM,),
        in_specs=[
            pl.BlockSpec((IN_TM, D_MODEL), row),
            pl.BlockSpec((1, D_MODEL), fixed),
            pl.BlockSpec((D_MODEL, in_cols), fixed),
            pl.BlockSpec((ATTN_WIDTH, D_MODEL), fixed),
        ],
        out_specs=[
            pl.BlockSpec((1, N_HEADS, IN_TM, HEAD_DIM), head_rows),
            pl.BlockSpec((1, N_HEADS, IN_TM, HEAD_DIM), head_rows),
            pl.BlockSpec((1, N_HEADS, HEAD_DIM, IN_TM), head_cols),
            pl.BlockSpec((IN_TM, POOL_WIDTH), row),
            pl.BlockSpec((IN_TM, 2 * D_MODEL), row),
        ],
        out_shape=[
            jax.ShapeDtypeStruct((batch, N_HEADS, seq, HEAD_DIM), BF16),
            jax.ShapeDtypeStruct((batch, N_HEADS, seq, HEAD_DIM), BF16),
            jax.ShapeDtypeStruct((batch, N_HEADS, HEAD_DIM, seq), BF16),
            jax.ShapeDtypeStruct((t, POOL_WIDTH), F32),
            jax.ShapeDtypeStruct((t, 2 * D_MODEL), BF16),
        ],
        compiler_params=pltpu.CompilerParams(
            dimension_semantics=("parallel",), vmem_limit_bytes=VMEM_LIMIT),
        name="in_proj",
    )(x2, g, w_bf16, wvt_bf16)


def _attn_kernel(bfar_ref, q_ref, k_ref, vt_ref, bown_ref, bprev_ref, o_ref,
                 kmean_s, am_s, sa_s, sb_s, ga_s, gb_s, m_s, l_s, acc_s, *, n_blocks):
    hp = pl.program_id(0)
    j = pl.program_id(1)
    blk = MOBA_BLOCK
    grp = ATTN_GROUP
    gk = grp * blk
    n_groups = n_blocks // grp
    n_far = jnp.maximum(j - 1, 0)
    ng = (n_far + grp - 1) // grp
    jp = jnp.maximum(j - 1, 0)
    heads = range(ATTN_HEADS)

    @pl.when(j == 0)
    def _():
        for hh in heads:
            for n in range(n_blocks):
                kb = k_ref[hh, n * blk:(n + 1) * blk, :].astype(F32)
                kmean_s[hh, n:n + 1, :] = jnp.mean(kb, axis=0, keepdims=True)

    qs = [q_ref[hh] for hh in heads]

    def select(hh):
        gate = lax.dot_general(kmean_s[hh], qs[hh].astype(F32), _NT,
                               precision=lax.Precision.HIGHEST, preferred_element_type=F32)
        n_iota = lax.broadcasted_iota(I32, gate.shape, 0)
        past = n_iota < j
        gate = jnp.where(past, gate, jnp.finfo(F32).min)
        sel = jnp.zeros(gate.shape, jnp.bool_)
        for _ in range(MOBA_TOPK):
            mx = jnp.max(gate, axis=0, keepdims=True)
            idx = jnp.min(jnp.where(gate == mx, n_iota, n_blocks), axis=0, keepdims=True)
            pick = n_iota == idx
            sel = jnp.logical_or(sel, pick)
            gate = jnp.where(pick, -jnp.inf, gate)
        sel = jnp.logical_and(sel, past)
        bfar = bfar_ref[(hp * ATTN_HEADS + hh) % N_HEADS]
        am_s[hh] = jnp.where(jnp.logical_and(sel, n_iota < j - 1), bfar, MASKED)
        return jnp.max(jnp.where(jnp.logical_and(sel, n_iota == j - 1), 0.0, MASKED),
                       axis=0, keepdims=True)

    def produce(hh, g, s_ref, gmax_ref):
        g = jnp.minimum(g, n_groups - 1)
        kb = k_ref[hh, pl.ds(pl.multiple_of(g * gk, gk), gk), :]
        s = lax.dot_general(kb, qs[hh], _NT, preferred_element_type=F32)
        s_ref[hh] = s
        gmax = None
        for b in range(grp):
            mb = (jnp.max(s[b * blk:(b + 1) * blk], axis=0, keepdims=True)
                  + am_s[hh, pl.ds(g * grp + b, 1), :])
            gmax = mb if gmax is None else jnp.maximum(gmax, mb)
        gmax_ref[hh] = gmax

    def consume(hh, g, s_ref, gmax_ref):
        m_old = m_s[hh]
        m_new = jnp.maximum(m_old, gmax_ref[hh])
        alpha = jnp.exp2(m_old - m_new)
        parts, lsum = [], None
        for b in range(grp):
            shift = m_new - am_s[hh, pl.ds(g * grp + b, 1), :]
            pb = jnp.exp2(s_ref[hh, b * blk:(b + 1) * blk, :] - shift)
            sb = jnp.sum(pb, axis=0, keepdims=True)
            lsum = sb if lsum is None else lsum + sb
            parts.append(pb.astype(BF16))
        p = jnp.concatenate(parts, axis=0)
        vt = vt_ref[hh, :, pl.ds(pl.multiple_of(g * gk, gk), gk)]
        pv = jnp.dot(vt, p, preferred_element_type=F32)
        m_s[hh] = m_new
        l_s[hh] = alpha * l_s[hh] + lsum
        acc_s[hh] = alpha * acc_s[hh] + pv

    def near_scores(hh, n, bias):
        kb = k_ref[hh, pl.ds(pl.multiple_of(n * blk, blk), blk), :]
        return lax.dot_general(kb, qs[hh], _NT, preferred_element_type=F32) + bias

    def near_pv(hh, n, p):
        vt = vt_ref[hh, :, pl.ds(pl.multiple_of(n * blk, blk), blk)]
        return jnp.dot(vt, p.astype(BF16), preferred_element_type=F32)

    for hh in heads:
        am_prev = select(hh)
        produce(hh, 0, sa_s, ga_s)
        s_own = near_scores(hh, j, bown_ref[hh])
        s_prev = near_scores(hh, jp, bprev_ref[hh])
        m = jnp.maximum(jnp.max(s_own, axis=0, keepdims=True),
                        jnp.max(s_prev, axis=0, keepdims=True) + am_prev)
        p_own = jnp.exp2(s_own - m)
        p_prev = jnp.exp2(s_prev - (m - am_prev))
        m_s[hh] = m
        l_s[hh] = jnp.sum(p_own, axis=0, keepdims=True) + jnp.sum(p_prev, axis=0, keepdims=True)
        acc_s[hh] = near_pv(hh, j, p_own) + near_pv(hh, jp, p_prev)

    def pair(h, carry):
        for hh in heads:
            consume(hh, 2 * h, sa_s, ga_s)
            produce(hh, 2 * h + 1, sb_s, gb_s)
        for hh in heads:
            consume(hh, 2 * h + 1, sb_s, gb_s)
            produce(hh, 2 * h + 2, sa_s, ga_s)
        return carry

    lax.fori_loop(0, ng // 2, pair, 0)

    @pl.when(ng % 2 == 1)
    def _():
        for hh in heads:
            consume(hh, ng - 1, sa_s, ga_s)

    for hh in heads:
        o_ref[hh] = (acc_s[hh] / l_s[hh]).astype(o_ref.dtype)


def _attention(qh, kh, vth, bown, bprev, bfar):
    bhn, s, dh = qh.shape
    nb = s // MOBA_BLOCK
    ah = ATTN_HEADS
    assert N_HEADS % ah == 0 and nb % ATTN_GROUP == 0
    kern = functools.partial(_attn_kernel, n_blocks=nb)
    head_blk = lambda b, j: (b % (N_HEADS // ah), 0, 0)
    grid_spec = pltpu.PrefetchScalarGridSpec(
        num_scalar_prefetch=0,
        grid=(bhn // ah, nb),
        in_specs=[
            pl.BlockSpec(memory_space=pltpu.SMEM),
            pl.BlockSpec((ah, MOBA_BLOCK, dh), lambda b, j: (b, j, 0)),
            pl.BlockSpec((ah, s, dh), lambda b, j: (b, 0, 0)),
            pl.BlockSpec((ah, dh, s), lambda b, j: (b, 0, 0)),
            pl.BlockSpec((ah, MOBA_BLOCK, MOBA_BLOCK), head_blk),
            pl.BlockSpec((ah, MOBA_BLOCK, MOBA_BLOCK), head_blk),
        ],
        out_specs=pl.BlockSpec((ah, dh, MOBA_BLOCK), lambda b, j: (b, 0, j)),
        scratch_shapes=[
            pltpu.VMEM((ah, nb, dh), F32),
            pltpu.VMEM((ah, nb, MOBA_BLOCK), F32),
            pltpu.VMEM((ah, ATTN_GROUP * MOBA_BLOCK, MOBA_BLOCK), F32),
            pltpu.VMEM((ah, ATTN_GROUP * MOBA_BLOCK, MOBA_BLOCK), F32),
            pltpu.VMEM((ah, 1, MOBA_BLOCK), F32),
            pltpu.VMEM((ah, 1, MOBA_BLOCK), F32),
            pltpu.VMEM((ah, 1, MOBA_BLOCK), F32),
            pltpu.VMEM((ah, 1, MOBA_BLOCK), F32),
            pltpu.VMEM((ah, dh, MOBA_BLOCK), F32),
        ],
    )
    return pl.pallas_call(
        kern,
        grid_spec=grid_spec,
        out_shape=jax.ShapeDtypeStruct((bhn, dh, s), BF16),
        compiler_params=pltpu.CompilerParams(
            dimension_semantics=("parallel", "arbitrary"), vmem_limit_bytes=VMEM_LIMIT),
        name="moba_attention",
    )(bfar, qh, kh, vth, bown, bprev)


def _t5_bias_tables(rel_bias):
    blk = MOBA_BLOCK
    n = jnp.arange(2 * blk)
    max_exact = NUM_BUCKETS // 2
    nf = jnp.maximum(n, 1).astype(F32)
    large = max_exact + (jnp.log(nf / max_exact) / math.log(MAX_DISTANCE / max_exact)
                         * (NUM_BUCKETS - max_exact)).astype(I32)
    large = jnp.minimum(large, NUM_BUCKETS - 1)
    bucket = jnp.where(n < max_exact, n, large)
    rel_bias = rel_bias.astype(F32) * LOG2E
    tbl = rel_bias[bucket].T
    key = jnp.arange(blk)[:, None]
    qry = jnp.arange(blk)[None, :]
    d = qry - key
    h = tbl.shape[0]
    wide = 2 * blk + 1
    skew = jnp.broadcast_to(jnp.pad(tbl, ((0, 0), (0, 1)))[:, None, :], (h, blk, wide))
    skew = skew.reshape(h, blk * wide)[:, :blk * 2 * blk].reshape(h, blk, 2 * blk)
    bown = jnp.where(d >= 0, skew[:, :, :blk], MASKED)
    bprev = skew[:, :, blk:]
    bfar = rel_bias[NUM_BUCKETS - 1]
    return bown, bprev, bfar


def _mix_kernel(x_ref, attn_ref, u_ref, halo_ref, gl_ref, pw_ref, ps_ref, woa_ref, wop_ref,
                wout_ref, gffn_ref, rwt_ref, rb_ref,
                x1_ref, hf_ref, idx_ref, rank_ref, gate_ref, cnt_ref,
                ext_s, carry_s, *, tiles_per_seq):
    i = pl.program_id(0)
    tm = MIX_TM

    @pl.when(i == 0)
    def _():
        carry_s[...] = jnp.zeros_like(carry_s)

    first = (i % tiles_per_seq) == 0
    ext_s[0:POOL_HALO, :] = jnp.where(first, 0.0, halo_ref[...])
    ext_s[POOL_HALO:POOL_HALO + tm, :] = u_ref[...]
    pos = (i % tiles_per_seq) * tm + lax.broadcasted_iota(I32, (tm, 1), 0)
    pooled_parts = []
    for g, w in enumerate(POOL_WINDOWS):
        c0, c1 = g * POOL_GROUP_DIM, (g + 1) * POOL_GROUP_DIM
        win = ext_s[POOL_HALO:POOL_HALO + tm, c0:c1]
        for sft in range(1, w):
            win = win + ext_s[POOL_HALO - sft:POOL_HALO - sft + tm, c0:c1]
        cnt = jnp.minimum(pos + 1, w).astype(F32)
        pin = win / cnt - ext_s[POOL_HALO:POOL_HALO + tm, c0:c1]
        pooled_parts.append(jnp.dot(pin.astype(BF16), pw_ref[g], preferred_element_type=F32))
    pooled = jnp.concatenate(pooled_parts, axis=1) * ps_ref[...]

    a = lax.dot_general(attn_ref[0], woa_ref[...], (((0,), (0,)), ((), ())),
                        preferred_element_type=F32)
    pm = jnp.dot(pooled.astype(BF16), wop_ref[...], preferred_element_type=F32)
    g0 = _sigmoid(gl_ref[:, 0:D_MODEL].astype(F32))
    g1 = _sigmoid(gl_ref[:, D_MODEL:2 * D_MODEL].astype(F32))
    merged = g0 * a + g1 * pm
    x1 = x_ref[...] + jnp.dot(merged.astype(BF16), wout_ref[...], preferred_element_type=F32)
    x1_ref[...] = x1

    hf = _rms(x1, gffn_ref[...])
    hfb = hf.astype(BF16)
    for s in range(ROW_TILES):
        hf_ref[pl.ds(s, tm, stride=ROW_TILES), :] = hfb[:, s * LANES:(s + 1) * LANES].astype(F32)

    logits = lax.dot_general(rwt_ref[...], hfb.astype(F32), _NT,
                             precision=lax.Precision.HIGHEST,
                             preferred_element_type=F32) + rb_ref[...]
    e_iota = lax.broadcasted_iota(I32, logits.shape, 0)
    vals, picks = [], []
    for _ in range(TOP_K):
        mx = jnp.max(logits, axis=0, keepdims=True)
        idx = jnp.min(jnp.where(logits == mx, e_iota, N_EXPERTS), axis=0, keepdims=True)
        pick = e_iota == idx
        vals.append(mx)
        picks.append(pick)
        idx_ref[len(picks) - 1:len(picks), :] = idx
        logits = jnp.where(pick, -jnp.inf, logits)
    ex = [jnp.exp(v - vals[0]) for v in vals]
    den = ex[0] + ex[1] + ex[2] + ex[3]
    for k in range(TOP_K):
        gate_ref[k:k + 1, :] = ex[k] / den

    onehot = jnp.zeros(e_iota.shape, F32)
    for pick in picks:
        onehot = onehot + pick.astype(F32)
    ra = lax.broadcasted_iota(I32, (tm, tm), 0)
    rb = lax.broadcasted_iota(I32, (tm, tm), 1)
    upper = (ra < rb).astype(BF16)
    before = jnp.dot(onehot.astype(BF16), upper, preferred_element_type=F32) + carry_s[...]
    for k, pick in enumerate(picks):
        rank_ref[k:k + 1, :] = jnp.sum(jnp.where(pick, before, 0.0), axis=0,
                                       keepdims=True).astype(I32)
    carry_s[...] = carry_s[...] + jnp.sum(onehot, axis=1, keepdims=True)
    cnt_ref[...] = jnp.broadcast_to(carry_s[...], cnt_ref.shape).astype(I32)


def _mix(x2, attn, u, gl, pw, ps, woa, wop, wout, gffn, rwt, rb, seq):
    t = x2.shape[0]
    tm = MIX_TM
    tiles_per_seq = seq // tm
    halo_per_tile = tm // POOL_HALO
    row = lambda i: (i, 0)
    fixed2 = lambda i: (0, 0)
    fixed3 = lambda i: (0, 0, 0)
    col = lambda i: (0, i)
    kern = functools.partial(_mix_kernel, tiles_per_seq=tiles_per_seq)
    return pl.pallas_call(
        kern,
        grid=(t // tm,),
        in_specs=[
            pl.BlockSpec((tm, D_MODEL), row),
            pl.BlockSpec((1, ATTN_WIDTH, tm), lambda i: (i // tiles_per_seq, 0, i % tiles_per_seq)),
            pl.BlockSpec((tm, POOL_WIDTH), row),
            pl.BlockSpec((POOL_HALO, POOL_WIDTH),
                         lambda i: (jnp.maximum(i * halo_per_tile - 1, 0), 0)),
            pl.BlockSpec((tm, 2 * D_MODEL), row),
            pl.BlockSpec((POOL_GROUPS, POOL_GROUP_DIM, POOL_GROUP_DIM), fixed3),
            pl.BlockSpec((1, POOL_WIDTH), fixed2),
            pl.BlockSpec((ATTN_WIDTH, D_MODEL), fixed2),
            pl.BlockSpec((POOL_WIDTH, D_MODEL), fixed2),
            pl.BlockSpec((D_MODEL, D_MODEL), fixed2),
            pl.BlockSpec((1, D_MODEL), fixed2),
            pl.BlockSpec((N_EXPERTS, D_MODEL), fixed2),
            pl.BlockSpec((N_EXPERTS, 1), fixed2),
        ],
        out_specs=[
            pl.BlockSpec((tm, D_MODEL), row),
            pl.BlockSpec((tm * ROW_TILES, LANES), row),
            pl.BlockSpec((TOP_K, tm), col),
            pl.BlockSpec((TOP_K, tm), col),
            pl.BlockSpec((TOP_K, tm), col),
            pl.BlockSpec((N_EXPERTS, LANES), fixed2),
        ],
        out_shape=[
            jax.ShapeDtypeStruct((t, D_MODEL), F32),
            jax.ShapeDtypeStruct((t * ROW_TILES, LANES), F32),
            jax.ShapeDtypeStruct((TOP_K, t), I32),
            jax.ShapeDtypeStruct((TOP_K, t), I32),
            jax.ShapeDtypeStruct((TOP_K, t), F32),
            jax.ShapeDtypeStruct((N_EXPERTS, LANES), I32),
        ],
        scratch_shapes=[
            pltpu.VMEM((POOL_HALO + tm, POOL_WIDTH), F32),
            pltpu.VMEM((N_EXPERTS, 1), F32),
        ],
        compiler_params=pltpu.CompilerParams(
            dimension_semantics=("arbitrary",), vmem_limit_bytes=VMEM_LIMIT),
        name="mix_router",
    )(x2, attn, u, u, gl, pw, ps, woa, wop, wout, gffn, rwt, rb)


def _row(ref, r):
    return ref.at[pl.ds(pl.multiple_of(r * ROW_TILES, ROW_TILES), ROW_TILES)]


def _pair_split_matrix():
    n = 2 * LANES
    r = lax.broadcasted_iota(I32, (n, n), 0)
    c = lax.broadcasted_iota(I32, (n, n), 1)
    src = jnp.where(c < LANES, 2 * c, 2 * (c - LANES) + 1)
    return (r == src).astype(BF16)


def _expert_kernel(be_ref, nused_ref, rowsrc_ref, first_ref, par_ref, next_ref,
                   hf_ref, wgu_hbm, wd_hbm, bgu_ref, bd_ref,
                   y_ref, xbuf, wgu_f, wd_f, wgu_s, wd_s, sems, wsems):
    i = pl.program_id(0)
    tm = EXPERT_TM
    n_used = nused_ref[0]
    slot = i % GATHER_DEPTH
    ahead = GATHER_DEPTH - 1

    def start_rows(blk, to_slot, r):
        src = pl.multiple_of(rowsrc_ref[blk * tm + r], ROW_TILES)
        pltpu.make_async_copy(hf_ref.at[pl.ds(src, ROW_TILES)], _row(xbuf.at[to_slot], r),
                              sems.at[to_slot]).start()

    def wait_rows(at_slot):
        pltpu.make_async_copy(hf_ref.at[pl.ds(0, tm * ROW_TILES)], xbuf.at[at_slot],
                              sems.at[at_slot]).wait()

    def weight_copies(e, w_slot, start):
        rows = D_MODEL // WEIGHT_DMA_CHUNKS
        for c in range(WEIGHT_DMA_CHUNKS):
            sl = pl.ds(c * rows, rows)
            for src, dst in ((wgu_hbm, wgu_f), (wd_hbm, wd_f)):
                if start:
                    pltpu.async_copy(src.at[e, sl], dst.at[w_slot, sl], wsems.at[w_slot],
                                     priority=1)
                else:
                    pltpu.make_async_copy(src.at[e, sl], dst.at[w_slot, sl],
                                          wsems.at[w_slot]).wait()

    @pl.when(i == 0)
    def _():
        weight_copies(be_ref[0], 0, True)

        def row(r, carry):
            for blk in range(ahead):
                start_rows(blk, blk, r)
            return carry
        lax.fori_loop(0, tm, row, 0)

    @pl.when(jnp.logical_and(first_ref[i] == 1, i < n_used))
    def _():
        par = par_ref[i]
        weight_copies(be_ref[i], par, False)

        @pl.when(next_ref[i] >= 0)
        def _():
            weight_copies(next_ref[i], 1 - par, True)

        split = _pair_split_matrix()
        for c in range(2 * D_MODEL // (2 * LANES)):
            lo, hi = c * 2 * LANES, (c + 1) * 2 * LANES
            wgu_s[:, lo:hi] = jnp.dot(wgu_f[par, :, lo:hi].astype(BF16), split,
                                      preferred_element_type=F32).astype(BF16)
        wd_s[...] = wd_f[par].astype(BF16)

    @pl.when(i >= n_used)
    def _():
        y_ref[...] = jnp.zeros_like(y_ref)

    @pl.when(jnp.logical_and(i >= n_used, i < n_used + ahead))
    def _():
        wait_rows(slot)

    @pl.when(i < n_used)
    def _():
        wait_rows(slot)
        xb = jnp.concatenate(
            [xbuf[slot, pl.ds(s, tm, stride=ROW_TILES), :] for s in range(ROW_TILES)],
            axis=1).astype(BF16)
        hid_parts = []
        n_chunks = D_MODEL // LANES
        assert n_chunks * GATHER_CHUNK == tm
        for c in range(n_chunks):
            lo, hi = c * 2 * LANES, (c + 1) * 2 * LANES
            gu = jnp.dot(xb, wgu_s[:, lo:hi], preferred_element_type=F32) + bgu_ref[0, :, lo:hi]
            glu = jnp.minimum(gu[:, :LANES], SWIGLU_LIMIT)
            lin = jnp.clip(gu[:, LANES:], -SWIGLU_LIMIT, SWIGLU_LIMIT)
            hid_parts.append(((lin + 1.0) * (glu * _sigmoid(SWIGLU_ALPHA * glu))).astype(BF16))
            for r in range(c * GATHER_CHUNK, (c + 1) * GATHER_CHUNK):
                start_rows(i + ahead, (i + ahead) % GATHER_DEPTH, r)
        hid = jnp.concatenate(hid_parts, axis=1)
        y = jnp.dot(hid, wd_s[...], preferred_element_type=F32) + bd_ref[0]
        for s in range(ROW_TILES):
            y_ref[pl.ds(s, tm, stride=ROW_TILES), :] = y[:, s * LANES:(s + 1) * LANES]


def _experts(blk_expert, n_used, row_src, first, parity, next_expert, hf_rows, wgu, wd,
             bgu_split, bd):
    n_blk = blk_expert.shape[0]
    tm = EXPERT_TM

    def ymap(i, be, nu, *_):
        return (i, 0)

    def bmap(i, be, nu, *_):
        return (be[jnp.minimum(i, nu[0] - 1)], 0, 0)

    grid_spec = pltpu.PrefetchScalarGridSpec(
        num_scalar_prefetch=6,
        grid=(n_blk,),
        in_specs=[
            pl.BlockSpec(memory_space=pl.ANY),
            pl.BlockSpec(memory_space=pl.ANY),
            pl.BlockSpec(memory_space=pl.ANY),
            pl.BlockSpec((1, 1, 2 * D_MODEL), bmap),
            pl.BlockSpec((1, 1, D_MODEL), bmap),
        ],
        out_specs=pl.BlockSpec((tm * ROW_TILES, LANES), ymap),
        scratch_shapes=[
            pltpu.VMEM((GATHER_DEPTH, tm * ROW_TILES, LANES), F32),
            pltpu.VMEM((2, D_MODEL, 2 * D_MODEL), F32),
            pltpu.VMEM((2, D_MODEL, D_MODEL), F32),
            pltpu.VMEM((D_MODEL, 2 * D_MODEL), BF16),
            pltpu.VMEM((D_MODEL, D_MODEL), BF16),
            pltpu.SemaphoreType.DMA((GATHER_DEPTH,)),
            pltpu.SemaphoreType.DMA((2,)),
        ],
    )
    return pl.pallas_call(
        _expert_kernel,
        grid_spec=grid_spec,
        out_shape=jax.ShapeDtypeStruct((n_blk * tm * ROW_TILES, LANES), F32),
        compiler_params=pltpu.CompilerParams(
            dimension_semantics=("arbitrary",), vmem_limit_bytes=VMEM_LIMIT),
        name="expert_ffn",
    )(blk_expert, n_used, row_src, first, parity, next_expert, hf_rows, wgu, wd, bgu_split, bd)


def _combine_kernel(dest_ref, x1_ref, gate_ref, ys_ref, p_ref, gple_ref, wpg_ref, wpp_ref,
                    gfin_ref, o_ref, ybuf, sems):
    i = pl.program_id(0)
    n_steps = pl.num_programs(0)
    tm = OUT_TM
    slot = i % 2

    def start_row(tile, to_slot, r, k):
        src = pl.multiple_of(dest_ref[(tile * tm + r) * TOP_K + k], ROW_TILES)
        pltpu.make_async_copy(ys_ref.at[pl.ds(src, ROW_TILES)], _row(ybuf.at[to_slot], k * tm + r),
                              sems.at[to_slot]).start()

    def wait_tile(at_slot):
        pltpu.make_async_copy(ys_ref.at[pl.ds(0, TOP_K * tm * ROW_TILES)], ybuf.at[at_slot],
                              sems.at[at_slot]).wait()

    @pl.when(i == 0)
    def _():
        def tok(r, carry):
            for k in range(TOP_K):
                start_row(0, 0, r, k)
            return carry
        lax.fori_loop(0, tm, tok, 0)

    nxt = jnp.minimum(i + 1, n_steps - 1)
    per_piece = tm // COMBINE_PIECES

    def fetch_piece(piece):
        for r in range(piece * per_piece, (piece + 1) * per_piece):
            for k in range(TOP_K):
                start_row(nxt, 1 - slot, r, k)

    wait_tile(slot)
    moe = None
    for k in range(TOP_K):
        yk = jnp.concatenate(
            [ybuf[slot, pl.ds(k * tm * ROW_TILES + s, tm, stride=ROW_TILES), :]
             for s in range(ROW_TILES)], axis=1)
        term = gate_ref[:, k:k + 1] * yk
        moe = term if moe is None else moe + term
        fetch_piece(k)
    x2 = x1_ref[...] + moe
    hp = _rms(x2, gple_ref[...]).astype(BF16)
    fetch_piece(4)
    pg = _sigmoid(jnp.dot(hp, wpg_ref[...], preferred_element_type=F32))
    fetch_piece(5)
    proj = jnp.dot(p_ref[...].astype(BF16), wpp_ref[...], preferred_element_type=F32)
    fetch_piece(6)
    x3 = x2 + pg * proj
    o_ref[...] = _rms(x3, gfin_ref[...])
    fetch_piece(7)

    @pl.when(i == n_steps - 1)
    def _():
        wait_tile(1 - slot)


def _combine(dest_flat, x1, gates_tk, ys, p2, gple, wpg, wpp, gfin):
    t = x1.shape[0]
    tm = OUT_TM
    ple = p2.shape[1]
    row = lambda i, d: (i, 0)
    fixed = lambda i, d: (0, 0)
    grid_spec = pltpu.PrefetchScalarGridSpec(
        num_scalar_prefetch=1,
        grid=(t // tm,),
        in_specs=[
            pl.BlockSpec((tm, D_MODEL), row),
            pl.BlockSpec((tm, TOP_K), row),
            pl.BlockSpec(memory_space=pl.ANY),
            pl.BlockSpec((tm, ple), row),
            pl.BlockSpec((1, D_MODEL), fixed),
            pl.BlockSpec((D_MODEL, D_MODEL), fixed),
            pl.BlockSpec((ple, D_MODEL), fixed),
            pl.BlockSpec((1, D_MODEL), fixed),
        ],
        out_specs=pl.BlockSpec((tm, D_MODEL), row),
        scratch_shapes=[
            pltpu.VMEM((2, TOP_K * tm * ROW_TILES, LANES), F32),
            pltpu.SemaphoreType.DMA((2,)),
        ],
    )
    return pl.pallas_call(
        _combine_kernel,
        grid_spec=grid_spec,
        out_shape=jax.ShapeDtypeStruct((t, D_MODEL), F32),
        compiler_params=pltpu.CompilerParams(
            dimension_semantics=("arbitrary",), vmem_limit_bytes=VMEM_LIMIT),
        name="combine_ple_final",
    )(dest_flat, x1, gates_tk, ys, p2, gple, wpg, wpp, gfin)


def kernel(x, p, rel_bias, norm_mix_g, w_in, pool_w, pool_scale, w_o_attn, w_o_pool, w_out,
           norm_ffn_g, router_w, router_b, w_gate_up, b_gate_up, w_down, b_down,
           norm_ple_g, w_ple_gate, w_ple_proj, norm_final_g):
    b, s, d = x.shape
    depth = w_in.shape[0]
    t = b * s
    assert d == D_MODEL and s % MOBA_BLOCK == 0 and t % IN_TM == 0 and s % MIX_TM == 0
    assert depth == 1, "the final norm is fused into the layer's last kernel"
    n_pad = t * TOP_K + (N_EXPERTS + GATHER_DEPTH - 2) * EXPERT_TM
    n_blk = n_pad // EXPERT_TM
    bown, bprev, bfar = _t5_bias_tables(rel_bias)

    x2 = x.reshape(t, d)
    for i in range(depth):
        w_in_b = w_in[i].astype(BF16)
        wvt = w_in[i][:, 2 * ATTN_WIDTH:3 * ATTN_WIDTH].T.astype(BF16)
        q, k, vt, u, gl = _in_proj(x2, norm_mix_g[i].reshape(1, d), w_in_b, wvt, b, s)
        bh = b * N_HEADS
        attn_t = _attention(q.reshape(bh, s, HEAD_DIM), k.reshape(bh, s, HEAD_DIM),
                            vt.reshape(bh, HEAD_DIM, s), bown, bprev, bfar)
        attn = attn_t.reshape(b, ATTN_WIDTH, s)

        x1, hf_rows, idx_kt, rank_kt, gate_kt, cnt = _mix(
            x2, attn, u, gl, pool_w[i].astype(BF16), pool_scale[i].reshape(1, POOL_WIDTH),
            w_o_attn[i].astype(BF16), w_o_pool[i].astype(BF16), w_out[i].astype(BF16),
            norm_ffn_g[i].reshape(1, d), router_w[i].T, router_b[i].reshape(N_EXPERTS, 1), s)

        counts = cnt[:, 0]
        padded = (counts + EXPERT_TM - 1) // EXPERT_TM * EXPERT_TM
        pend = jnp.cumsum(padded)
        pstart = pend - padded
        e_ids = jnp.arange(N_EXPERTS, dtype=I32)
        pstart_of = jnp.sum(jnp.where(idx_kt[..., None] == e_ids, pstart, 0), axis=-1)
        dest_flat = (pstart_of + rank_kt).T.reshape(-1).astype(I32)
        blk_row0 = jnp.arange(n_blk, dtype=I32) * EXPERT_TM
        blk_expert = jnp.minimum(jnp.sum(pend[None, :] <= blk_row0[:, None], axis=1),
                                 N_EXPERTS - 1).astype(I32)
        n_used = (pend[-1:] // EXPERT_TM).astype(I32)
        row_tok = jnp.zeros((n_pad,), I32).at[dest_flat].set(
            jnp.arange(t * TOP_K, dtype=I32) // TOP_K, unique_indices=True)

        bgu = b_gate_up[i].reshape(N_EXPERTS, D_MODEL // LANES, LANES, 2).transpose(
            0, 1, 3, 2).reshape(N_EXPERTS, 1, 2 * D_MODEL)
        first = jnp.concatenate([jnp.ones((1,), I32),
                                 (blk_expert[1:] != blk_expert[:-1]).astype(I32)])
        parity = ((jnp.cumsum(first) - 1) % 2).astype(I32)
        later = jnp.logical_and(e_ids[None, :] > e_ids[:, None], (padded > 0)[None, :])
        next_e = jnp.min(jnp.where(later, e_ids[None, :], N_EXPERTS), axis=1)
        next_e = jnp.where(next_e == N_EXPERTS, -1, next_e).astype(I32)
        next_expert = jnp.sum(jnp.where(blk_expert[:, None] == e_ids, next_e, 0),
                              axis=1).astype(I32)
        ys = _experts(blk_expert, n_used, row_tok * ROW_TILES, first, parity, next_expert,
                      hf_rows, w_gate_up[i], w_down[i], bgu, b_down[i][:, None, :])

        x2 = _combine(dest_flat * ROW_TILES, x1, gate_kt.T, ys, p[i].reshape(t, -1),
                      norm_ple_g[i].reshape(1, d), w_ple_gate[i].astype(BF16),
                      w_ple_proj[i].astype(BF16), norm_final_g.reshape(1, d))
    return x2.reshape(b, s, d)
```

```python
import functools
import math

import jax
import jax.numpy as jnp
from jax import lax
from jax.experimental import pallas as pl
from jax.experimental.pallas import tpu as pltpu

F32 = jnp.float32
BF16 = jnp.bfloat16
I32 = jnp.int32

D_MODEL = 1024
N_HEADS = 8
HEAD_DIM = 64
ATTN_WIDTH = N_HEADS * HEAD_DIM
MOBA_BLOCK = 256
MOBA_TOPK = 3
NUM_BUCKETS = 32
MAX_DISTANCE = 128
POOL_WINDOWS = (2, 4, 8, 16)
POOL_GROUPS = 4
POOL_GROUP_DIM = 128
POOL_WIDTH = POOL_GROUPS * POOL_GROUP_DIM
N_EXPERTS = 32
TOP_K = 4
SWIGLU_LIMIT = 7.0
SWIGLU_ALPHA = 1.702
RMS_EPS = 1e-6

LANES = 128
SUBLANES = 8
ROW_TILES = D_MODEL // LANES

IN_TM = 512
MIX_TM = 512
EXPERT_TM = 256
OUT_TM = 256
ATTN_GROUP = 4
ATTN_HEADS = 4
GATHER_CHUNK = EXPERT_TM // (D_MODEL // LANES)
WEIGHT_DMA_CHUNKS = 4
INVERT_UNROLL = 4
COMBINE_PIECES = 8
GATHER_DEPTH = 4
POOL_HALO = 16
MASKED = -1e30
LOG2E = math.log2(math.e)
Q_SCALE = HEAD_DIM ** -0.5 * LOG2E

VMEM_LIMIT = 56 * 1024 * 1024

_NT = (((1,), (1,)), ((), ()))


def _rms(x, g):
    ms = jnp.mean(x * x, axis=-1, keepdims=True)
    return x * lax.rsqrt(ms + RMS_EPS) * g


def _sigmoid(x):
    return 1.0 / (1.0 + jnp.exp(-x))


def _in_proj_kernel(x_ref, g_ref, w_ref, wvt_ref, q_ref, k_ref, vt_ref, u_ref, gl_ref):
    h = _rms(x_ref[...], g_ref[...]).astype(BF16)
    aw = ATTN_WIDTH

    def proj(lo, hi):
        return jnp.dot(h, w_ref[:, lo:hi], preferred_element_type=F32)

    qv = (proj(0, aw) * Q_SCALE).astype(BF16)
    kv = proj(aw, 2 * aw).astype(BF16)
    for hd in range(N_HEADS):
        q_ref[0, hd] = qv[:, hd * HEAD_DIM:(hd + 1) * HEAD_DIM]
        k_ref[0, hd] = kv[:, hd * HEAD_DIM:(hd + 1) * HEAD_DIM]
    vt = lax.dot_general(wvt_ref[...], h, _NT, preferred_element_type=F32).astype(BF16)
    vt_ref[0] = vt.reshape(N_HEADS, HEAD_DIM, vt.shape[1])
    u_ref[...] = proj(3 * aw, 3 * aw + POOL_WIDTH)
    base = 3 * aw + POOL_WIDTH
    for c in range(2):
        lo = base + c * D_MODEL
        gl_ref[:, c * D_MODEL:(c + 1) * D_MODEL] = proj(lo, lo + D_MODEL).astype(BF16)


def _in_proj(x2, g, w_bf16, wvt_bf16, batch, seq):
    t = x2.shape[0]
    in_cols = w_bf16.shape[1]
    tps = seq // IN_TM
    row = lambda i: (i, 0)
    fixed = lambda i: (0, 0)
    head_rows = lambda i: (i // tps, 0, i % tps, 0)
    head_cols = lambda i: (i // tps, 0, 0, i % tps)
    return pl.pallas_call(
        _in_proj_kernel,
        grid=(t // IN_TM,),
        in_specs=[
            pl.BlockSpec((IN_TM, D_MODEL), row),
            pl.BlockSpec((1, D_MODEL), fixed),
            pl.BlockSpec((D_MODEL, in_cols), fixed),
            pl.BlockSpec((ATTN_WIDTH, D_MODEL), fixed),
        ],
        out_specs=[
            pl.BlockSpec((1, N_HEADS, IN_TM, HEAD_DIM), head_rows),
            pl.BlockSpec((1, N_HEADS, IN_TM, HEAD_DIM), head_rows),
            pl.BlockSpec((1, N_HEADS, HEAD_DIM, IN_TM), head_cols),
            pl.BlockSpec((IN_TM, POOL_WIDTH), row),
            pl.BlockSpec((IN_TM, 2 * D_MODEL), row),
        ],
        out_shape=[
            jax.ShapeDtypeStruct((batch, N_HEADS, seq, HEAD_DIM), BF16),
            jax.ShapeDtypeStruct((batch, N_HEADS, seq, HEAD_DIM), BF16),
            jax.ShapeDtypeStruct((batch, N_HEADS, HEAD_DIM, seq), BF16),
            jax.ShapeDtypeStruct((t, POOL_WIDTH), F32),
            jax.ShapeDtypeStruct((t, 2 * D_MODEL), BF16),
        ],
        compiler_params=pltpu.CompilerParams(
            dimension_semantics=("parallel",), vmem_limit_bytes=VMEM_LIMIT),
        name="in_proj",
    )(x2, g, w_bf16, wvt_bf16)


def _attn_kernel(bfar_ref, q_ref, k_ref, vt_ref, bown_ref, bprev_ref, o_ref,
                 kmean_s, am_s, sa_s, sb_s, ga_s, gb_s, m_s, l_s, acc_s, *, n_blocks):
    hp = pl.program_id(0)
    j = pl.program_id(1)
    blk = MOBA_BLOCK
    grp = ATTN_GROUP
    gk = grp * blk
    n_groups = n_blocks // grp
    n_far = jnp.maximum(j - 1, 0)
    ng = (n_far + grp - 1) // grp
    jp = jnp.maximum(j - 1, 0)
    heads = range(ATTN_HEADS)

    @pl.when(j == 0)
    def _():
        for hh in heads:
            for n in range(n_blocks):
                kb = k_ref[hh, n * blk:(n + 1) * blk, :].astype(F32)
                kmean_s[hh, n:n + 1, :] = jnp.mean(kb, axis=0, keepdims=True)

    qs = [q_ref[hh] for hh in heads]

    def select(hh):
        gate = lax.dot_general(kmean_s[hh], qs[hh].astype(F32), _NT,
                               precision=lax.Precision.HIGHEST, preferred_element_type=F32)
        n_iota = lax.broadcasted_iota(I32, gate.shape, 0)
        past = n_iota < j
        gate = jnp.where(past, gate, jnp.finfo(F32).min)
        sel = jnp.zeros(gate.shape, jnp.bool_)
        for _ in range(MOBA_TOPK):
            mx = jnp.max(gate, axis=0, keepdims=True)
            idx = jnp.min(jnp.where(gate == mx, n_iota, n_blocks), axis=0, keepdims=True)
            pick = n_iota == idx
            sel = jnp.logical_or(sel, pick)
            gate = jnp.where(pick, -jnp.inf, gate)
        sel = jnp.logical_and(sel, past)
        bfar = bfar_ref[(hp * ATTN_HEADS + hh) % N_HEADS]
        am_s[hh] = jnp.where(jnp.logical_and(sel, n_iota < j - 1), bfar, MASKED)
        return jnp.max(jnp.where(jnp.logical_and(sel, n_iota == j - 1), 0.0, MASKED),
                       axis=0, keepdims=True)

    def produce(hh, g, s_ref, gmax_ref):
        g = jnp.minimum(g, n_groups - 1)
        kb = k_ref[hh, pl.ds(pl.multiple_of(g * gk, gk), gk), :]
        s = lax.dot_general(kb, qs[hh], _NT, preferred_element_type=F32)
        s_ref[hh] = s
        gmax = None
        for b in range(grp):
            mb = (jnp.max(s[b * blk:(b + 1) * blk], axis=0, keepdims=True)
                  + am_s[hh, pl.ds(g * grp + b, 1), :])
            gmax = mb if gmax is None else jnp.maximum(gmax, mb)
        gmax_ref[hh] = gmax

    def consume(hh, g, s_ref, gmax_ref):
        m_old = m_s[hh]
        m_new = jnp.maximum(m_old, gmax_ref[hh])
        alpha = jnp.exp2(m_old - m_new)
        parts, lsum = [], None
        for b in range(grp):
            shift = m_new - am_s[hh, pl.ds(g * grp + b, 1), :]
            pb = jnp.exp2(s_ref[hh, b * blk:(b + 1) * blk, :] - shift)
            sb = jnp.sum(pb, axis=0, keepdims=True)
            lsum = sb if lsum is None else lsum + sb
            parts.append(pb.astype(BF16))
        p = jnp.concatenate(parts, axis=0)
        vt = vt_ref[hh, :, pl.ds(pl.multiple_of(g * gk, gk), gk)]
        pv = jnp.dot(vt, p, preferred_element_type=F32)
        m_s[hh] = m_new
        l_s[hh] = alpha * l_s[hh] + lsum
        acc_s[hh] = alpha * acc_s[hh] + pv

    def near_scores(hh, n, bias):
        kb = k_ref[hh, pl.ds(pl.multiple_of(n * blk, blk), blk), :]
        return lax.dot_general(kb, qs[hh], _NT, preferred_element_type=F32) + bias

    def near_pv(hh, n, p):
        vt = vt_ref[hh, :, pl.ds(pl.multiple_of(n * blk, blk), blk)]
        return jnp.dot(vt, p.astype(BF16), preferred_element_type=F32)

    for hh in heads:
        am_prev = select(hh)
        produce(hh, 0, sa_s, ga_s)
        s_own = near_scores(hh, j, bown_ref[hh])
        s_prev = near_scores(hh, jp, bprev_ref[hh])
        m = jnp.maximum(jnp.max(s_own, axis=0, keepdims=True),
                        jnp.max(s_prev, axis=0, keepdims=True) + am_prev)
        p_own = jnp.exp2(s_own - m)
        p_prev = jnp.exp2(s_prev - (m - am_prev))
        m_s[hh] = m
        l_s[hh] = jnp.sum(p_own, axis=0, keepdims=True) + jnp.sum(p_prev, axis=0, keepdims=True)
        acc_s[hh] = near_pv(hh, j, p_own) + near_pv(hh, jp, p_prev)

    def pair(h, carry):
        for hh in heads:
            consume(hh, 2 * h, sa_s, ga_s)
            produce(hh, 2 * h + 1, sb_s, gb_s)
        for hh in heads:
            consume(hh, 2 * h + 1, sb_s, gb_s)
            produce(hh, 2 * h + 2, sa_s, ga_s)
        return carry

    lax.fori_loop(0, ng // 2, pair, 0)

    @pl.when(ng % 2 == 1)
    def _():
        for hh in heads:
            consume(hh, ng - 1, sa_s, ga_s)

    for hh in heads:
        o_ref[hh] = (acc_s[hh] / l_s[hh]).astype(o_ref.dtype)


def _attention(qh, kh, vth, bown, bprev, bfar):
    bhn, s, dh = qh.shape
    nb = s // MOBA_BLOCK
    ah = ATTN_HEADS
    assert N_HEADS % ah == 0 and nb % ATTN_GROUP == 0
    kern = functools.partial(_attn_kernel, n_blocks=nb)
    head_blk = lambda b, j: (b % (N_HEADS // ah), 0, 0)
    grid_spec = pltpu.PrefetchScalarGridSpec(
        num_scalar_prefetch=0,
        grid=(bhn // ah, nb),
        in_specs=[
            pl.BlockSpec(memory_space=pltpu.SMEM),
            pl.BlockSpec((ah, MOBA_BLOCK, dh), lambda b, j: (b, j, 0)),
            pl.BlockSpec((ah, s, dh), lambda b, j: (b, 0, 0)),
            pl.BlockSpec((ah, dh, s), lambda b, j: (b, 0, 0)),
            pl.BlockSpec((ah, MOBA_BLOCK, MOBA_BLOCK), head_blk),
            pl.BlockSpec((ah, MOBA_BLOCK, MOBA_BLOCK), head_blk),
        ],
        out_specs=pl.BlockSpec((ah, dh, MOBA_BLOCK), lambda b, j: (b, 0, j)),
        scratch_shapes=[
            pltpu.VMEM((ah, nb, dh), F32),
            pltpu.VMEM((ah, nb, MOBA_BLOCK), F32),
            pltpu.VMEM((ah, ATTN_GROUP * MOBA_BLOCK, MOBA_BLOCK), F32),
            pltpu.VMEM((ah, ATTN_GROUP * MOBA_BLOCK, MOBA_BLOCK), F32),
            pltpu.VMEM((ah, 1, MOBA_BLOCK), F32),
            pltpu.VMEM((ah, 1, MOBA_BLOCK), F32),
            pltpu.VMEM((ah, 1, MOBA_BLOCK), F32),
            pltpu.VMEM((ah, 1, MOBA_BLOCK), F32),
            pltpu.VMEM((ah, dh, MOBA_BLOCK), F32),
        ],
    )
    return pl.pallas_call(
        kern,
        grid_spec=grid_spec,
        out_shape=jax.ShapeDtypeStruct((bhn, dh, s), BF16),
        compiler_params=pltpu.CompilerParams(
            dimension_semantics=("parallel", "arbitrary"), vmem_limit_bytes=VMEM_LIMIT),
        name="moba_attention",
    )(bfar, qh, kh, vth, bown, bprev)


def _t5_bias_tables(rel_bias):
    blk = MOBA_BLOCK
    n = jnp.arange(2 * blk)
    max_exact = NUM_BUCKETS // 2
    nf = jnp.maximum(n, 1).astype(F32)
    large = max_exact + (jnp.log(nf / max_exact) / math.log(MAX_DISTANCE / max_exact)
                         * (NUM_BUCKETS - max_exact)).astype(I32)
    large = jnp.minimum(large, NUM_BUCKETS - 1)
    bucket = jnp.where(n < max_exact, n, large)
    rel_bias = rel_bias.astype(F32) * LOG2E
    tbl = rel_bias[bucket].T
    key = jnp.arange(blk)[:, None]
    qry = jnp.arange(blk)[None, :]
    d = qry - key
    h = tbl.shape[0]
    wide = 2 * blk + 1
    skew = jnp.broadcast_to(jnp.pad(tbl, ((0, 0), (0, 1)))[:, None, :], (h, blk, wide))
    skew = skew.reshape(h, blk * wide)[:, :blk * 2 * blk].reshape(h, blk, 2 * blk)
    bown = jnp.where(d >= 0, skew[:, :, :blk], MASKED)
    bprev = skew[:, :, blk:]
    bfar = rel_bias[NUM_BUCKETS - 1]
    return bown, bprev, bfar


def _mix_kernel(x_ref, attn_ref, u_ref, halo_ref, gl_ref, pw_ref, ps_ref, woa_ref, wop_ref,
                wout_ref, gffn_ref, rwt_ref, rb_ref,
                x1_ref, hf_ref, idx_ref, rank_ref, gate_ref, cnt_ref,
                ext_s, carry_s, *, tiles_per_seq):
    i = pl.program_id(0)
    tm = MIX_TM

    @pl.when(i == 0)
    def _():
        carry_s[...] = jnp.zeros_like(carry_s)

    first = (i % tiles_per_seq) == 0
    ext_s[0:POOL_HALO, :] = jnp.where(first, 0.0, halo_ref[...])
    ext_s[POOL_HALO:POOL_HALO + tm, :] = u_ref[...]
    pos = (i % tiles_per_seq) * tm + lax.broadcasted_iota(I32, (tm, 1), 0)
    pooled_parts = []
    for g, w in enumerate(POOL_WINDOWS):
        c0, c1 = g * POOL_GROUP_DIM, (g + 1) * POOL_GROUP_DIM
        win = ext_s[POOL_HALO:POOL_HALO + tm, c0:c1]
        for sft in range(1, w):
            win = win + ext_s[POOL_HALO - sft:POOL_HALO - sft + tm, c0:c1]
        cnt = jnp.minimum(pos + 1, w).astype(F32)
        pin = win / cnt - ext_s[POOL_HALO:POOL_HALO + tm, c0:c1]
        pooled_parts.append(jnp.dot(pin.astype(BF16), pw_ref[g], preferred_element_type=F32))
    pooled = jnp.concatenate(pooled_parts, axis=1) * ps_ref[...]

    a = lax.dot_general(attn_ref[0], woa_ref[...], (((0,), (0,)), ((), ())),
                        preferred_element_type=F32)
    pm = jnp.dot(pooled.astype(BF16), wop_ref[...], preferred_element_type=F32)
    g0 = _sigmoid(gl_ref[:, 0:D_MODEL].astype(F32))
    g1 = _sigmoid(gl_ref[:, D_MODEL:2 * D_MODEL].astype(F32))
    merged = g0 * a + g1 * pm
    x1 = x_ref[...] + jnp.dot(merged.astype(BF16), wout_ref[...], preferred_element_type=F32)
    x1_ref[...] = x1

    hf = _rms(x1, gffn_ref[...])
    hfb = hf.astype(BF16)
    for s in range(ROW_TILES):
        hf_ref[pl.ds(s, tm, stride=ROW_TILES), :] = hfb[:, s * LANES:(s + 1) * LANES].astype(F32)

    logits = lax.dot_general(rwt_ref[...], hfb.astype(F32), _NT,
                             precision=lax.Precision.HIGHEST,
                             preferred_element_type=F32) + rb_ref[...]
    e_iota = lax.broadcasted_iota(I32, logits.shape, 0)
    vals, picks = [], []
    for _ in range(TOP_K):
        mx = jnp.max(logits, axis=0, keepdims=True)
        idx = jnp.min(jnp.where(logits == mx, e_iota, N_EXPERTS), axis=0, keepdims=True)
        pick = e_iota == idx
        vals.append(mx)
        picks.append(pick)
        idx_ref[len(picks) - 1:len(picks), :] = idx
        logits = jnp.where(pick, -jnp.inf, logits)
    ex = [jnp.exp(v - vals[0]) for v in vals]
    den = ex[0] + ex[1] + ex[2] + ex[3]
    for k in range(TOP_K):
        gate_ref[k:k + 1, :] = ex[k] / den

    onehot = jnp.zeros(e_iota.shape, F32)
    for pick in picks:
        onehot = onehot + pick.astype(F32)
    ra = lax.broadcasted_iota(I32, (tm, tm), 0)
    rb = lax.broadcasted_iota(I32, (tm, tm), 1)
    upper = (ra < rb).astype(BF16)
    before = jnp.dot(onehot.astype(BF16), upper, preferred_element_type=F32) + carry_s[...]
    for k, pick in enumerate(picks):
        rank_ref[k:k + 1, :] = jnp.sum(jnp.where(pick, before, 0.0), axis=0,
                                       keepdims=True).astype(I32)
    carry_s[...] = carry_s[...] + jnp.sum(onehot, axis=1, keepdims=True)
    cnt_ref[...] = jnp.broadcast_to(carry_s[...], cnt_ref.shape).astype(I32)


def _mix(x2, attn, u, gl, pw, ps, woa, wop, wout, gffn, rwt, rb, seq):
    t = x2.shape[0]
    tm = MIX_TM
    tiles_per_seq = seq // tm
    halo_per_tile = tm // POOL_HALO
    row = lambda i: (i, 0)
    fixed2 = lambda i: (0, 0)
    fixed3 = lambda i: (0, 0, 0)
    col = lambda i: (0, i)
    kern = functools.partial(_mix_kernel, tiles_per_seq=tiles_per_seq)
    return pl.pallas_call(
        kern,
        grid=(t // tm,),
        in_specs=[
            pl.BlockSpec((tm, D_MODEL), row),
            pl.BlockSpec((1, ATTN_WIDTH, tm), lambda i: (i // tiles_per_seq, 0, i % tiles_per_seq)),
            pl.BlockSpec((tm, POOL_WIDTH), row),
            pl.BlockSpec((POOL_HALO, POOL_WIDTH),
                         lambda i: (jnp.maximum(i * halo_per_tile - 1, 0), 0)),
            pl.BlockSpec((tm, 2 * D_MODEL), row),
            pl.BlockSpec((POOL_GROUPS, POOL_GROUP_DIM, POOL_GROUP_DIM), fixed3),
            pl.BlockSpec((1, POOL_WIDTH), fixed2),
            pl.BlockSpec((ATTN_WIDTH, D_MODEL), fixed2),
            pl.BlockSpec((POOL_WIDTH, D_MODEL), fixed2),
            pl.BlockSpec((D_MODEL, D_MODEL), fixed2),
            pl.BlockSpec((1, D_MODEL), fixed2),
            pl.BlockSpec((N_EXPERTS, D_MODEL), fixed2),
            pl.BlockSpec((N_EXPERTS, 1), fixed2),
        ],
        out_specs=[
            pl.BlockSpec((tm, D_MODEL), row),
            pl.BlockSpec((tm * ROW_TILES, LANES), row),
            pl.BlockSpec((TOP_K, tm), col),
            pl.BlockSpec((TOP_K, tm), col),
            pl.BlockSpec((TOP_K, tm), col),
            pl.BlockSpec((N_EXPERTS, LANES), fixed2),
        ],
        out_shape=[
            jax.ShapeDtypeStruct((t, D_MODEL), F32),
            jax.ShapeDtypeStruct((t * ROW_TILES, LANES), F32),
            jax.ShapeDtypeStruct((TOP_K, t), I32),
            jax.ShapeDtypeStruct((TOP_K, t), I32),
            jax.ShapeDtypeStruct((TOP_K, t), F32),
            jax.ShapeDtypeStruct((N_EXPERTS, LANES), I32),
        ],
        scratch_shapes=[
            pltpu.VMEM((POOL_HALO + tm, POOL_WIDTH), F32),
            pltpu.VMEM((N_EXPERTS, 1), F32),
        ],
        compiler_params=pltpu.CompilerParams(
            dimension_semantics=("arbitrary",), vmem_limit_bytes=VMEM_LIMIT),
        name="mix_router",
    )(x2, attn, u, u, gl, pw, ps, woa, wop, wout, gffn, rwt, rb)


def _row(ref, r):
    return ref.at[pl.ds(pl.multiple_of(r * ROW_TILES, ROW_TILES), ROW_TILES)]


def _pair_split_matrix():
    n = 2 * LANES
    r = lax.broadcasted_iota(I32, (n, n), 0)
    c = lax.broadcasted_iota(I32, (n, n), 1)
    src = jnp.where(c < LANES, 2 * c, 2 * (c - LANES) + 1)
    return (r == src).astype(BF16)


def _expert_kernel(be_ref, nused_ref, dest_ref, first_ref, par_ref, next_ref, padlo_ref, padhi_ref,
                   hf_ref, wgu_hbm, wd_hbm, bgu_ref, bd_ref,
                   y_ref, xbuf, wgu_f, wd_f, wgu_s, wd_s, rowsrc_ref, sems, wsems):
    i = pl.program_id(0)
    tm = EXPERT_TM
    n_used = nused_ref[0]
    slot = i % GATHER_DEPTH
    ahead = GATHER_DEPTH - 1

    def start_rows(blk, to_slot, r):
        src = pl.multiple_of(rowsrc_ref[blk * tm + r], ROW_TILES)
        pltpu.make_async_copy(hf_ref.at[pl.ds(src, ROW_TILES)], _row(xbuf.at[to_slot], r),
                              sems.at[to_slot]).start()

    def wait_rows(at_slot):
        pltpu.make_async_copy(hf_ref.at[pl.ds(0, tm * ROW_TILES)], xbuf.at[at_slot],
                              sems.at[at_slot]).wait()

    def weight_copies(e, w_slot, start):
        rows = D_MODEL // WEIGHT_DMA_CHUNKS
        for c in range(WEIGHT_DMA_CHUNKS):
            sl = pl.ds(c * rows, rows)
            for src, dst in ((wgu_hbm, wgu_f), (wd_hbm, wd_f)):
                if start:
                    pltpu.async_copy(src.at[e, sl], dst.at[w_slot, sl], wsems.at[w_slot],
                                     priority=1)
                else:
                    pltpu.make_async_copy(src.at[e, sl], dst.at[w_slot, sl],
                                          wsems.at[w_slot]).wait()

    @pl.when(i == 0)
    def _():
        weight_copies(be_ref[0], 0, True)

        def zero_range(e, carry):
            def zero(r, c):
                rowsrc_ref[r] = 0
                return c
            lax.fori_loop(padlo_ref[e], padhi_ref[e], zero, 0)
            return carry
        lax.fori_loop(0, N_EXPERTS + 1, zero_range, 0)

        def invert(t0, carry):
            for u in range(INVERT_UNROLL):
                tok = t0 * INVERT_UNROLL + u
                for k in range(TOP_K):
                    rowsrc_ref[dest_ref[tok * TOP_K + k]] = tok * ROW_TILES
            return carry
        lax.fori_loop(0, dest_ref.shape[0] // (TOP_K * INVERT_UNROLL), invert, 0)

        def row(r, carry):
            for blk in range(ahead):
                start_rows(blk, blk, r)
            return carry
        lax.fori_loop(0, tm, row, 0)

    @pl.when(jnp.logical_and(first_ref[i] == 1, i < n_used))
    def _():
        par = par_ref[i]
        weight_copies(be_ref[i], par, False)

        @pl.when(next_ref[i] >= 0)
        def _():
            weight_copies(next_ref[i], 1 - par, True)

        split = _pair_split_matrix()
        for c in range(2 * D_MODEL // (2 * LANES)):
            lo, hi = c * 2 * LANES, (c + 1) * 2 * LANES
            wgu_s[:, lo:hi] = jnp.dot(wgu_f[par, :, lo:hi].astype(BF16), split,
                                      preferred_element_type=F32).astype(BF16)
        wd_s[...] = wd_f[par].astype(BF16)

    @pl.when(i >= n_used)
    def _():
        y_ref[...] = jnp.zeros_like(y_ref)

    @pl.when(jnp.logical_and(i >= n_used, i < n_used + ahead))
    def _():
        wait_rows(slot)

    @pl.when(i < n_used)
    def _():
        wait_rows(slot)
        xb = jnp.concatenate(
            [xbuf[slot, pl.ds(s, tm, stride=ROW_TILES), :] for s in range(ROW_TILES)],
            axis=1).astype(BF16)
        hid_parts = []
        n_chunks = D_MODEL // LANES
        assert n_chunks * GATHER_CHUNK == tm
        for c in range(n_chunks):
            lo, hi = c * 2 * LANES, (c + 1) * 2 * LANES
            gu = jnp.dot(xb, wgu_s[:, lo:hi], preferred_element_type=F32) + bgu_ref[0, :, lo:hi]
            glu = jnp.minimum(gu[:, :LANES], SWIGLU_LIMIT)
            lin = jnp.clip(gu[:, LANES:], -SWIGLU_LIMIT, SWIGLU_LIMIT)
            hid_parts.append(((lin + 1.0) * (glu * _sigmoid(SWIGLU_ALPHA * glu))).astype(BF16))
            for r in range(c * GATHER_CHUNK, (c + 1) * GATHER_CHUNK):
                start_rows(i + ahead, (i + ahead) % GATHER_DEPTH, r)
        hid = jnp.concatenate(hid_parts, axis=1)
        y = jnp.dot(hid, wd_s[...], preferred_element_type=F32) + bd_ref[0]
        for s in range(ROW_TILES):
            y_ref[pl.ds(s, tm, stride=ROW_TILES), :] = y[:, s * LANES:(s + 1) * LANES]


def _experts(blk_expert, n_used, dest, first, parity, next_expert, pad_lo, pad_hi, hf_rows,
             wgu, wd, bgu_split, bd):
    n_blk = blk_expert.shape[0]
    tm = EXPERT_TM
    assert dest.shape[0] % (TOP_K * INVERT_UNROLL) == 0

    def ymap(i, be, nu, *_):
        return (i, 0)

    def bmap(i, be, nu, *_):
        return (be[jnp.minimum(i, nu[0] - 1)], 0, 0)

    grid_spec = pltpu.PrefetchScalarGridSpec(
        num_scalar_prefetch=8,
        grid=(n_blk,),
        in_specs=[
            pl.BlockSpec(memory_space=pl.ANY),
            pl.BlockSpec(memory_space=pl.ANY),
            pl.BlockSpec(memory_space=pl.ANY),
            pl.BlockSpec((1, 1, 2 * D_MODEL), bmap),
            pl.BlockSpec((1, 1, D_MODEL), bmap),
        ],
        out_specs=pl.BlockSpec((tm * ROW_TILES, LANES), ymap),
        scratch_shapes=[
            pltpu.VMEM((GATHER_DEPTH, tm * ROW_TILES, LANES), F32),
            pltpu.VMEM((2, D_MODEL, 2 * D_MODEL), F32),
            pltpu.VMEM((2, D_MODEL, D_MODEL), F32),
            pltpu.VMEM((D_MODEL, 2 * D_MODEL), BF16),
            pltpu.VMEM((D_MODEL, D_MODEL), BF16),
            pltpu.SMEM((n_blk * tm,), I32),
            pltpu.SemaphoreType.DMA((GATHER_DEPTH,)),
            pltpu.SemaphoreType.DMA((2,)),
        ],
    )
    return pl.pallas_call(
        _expert_kernel,
        grid_spec=grid_spec,
        out_shape=jax.ShapeDtypeStruct((n_blk * tm * ROW_TILES, LANES), F32),
        compiler_params=pltpu.CompilerParams(
            dimension_semantics=("arbitrary",), vmem_limit_bytes=VMEM_LIMIT),
        name="expert_ffn",
    )(blk_expert, n_used, dest, first, parity, next_expert, pad_lo, pad_hi, hf_rows, wgu, wd,
      bgu_split, bd)


def _combine_kernel(dest_ref, x1_ref, gate_ref, ys_ref, p_ref, gple_ref, wpg_ref, wpp_ref,
                    gfin_ref, o_ref, ybuf, sems):
    i = pl.program_id(0)
    n_steps = pl.num_programs(0)
    tm = OUT_TM
    slot = i % 2

    def start_row(tile, to_slot, r, k):
        src = pl.multiple_of(dest_ref[(tile * tm + r) * TOP_K + k], ROW_TILES)
        pltpu.make_async_copy(ys_ref.at[pl.ds(src, ROW_TILES)], _row(ybuf.at[to_slot], k * tm + r),
                              sems.at[to_slot]).start()

    def wait_tile(at_slot):
        pltpu.make_async_copy(ys_ref.at[pl.ds(0, TOP_K * tm * ROW_TILES)], ybuf.at[at_slot],
                              sems.at[at_slot]).wait()

    @pl.when(i == 0)
    def _():
        def tok(r, carry):
            for k in range(TOP_K):
                start_row(0, 0, r, k)
            return carry
        lax.fori_loop(0, tm, tok, 0)

    nxt = jnp.minimum(i + 1, n_steps - 1)
    per_piece = tm // COMBINE_PIECES

    def fetch_piece(piece):
        for r in range(piece * per_piece, (piece + 1) * per_piece):
            for k in range(TOP_K):
                start_row(nxt, 1 - slot, r, k)

    wait_tile(slot)
    moe = None
    for k in range(TOP_K):
        yk = jnp.concatenate(
            [ybuf[slot, pl.ds(k * tm * ROW_TILES + s, tm, stride=ROW_TILES), :]
             for s in range(ROW_TILES)], axis=1)
        term = gate_ref[:, k:k + 1] * yk
        moe = term if moe is None else moe + term
        fetch_piece(k)
    x2 = x1_ref[...] + moe
    hp = _rms(x2, gple_ref[...]).astype(BF16)
    fetch_piece(4)
    pg = _sigmoid(jnp.dot(hp, wpg_ref[...], preferred_element_type=F32))
    fetch_piece(5)
    proj = jnp.dot(p_ref[...].astype(BF16), wpp_ref[...], preferred_element_type=F32)
    fetch_piece(6)
    x3 = x2 + pg * proj
    o_ref[...] = _rms(x3, gfin_ref[...])
    fetch_piece(7)

    @pl.when(i == n_steps - 1)
    def _():
        wait_tile(1 - slot)


def _combine(dest_flat, x1, gates_tk, ys, p2, gple, wpg, wpp, gfin):
    t = x1.shape[0]
    tm = OUT_TM
    ple = p2.shape[1]
    row = lambda i, d: (i, 0)
    fixed = lambda i, d: (0, 0)
    grid_spec = pltpu.PrefetchScalarGridSpec(
        num_scalar_prefetch=1,
        grid=(t // tm,),
        in_specs=[
            pl.BlockSpec((tm, D_MODEL), row),
            pl.BlockSpec((tm, TOP_K), row),
            pl.BlockSpec(memory_space=pl.ANY),
            pl.BlockSpec((tm, ple), row),
            pl.BlockSpec((1, D_MODEL), fixed),
            pl.BlockSpec((D_MODEL, D_MODEL), fixed),
            pl.BlockSpec((ple, D_MODEL), fixed),
            pl.BlockSpec((1, D_MODEL), fixed),
        ],
        out_specs=pl.BlockSpec((tm, D_MODEL), row),
        scratch_shapes=[
            pltpu.VMEM((2, TOP_K * tm * ROW_TILES, LANES), F32),
            pltpu.SemaphoreType.DMA((2,)),
        ],
    )
    return pl.pallas_call(
        _combine_kernel,
        grid_spec=grid_spec,
        out_shape=jax.ShapeDtypeStruct((t, D_MODEL), F32),
        compiler_params=pltpu.CompilerParams(
            dimension_semantics=("arbitrary",), vmem_limit_bytes=VMEM_LIMIT),
        name="combine_ple_final",
    )(dest_flat, x1, gates_tk, ys, p2, gple, wpg, wpp, gfin)


def kernel(x, p, rel_bias, norm_mix_g, w_in, pool_w, pool_scale, w_o_attn, w_o_pool, w_out,
           norm_ffn_g, router_w, router_b, w_gate_up, b_gate_up, w_down, b_down,
           norm_ple_g, w_ple_gate, w_ple_proj, norm_final_g):
    b, s, d = x.shape
    depth = w_in.shape[0]
    t = b * s
    assert d == D_MODEL and s % MOBA_BLOCK == 0 and t % IN_TM == 0 and s % MIX_TM == 0
    assert depth == 1, "the final norm is fused into the layer's last kernel"
    n_pad = t * TOP_K + (N_EXPERTS + GATHER_DEPTH - 2) * EXPERT_TM
    n_blk = n_pad // EXPERT_TM
    bown, bprev, bfar = _t5_bias_tables(rel_bias)

    x2 = x.reshape(t, d)
    for i in range(depth):
        w_in_b = w_in[i].astype(BF16)
        wvt = w_in[i][:, 2 * ATTN_WIDTH:3 * ATTN_WIDTH].T.astype(BF16)
        q, k, vt, u, gl = _in_proj(x2, norm_mix_g[i].reshape(1, d), w_in_b, wvt, b, s)
        bh = b * N_HEADS
        attn_t = _attention(q.reshape(bh, s, HEAD_DIM), k.reshape(bh, s, HEAD_DIM),
                            vt.reshape(bh, HEAD_DIM, s), bown, bprev, bfar)
        attn = attn_t.reshape(b, ATTN_WIDTH, s)

        x1, hf_rows, idx_kt, rank_kt, gate_kt, cnt = _mix(
            x2, attn, u, gl, pool_w[i].astype(BF16), pool_scale[i].reshape(1, POOL_WIDTH),
            w_o_attn[i].astype(BF16), w_o_pool[i].astype(BF16), w_out[i].astype(BF16),
            norm_ffn_g[i].reshape(1, d), router_w[i].T, router_b[i].reshape(N_EXPERTS, 1), s)

        counts = cnt[:, 0]
        padded = (counts + EXPERT_TM - 1) // EXPERT_TM * EXPERT_TM
        pend = jnp.cumsum(padded)
        pstart = pend - padded
        e_ids = jnp.arange(N_EXPERTS, dtype=I32)
        pstart_of = jnp.sum(jnp.where(idx_kt[..., None] == e_ids, pstart, 0), axis=-1)
        dest_flat = (pstart_of + rank_kt).T.reshape(-1).astype(I32)
        blk_row0 = jnp.arange(n_blk, dtype=I32) * EXPERT_TM
        blk_expert = jnp.minimum(jnp.sum(pend[None, :] <= blk_row0[:, None], axis=1),
                                 N_EXPERTS - 1).astype(I32)
        n_used = (pend[-1:] // EXPERT_TM).astype(I32)
        pad_lo = jnp.concatenate([pstart + counts, pend[-1:]]).astype(I32)
        pad_hi = jnp.concatenate([pend, pend[-1:] + (GATHER_DEPTH - 1) * EXPERT_TM]).astype(I32)

        bgu = b_gate_up[i].reshape(N_EXPERTS, D_MODEL // LANES, LANES, 2).transpose(
            0, 1, 3, 2).reshape(N_EXPERTS, 1, 2 * D_MODEL)
        first = jnp.concatenate([jnp.ones((1,), I32),
                                 (blk_expert[1:] != blk_expert[:-1]).astype(I32)])
        parity = ((jnp.cumsum(first) - 1) % 2).astype(I32)
        later = jnp.logical_and(e_ids[None, :] > e_ids[:, None], (padded > 0)[None, :])
        next_e = jnp.min(jnp.where(later, e_ids[None, :], N_EXPERTS), axis=1)
        next_e = jnp.where(next_e == N_EXPERTS, -1, next_e).astype(I32)
        next_expert = jnp.sum(jnp.where(blk_expert[:, None] == e_ids, next_e, 0),
                              axis=1).astype(I32)
        ys = _experts(blk_expert, n_used, dest_flat, first, parity, next_expert, pad_lo, pad_hi,
                      hf_rows, w_gate_up[i], w_down[i], bgu, b_down[i][:, None, :])

        x2 = _combine(dest_flat * ROW_TILES, x1, gate_kt.T, ys, p[i].reshape(t, -1),
                      norm_ple_g[i].reshape(1, d), w_ple_gate[i].astype(BF16),
                      w_ple_proj[i].astype(BF16), norm_final_g.reshape(1, d))
    return x2.reshape(b, s, d)
```

```python
import functools
import math

import jax
import jax.numpy as jnp
from jax import lax
from jax.experimental import pallas as pl
from jax.experimental.pallas import tpu as pltpu

F32 = jnp.float32
BF16 = jnp.bfloat16
I32 = jnp.int32

D_MODEL = 1024
N_HEADS = 8
HEAD_DIM = 64
ATTN_WIDTH = N_HEADS * HEAD_DIM
MOBA_BLOCK = 256
MOBA_TOPK = 3
NUM_BUCKETS = 32
MAX_DISTANCE = 128
POOL_WINDOWS = (2, 4, 8, 16)
POOL_GROUPS = 4
POOL_GROUP_DIM = 128
POOL_WIDTH = POOL_GROUPS * POOL_GROUP_DIM
N_EXPERTS = 32
TOP_K = 4
SWIGLU_LIMIT = 7.0
SWIGLU_ALPHA = 1.702
RMS_EPS = 1e-6

LANES = 128
SUBLANES = 8
ROW_TILES = D_MODEL // LANES

IN_TM = 512
MIX_TM = 512
EXPERT_TM = 256
OUT_TM = 256
ATTN_GROUP = 4
ATTN_HEADS = 4
GATHER_CHUNK = EXPERT_TM // (D_MODEL // LANES)
WEIGHT_DMA_CHUNKS = 4
INVERT_UNROLL = 4
GATHER_DEPTH = 4
POOL_HALO = 16
MASKED = -1e30
LOG2E = math.log2(math.e)
Q_SCALE = HEAD_DIM ** -0.5 * LOG2E

VMEM_LIMIT = 56 * 1024 * 1024

_NT = (((1,), (1,)), ((), ()))


def _rms(x, g):
    ms = jnp.mean(x * x, axis=-1, keepdims=True)
    return x * lax.rsqrt(ms + RMS_EPS) * g


def _sigmoid(x):
    return 1.0 / (1.0 + jnp.exp(-x))


def _in_proj_kernel(x_ref, g_ref, w_ref, wvt_ref, q_ref, k_ref, vt_ref, u_ref, gl_ref):
    h = _rms(x_ref[...], g_ref[...]).astype(BF16)
    aw = ATTN_WIDTH

    def proj(lo, hi):
        return jnp.dot(h, w_ref[:, lo:hi], preferred_element_type=F32)

    qv = (proj(0, aw) * Q_SCALE).astype(BF16)
    kv = proj(aw, 2 * aw).astype(BF16)
    for hd in range(N_HEADS):
        q_ref[0, hd] = qv[:, hd * HEAD_DIM:(hd + 1) * HEAD_DIM]
        k_ref[0, hd] = kv[:, hd * HEAD_DIM:(hd + 1) * HEAD_DIM]
    vt = lax.dot_general(wvt_ref[...], h, _NT, preferred_element_type=F32).astype(BF16)
    vt_ref[0] = vt.reshape(N_HEADS, HEAD_DIM, vt.shape[1])
    u_ref[...] = proj(3 * aw, 3 * aw + POOL_WIDTH)
    base = 3 * aw + POOL_WIDTH
    for c in range(2):
        lo = base + c * D_MODEL
        gl_ref[:, c * D_MODEL:(c + 1) * D_MODEL] = proj(lo, lo + D_MODEL).astype(BF16)


def _in_proj(x2, g, w_bf16, wvt_bf16, batch, seq):
    t = x2.shape[0]
    in_cols = w_bf16.shape[1]
    tps = seq // IN_TM
    row = lambda i: (i, 0)
    fixed = lambda i: (0, 0)
    head_rows = lambda i: (i // tps, 0, i % tps, 0)
    head_cols = lambda i: (i // tps, 0, 0, i % tps)
    return pl.pallas_call(
        _in_proj_kernel,
        grid=(t // IN_TM,),
        in_specs=[
            pl.BlockSpec((IN_TM, D_MODEL), row),
            pl.BlockSpec((1, D_MODEL), fixed),
            pl.BlockSpec((D_MODEL, in_cols), fixed),
            pl.BlockSpec((ATTN_WIDTH, D_MODEL), fixed),
        ],
        out_specs=[
            pl.BlockSpec((1, N_HEADS, IN_TM, HEAD_DIM), head_rows),
            pl.BlockSpec((1, N_HEADS, IN_TM, HEAD_DIM), head_rows),
            pl.BlockSpec((1, N_HEADS, HEAD_DIM, IN_TM), head_cols),
            pl.BlockSpec((IN_TM, POOL_WIDTH), row),
            pl.BlockSpec((IN_TM, 2 * D_MODEL), row),
        ],
        out_shape=[
            jax.ShapeDtypeStruct((batch, N_HEADS, seq, HEAD_DIM), BF16),
            jax.ShapeDtypeStruct((batch, N_HEADS, seq, HEAD_DIM), BF16),
            jax.ShapeDtypeStruct((batch, N_HEADS, HEAD_DIM, seq), BF16),
            jax.ShapeDtypeStruct((t, POOL_WIDTH), F32),
            jax.ShapeDtypeStruct((t, 2 * D_MODEL), BF16),
        ],
        compiler_params=pltpu.CompilerParams(
            dimension_semantics=("parallel",), vmem_limit_bytes=VMEM_LIMIT),
        name="in_proj",
    )(x2, g, w_bf16, wvt_bf16)


def _attn_kernel(bfar_ref, q_ref, k_ref, vt_ref, bown_ref, bprev_ref, o_ref,
                 kmean_s, am_s, sa_s, sb_s, ga_s, gb_s, m_s, l_s, acc_s, *, n_blocks):
    hp = pl.program_id(0)
    j = pl.program_id(1)
    blk = MOBA_BLOCK
    grp = ATTN_GROUP
    gk = grp * blk
    n_groups = n_blocks // grp
    n_far = jnp.maximum(j - 1, 0)
    ng = (n_far + grp - 1) // grp
    jp = jnp.maximum(j - 1, 0)
    heads = range(ATTN_HEADS)

    @pl.when(j == 0)
    def _():
        for hh in heads:
            for n in range(n_blocks):
                kb = k_ref[hh, n * blk:(n + 1) * blk, :].astype(F32)
                kmean_s[hh, n:n + 1, :] = jnp.mean(kb, axis=0, keepdims=True)

    qs = [q_ref[hh] for hh in heads]

    def select(hh):
        gate = lax.dot_general(kmean_s[hh], qs[hh].astype(F32), _NT,
                               precision=lax.Precision.HIGHEST, preferred_element_type=F32)
        n_iota = lax.broadcasted_iota(I32, gate.shape, 0)
        past = n_iota < j
        gate = jnp.where(past, gate, jnp.finfo(F32).min)
        sel = jnp.zeros(gate.shape, jnp.bool_)
        for _ in range(MOBA_TOPK):
            mx = jnp.max(gate, axis=0, keepdims=True)
            idx = jnp.min(jnp.where(gate == mx, n_iota, n_blocks), axis=0, keepdims=True)
            pick = n_iota == idx
            sel = jnp.logical_or(sel, pick)
            gate = jnp.where(pick, -jnp.inf, gate)
        sel = jnp.logical_and(sel, past)
        bfar = bfar_ref[(hp * ATTN_HEADS + hh) % N_HEADS]
        am_s[hh] = jnp.where(jnp.logical_and(sel, n_iota < j - 1), bfar, MASKED)
        return jnp.max(jnp.where(jnp.logical_and(sel, n_iota == j - 1), 0.0, MASKED),
                       axis=0, keepdims=True)

    def produce(hh, g, s_ref, gmax_ref):
        g = jnp.minimum(g, n_groups - 1)
        kb = k_ref[hh, pl.ds(pl.multiple_of(g * gk, gk), gk), :]
        s = lax.dot_general(kb, qs[hh], _NT, preferred_element_type=F32)
        s_ref[hh] = s
        gmax = None
        for b in range(grp):
            mb = (jnp.max(s[b * blk:(b + 1) * blk], axis=0, keepdims=True)
                  + am_s[hh, pl.ds(g * grp + b, 1), :])
            gmax = mb if gmax is None else jnp.maximum(gmax, mb)
        gmax_ref[hh] = gmax

    def consume(hh, g, s_ref, gmax_ref):
        m_old = m_s[hh]
        m_new = jnp.maximum(m_old, gmax_ref[hh])
        alpha = jnp.exp2(m_old - m_new)
        parts, lsum = [], None
        for b in range(grp):
            shift = m_new - am_s[hh, pl.ds(g * grp + b, 1), :]
            pb = jnp.exp2(s_ref[hh, b * blk:(b + 1) * blk, :] - shift)
            sb = jnp.sum(pb, axis=0, keepdims=True)
            lsum = sb if lsum is None else lsum + sb
            parts.append(pb.astype(BF16))
        p = jnp.concatenate(parts, axis=0)
        vt = vt_ref[hh, :, pl.ds(pl.multiple_of(g * gk, gk), gk)]
        pv = jnp.dot(vt, p, preferred_element_type=F32)
        m_s[hh] = m_new
        l_s[hh] = alpha * l_s[hh] + lsum
        acc_s[hh] = alpha * acc_s[hh] + pv

    def near_scores(hh, n, bias):
        kb = k_ref[hh, pl.ds(pl.multiple_of(n * blk, blk), blk), :]
        return lax.dot_general(kb, qs[hh], _NT, preferred_element_type=F32) + bias

    def near_pv(hh, n, p):
        vt = vt_ref[hh, :, pl.ds(pl.multiple_of(n * blk, blk), blk)]
        return jnp.dot(vt, p.astype(BF16), preferred_element_type=F32)

    for hh in heads:
        am_prev = select(hh)
        produce(hh, 0, sa_s, ga_s)
        s_own = near_scores(hh, j, bown_ref[hh])
        s_prev = near_scores(hh, jp, bprev_ref[hh])
        m = jnp.maximum(jnp.max(s_own, axis=0, keepdims=True),
                        jnp.max(s_prev, axis=0, keepdims=True) + am_prev)
        p_own = jnp.exp2(s_own - m)
        p_prev = jnp.exp2(s_prev - (m - am_prev))
        m_s[hh] = m
        l_s[hh] = jnp.sum(p_own, axis=0, keepdims=True) + jnp.sum(p_prev, axis=0, keepdims=True)
        acc_s[hh] = near_pv(hh, j, p_own) + near_pv(hh, jp, p_prev)

    def pair(h, carry):
        for hh in heads:
            consume(hh, 2 * h, sa_s, ga_s)
            produce(hh, 2 * h + 1, sb_s, gb_s)
        for hh in heads:
            consume(hh, 2 * h + 1, sb_s, gb_s)
            produce(hh, 2 * h + 2, sa_s, ga_s)
        return carry

    lax.fori_loop(0, ng // 2, pair, 0)

    @pl.when(ng % 2 == 1)
    def _():
        for hh in heads:
            consume(hh, ng - 1, sa_s, ga_s)

    for hh in heads:
        o_ref[hh] = (acc_s[hh] / l_s[hh]).astype(o_ref.dtype)


def _attention(qh, kh, vth, bown, bprev, bfar):
    bhn, s, dh = qh.shape
    nb = s // MOBA_BLOCK
    ah = ATTN_HEADS
    assert N_HEADS % ah == 0 and nb % ATTN_GROUP == 0
    kern = functools.partial(_attn_kernel, n_blocks=nb)
    head_blk = lambda b, j: (b % (N_HEADS // ah), 0, 0)
    grid_spec = pltpu.PrefetchScalarGridSpec(
        num_scalar_prefetch=0,
        grid=(bhn // ah, nb),
        in_specs=[
            pl.BlockSpec(memory_space=pltpu.SMEM),
            pl.BlockSpec((ah, MOBA_BLOCK, dh), lambda b, j: (b, j, 0)),
            pl.BlockSpec((ah, s, dh), lambda b, j: (b, 0, 0)),
            pl.BlockSpec((ah, dh, s), lambda b, j: (b, 0, 0)),
            pl.BlockSpec((ah, MOBA_BLOCK, MOBA_BLOCK), head_blk),
            pl.BlockSpec((ah, MOBA_BLOCK, MOBA_BLOCK), head_blk),
        ],
        out_specs=pl.BlockSpec((ah, dh, MOBA_BLOCK), lambda b, j: (b, 0, j)),
        scratch_shapes=[
            pltpu.VMEM((ah, nb, dh), F32),
            pltpu.VMEM((ah, nb, MOBA_BLOCK), F32),
            pltpu.VMEM((ah, ATTN_GROUP * MOBA_BLOCK, MOBA_BLOCK), F32),
            pltpu.VMEM((ah, ATTN_GROUP * MOBA_BLOCK, MOBA_BLOCK), F32),
            pltpu.VMEM((ah, 1, MOBA_BLOCK), F32),
            pltpu.VMEM((ah, 1, MOBA_BLOCK), F32),
            pltpu.VMEM((ah, 1, MOBA_BLOCK), F32),
            pltpu.VMEM((ah, 1, MOBA_BLOCK), F32),
            pltpu.VMEM((ah, dh, MOBA_BLOCK), F32),
        ],
    )
    return pl.pallas_call(
        kern,
        grid_spec=grid_spec,
        out_shape=jax.ShapeDtypeStruct((bhn, dh, s), BF16),
        compiler_params=pltpu.CompilerParams(
            dimension_semantics=("parallel", "arbitrary"), vmem_limit_bytes=VMEM_LIMIT),
        name="moba_attention",
    )(bfar, qh, kh, vth, bown, bprev)


def _t5_bias_tables(rel_bias):
    blk = MOBA_BLOCK
    n = jnp.arange(2 * blk)
    max_exact = NUM_BUCKETS // 2
    nf = jnp.maximum(n, 1).astype(F32)
    large = max_exact + (jnp.log(nf / max_exact) / math.log(MAX_DISTANCE / max_exact)
                         * (NUM_BUCKETS - max_exact)).astype(I32)
    large = jnp.minimum(large, NUM_BUCKETS - 1)
    bucket = jnp.where(n < max_exact, n, large)
    rel_bias = rel_bias.astype(F32) * LOG2E
    tbl = rel_bias[bucket].T
    key = jnp.arange(blk)[:, None]
    qry = jnp.arange(blk)[None, :]
    d = qry - key
    h = tbl.shape[0]
    wide = 2 * blk + 1
    skew = jnp.broadcast_to(jnp.pad(tbl, ((0, 0), (0, 1)))[:, None, :], (h, blk, wide))
    skew = skew.reshape(h, blk * wide)[:, :blk * 2 * blk].reshape(h, blk, 2 * blk)
    bown = jnp.where(d >= 0, skew[:, :, :blk], MASKED)
    bprev = skew[:, :, blk:]
    bfar = rel_bias[NUM_BUCKETS - 1]
    return bown, bprev, bfar


def _mix_kernel(x_ref, attn_ref, u_ref, halo_ref, gl_ref, pw_ref, ps_ref, woa_ref, wop_ref,
                wout_ref, gffn_ref, rwt_ref, rb_ref,
                x1_ref, hf_ref, idx_ref, rank_ref, gate_ref, cnt_ref,
                ext_s, carry_s, *, tiles_per_seq):
    i = pl.program_id(0)
    tm = MIX_TM

    @pl.when(i == 0)
    def _():
        carry_s[...] = jnp.zeros_like(carry_s)

    first = (i % tiles_per_seq) == 0
    ext_s[0:POOL_HALO, :] = jnp.where(first, 0.0, halo_ref[...])
    ext_s[POOL_HALO:POOL_HALO + tm, :] = u_ref[...]
    pos = (i % tiles_per_seq) * tm + lax.broadcasted_iota(I32, (tm, 1), 0)
    pooled_parts = []
    for g, w in enumerate(POOL_WINDOWS):
        c0, c1 = g * POOL_GROUP_DIM, (g + 1) * POOL_GROUP_DIM
        win = ext_s[POOL_HALO:POOL_HALO + tm, c0:c1]
        for sft in range(1, w):
            win = win + ext_s[POOL_HALO - sft:POOL_HALO - sft + tm, c0:c1]
        cnt = jnp.minimum(pos + 1, w).astype(F32)
        pin = win / cnt - ext_s[POOL_HALO:POOL_HALO + tm, c0:c1]
        pooled_parts.append(jnp.dot(pin.astype(BF16), pw_ref[g], preferred_element_type=F32))
    pooled = jnp.concatenate(pooled_parts, axis=1) * ps_ref[...]

    a = lax.dot_general(attn_ref[0], woa_ref[...], (((0,), (0,)), ((), ())),
                        preferred_element_type=F32)
    pm = jnp.dot(pooled.astype(BF16), wop_ref[...], preferred_element_type=F32)
    g0 = _sigmoid(gl_ref[:, 0:D_MODEL].astype(F32))
    g1 = _sigmoid(gl_ref[:, D_MODEL:2 * D_MODEL].astype(F32))
    merged = g0 * a + g1 * pm
    x1 = x_ref[...] + jnp.dot(merged.astype(BF16), wout_ref[...], preferred_element_type=F32)
    x1_ref[...] = x1

    hf = _rms(x1, gffn_ref[...])
    hfb = hf.astype(BF16)
    for s in range(ROW_TILES):
        hf_ref[pl.ds(s, tm, stride=ROW_TILES), :] = hfb[:, s * LANES:(s + 1) * LANES].astype(F32)

    logits = lax.dot_general(rwt_ref[...], hfb.astype(F32), _NT,
                             precision=lax.Precision.HIGHEST,
                             preferred_element_type=F32) + rb_ref[...]
    e_iota = lax.broadcasted_iota(I32, logits.shape, 0)
    vals, picks = [], []
    for _ in range(TOP_K):
        mx = jnp.max(logits, axis=0, keepdims=True)
        idx = jnp.min(jnp.where(logits == mx, e_iota, N_EXPERTS), axis=0, keepdims=True)
        pick = e_iota == idx
        vals.append(mx)
        picks.append(pick)
        idx_ref[len(picks) - 1:len(picks), :] = idx
        logits = jnp.where(pick, -jnp.inf, logits)
    ex = [jnp.exp(v - vals[0]) for v in vals]
    den = ex[0] + ex[1] + ex[2] + ex[3]
    for k in range(TOP_K):
        gate_ref[k:k + 1, :] = ex[k] / den

    onehot = jnp.zeros(e_iota.shape, F32)
    for pick in picks:
        onehot = onehot + pick.astype(F32)
    ra = lax.broadcasted_iota(I32, (tm, tm), 0)
    rb = lax.broadcasted_iota(I32, (tm, tm), 1)
    upper = (ra < rb).astype(BF16)
    before = jnp.dot(onehot.astype(BF16), upper, preferred_element_type=F32) + carry_s[...]
    for k, pick in enumerate(picks):
        rank_ref[k:k + 1, :] = jnp.sum(jnp.where(pick, before, 0.0), axis=0,
                                       keepdims=True).astype(I32)
    carry_s[...] = carry_s[...] + jnp.sum(onehot, axis=1, keepdims=True)
    cnt_ref[...] = jnp.broadcast_to(carry_s[...], cnt_ref.shape).astype(I32)


def _mix(x2, attn, u, gl, pw, ps, woa, wop, wout, gffn, rwt, rb, seq):
    t = x2.shape[0]
    tm = MIX_TM
    tiles_per_seq = seq // tm
    halo_per_tile = tm // POOL_HALO
    row = lambda i: (i, 0)
    fixed2 = lambda i: (0, 0)
    fixed3 = lambda i: (0, 0, 0)
    col = lambda i: (0, i)
    kern = functools.partial(_mix_kernel, tiles_per_seq=tiles_per_seq)
    return pl.pallas_call(
        kern,
        grid=(t // tm,),
        in_specs=[
            pl.BlockSpec((tm, D_MODEL), row),
            pl.BlockSpec((1, ATTN_WIDTH, tm), lambda i: (i // tiles_per_seq, 0, i % tiles_per_seq)),
            pl.BlockSpec((tm, POOL_WIDTH), row),
            pl.BlockSpec((POOL_HALO, POOL_WIDTH),
                         lambda i: (jnp.maximum(i * halo_per_tile - 1, 0), 0)),
            pl.BlockSpec((tm, 2 * D_MODEL), row),
            pl.BlockSpec((POOL_GROUPS, POOL_GROUP_DIM, POOL_GROUP_DIM), fixed3),
            pl.BlockSpec((1, POOL_WIDTH), fixed2),
            pl.BlockSpec((ATTN_WIDTH, D_MODEL), fixed2),
            pl.BlockSpec((POOL_WIDTH, D_MODEL), fixed2),
            pl.BlockSpec((D_MODEL, D_MODEL), fixed2),
            pl.BlockSpec((1, D_MODEL), fixed2),
            pl.BlockSpec((N_EXPERTS, D_MODEL), fixed2),
            pl.BlockSpec((N_EXPERTS, 1), fixed2),
        ],
        out_specs=[
            pl.BlockSpec((tm, D_MODEL), row),
            pl.BlockSpec((tm * ROW_TILES, LANES), row),
            pl.BlockSpec((TOP_K, tm), col),
            pl.BlockSpec((TOP_K, tm), col),
            pl.BlockSpec((TOP_K, tm), col),
            pl.BlockSpec((N_EXPERTS, LANES), fixed2),
        ],
        out_shape=[
            jax.ShapeDtypeStruct((t, D_MODEL), F32),
            jax.ShapeDtypeStruct((t * ROW_TILES, LANES), F32),
            jax.ShapeDtypeStruct((TOP_K, t), I32),
            jax.ShapeDtypeStruct((TOP_K, t), I32),
            jax.ShapeDtypeStruct((TOP_K, t), F32),
            jax.ShapeDtypeStruct((N_EXPERTS, LANES), I32),
        ],
        scratch_shapes=[
            pltpu.VMEM((POOL_HALO + tm, POOL_WIDTH), F32),
            pltpu.VMEM((N_EXPERTS, 1), F32),
        ],
        compiler_params=pltpu.CompilerParams(
            dimension_semantics=("arbitrary",), vmem_limit_bytes=VMEM_LIMIT),
        name="mix_router",
    )(x2, attn, u, u, gl, pw, ps, woa, wop, wout, gffn, rwt, rb)


def _row(ref, r):
    return ref.at[pl.ds(pl.multiple_of(r * ROW_TILES, ROW_TILES), ROW_TILES)]


def _pair_split_matrix():
    n = 2 * LANES
    r = lax.broadcasted_iota(I32, (n, n), 0)
    c = lax.broadcasted_iota(I32, (n, n), 1)
    src = jnp.where(c < LANES, 2 * c, 2 * (c - LANES) + 1)
    return (r == src).astype(BF16)


def _expert_kernel(be_ref, nused_ref, dest_ref, first_ref, par_ref, next_ref, padlo_ref, padhi_ref,
                   hf_ref, wgu_hbm, wd_hbm, bgu_ref, bd_ref,
                   y_ref, xbuf, wgu_f, wd_f, wgu_s, wd_s, rowsrc_ref, sems, wsems):
    i = pl.program_id(0)
    tm = EXPERT_TM
    n_used = nused_ref[0]
    slot = i % GATHER_DEPTH
    ahead = GATHER_DEPTH - 1

    def start_rows(blk, to_slot, r):
        src = pl.multiple_of(rowsrc_ref[blk * tm + r], ROW_TILES)
        pltpu.make_async_copy(hf_ref.at[pl.ds(src, ROW_TILES)], _row(xbuf.at[to_slot], r),
                              sems.at[to_slot]).start()

    def wait_rows(at_slot):
        pltpu.make_async_copy(hf_ref.at[pl.ds(0, tm * ROW_TILES)], xbuf.at[at_slot],
                              sems.at[at_slot]).wait()

    def weight_copies(e, w_slot, start):
        rows = D_MODEL // WEIGHT_DMA_CHUNKS
        for c in range(WEIGHT_DMA_CHUNKS):
            sl = pl.ds(c * rows, rows)
            for src, dst in ((wgu_hbm, wgu_f), (wd_hbm, wd_f)):
                if start:
                    pltpu.async_copy(src.at[e, sl], dst.at[w_slot, sl], wsems.at[w_slot],
                                     priority=1)
                else:
                    pltpu.make_async_copy(src.at[e, sl], dst.at[w_slot, sl],
                                          wsems.at[w_slot]).wait()

    @pl.when(i == 0)
    def _():
        weight_copies(be_ref[0], 0, True)

        def zero_range(e, carry):
            def zero(r, c):
                rowsrc_ref[r] = 0
                return c
            lax.fori_loop(padlo_ref[e], padhi_ref[e], zero, 0)
            return carry
        lax.fori_loop(0, N_EXPERTS + 1, zero_range, 0)

        def invert(t0, carry):
            for u in range(INVERT_UNROLL):
                tok = t0 * INVERT_UNROLL + u
                for k in range(TOP_K):
                    rowsrc_ref[dest_ref[tok * TOP_K + k]] = tok * ROW_TILES
            return carry
        lax.fori_loop(0, dest_ref.shape[0] // (TOP_K * INVERT_UNROLL), invert, 0)

        def row(r, carry):
            for blk in range(ahead):
                start_rows(blk, blk, r)
            return carry
        lax.fori_loop(0, tm, row, 0)

    @pl.when(jnp.logical_and(first_ref[i] == 1, i < n_used))
    def _():
        par = par_ref[i]
        weight_copies(be_ref[i], par, False)

        @pl.when(next_ref[i] >= 0)
        def _():
            weight_copies(next_ref[i], 1 - par, True)

        split = _pair_split_matrix()
        for c in range(2 * D_MODEL // (2 * LANES)):
            lo, hi = c * 2 * LANES, (c + 1) * 2 * LANES
            wgu_s[:, lo:hi] = jnp.dot(wgu_f[par, :, lo:hi].astype(BF16), split,
                                      preferred_element_type=F32).astype(BF16)
        wd_s[...] = wd_f[par].astype(BF16)

    @pl.when(i >= n_used)
    def _():
        y_ref[...] = jnp.zeros_like(y_ref)

    @pl.when(jnp.logical_and(i >= n_used, i < n_used + ahead))
    def _():
        wait_rows(slot)

    @pl.when(i < n_used)
    def _():
        wait_rows(slot)
        xb = jnp.concatenate(
            [xbuf[slot, pl.ds(s, tm, stride=ROW_TILES), :] for s in range(ROW_TILES)],
            axis=1).astype(BF16)
        hid_parts = []
        n_chunks = D_MODEL // LANES
        assert n_chunks * GATHER_CHUNK == tm
        for c in range(n_chunks):
            lo, hi = c * 2 * LANES, (c + 1) * 2 * LANES
            gu = jnp.dot(xb, wgu_s[:, lo:hi], preferred_element_type=F32) + bgu_ref[0, :, lo:hi]
            glu = jnp.minimum(gu[:, :LANES], SWIGLU_LIMIT)
            lin = jnp.clip(gu[:, LANES:], -SWIGLU_LIMIT, SWIGLU_LIMIT)
            hid_parts.append(((lin + 1.0) * (glu * _sigmoid(SWIGLU_ALPHA * glu))).astype(BF16))
            for r in range(c * GATHER_CHUNK, (c + 1) * GATHER_CHUNK):
                start_rows(i + ahead, (i + ahead) % GATHER_DEPTH, r)
        hid = jnp.concatenate(hid_parts, axis=1)
        y = jnp.dot(hid, wd_s[...], preferred_element_type=F32) + bd_ref[0]
        for s in range(ROW_TILES):
            y_ref[pl.ds(s, tm, stride=ROW_TILES), :] = y[:, s * LANES:(s + 1) * LANES]


def _experts(blk_expert, n_used, dest, first, parity, next_expert, pad_lo, pad_hi, hf_rows,
             wgu, wd, bgu_split, bd):
    n_blk = blk_expert.shape[0]
    tm = EXPERT_TM
    assert dest.shape[0] % (TOP_K * INVERT_UNROLL) == 0

    def ymap(i, be, nu, *_):
        return (i, 0)

    def bmap(i, be, nu, *_):
        return (be[jnp.minimum(i, nu[0] - 1)], 0, 0)

    grid_spec = pltpu.PrefetchScalarGridSpec(
        num_scalar_prefetch=8,
        grid=(n_blk,),
        in_specs=[
            pl.BlockSpec(memory_space=pl.ANY),
            pl.BlockSpec(memory_space=pl.ANY),
            pl.BlockSpec(memory_space=pl.ANY),
            pl.BlockSpec((1, 1, 2 * D_MODEL), bmap),
            pl.BlockSpec((1, 1, D_MODEL), bmap),
        ],
        out_specs=pl.BlockSpec((tm * ROW_TILES, LANES), ymap),
        scratch_shapes=[
            pltpu.VMEM((GATHER_DEPTH, tm * ROW_TILES, LANES), F32),
            pltpu.VMEM((2, D_MODEL, 2 * D_MODEL), F32),
            pltpu.VMEM((2, D_MODEL, D_MODEL), F32),
            pltpu.VMEM((D_MODEL, 2 * D_MODEL), BF16),
            pltpu.VMEM((D_MODEL, D_MODEL), BF16),
            pltpu.SMEM((n_blk * tm,), I32),
            pltpu.SemaphoreType.DMA((GATHER_DEPTH,)),
            pltpu.SemaphoreType.DMA((2,)),
        ],
    )
    return pl.pallas_call(
        _expert_kernel,
        grid_spec=grid_spec,
        out_shape=jax.ShapeDtypeStruct((n_blk * tm * ROW_TILES, LANES), F32),
        compiler_params=pltpu.CompilerParams(
            dimension_semantics=("arbitrary",), vmem_limit_bytes=VMEM_LIMIT),
        name="expert_ffn",
    )(blk_expert, n_used, dest, first, parity, next_expert, pad_lo, pad_hi, hf_rows, wgu, wd,
      bgu_split, bd)


def _combine_kernel(segsrc_ref, seglen_ref, segdst_ref, pos_ref,
                    x1_ref, gate_ref, ys_ref, p_ref, gple_ref, wpg_ref, wpp_ref, gfin_ref,
                    o_ref, segbuf, ybuf, sems):
    i = pl.program_id(0)
    n_steps = pl.num_programs(0)
    tm = OUT_TM
    slot = i % 2

    def fetch(tile, to_slot):
        for e in range(N_EXPERTS):
            n = pl.multiple_of(seglen_ref[tile * N_EXPERTS + e], ROW_TILES)
            src = pl.multiple_of(segsrc_ref[tile * N_EXPERTS + e], ROW_TILES)
            dst = pl.multiple_of(segdst_ref[tile * N_EXPERTS + e], ROW_TILES)

            @pl.when(n > 0)
            def _():
                pltpu.make_async_copy(ys_ref.at[pl.ds(src, n)],
                                      segbuf.at[to_slot, pl.ds(dst, n)], sems.at[to_slot]).start()

    @pl.when(i == 0)
    def _():
        fetch(0, 0)

    @pl.when(i + 1 < n_steps)
    def _():
        fetch(i + 1, 1 - slot)

    pltpu.make_async_copy(ys_ref.at[pl.ds(0, TOP_K * tm * ROW_TILES)], segbuf.at[slot],
                          sems.at[slot]).wait()

    for r in range(tm):
        for k in range(TOP_K):
            p = pl.multiple_of(pos_ref[(i * tm + r) * TOP_K + k], ROW_TILES)
            ybuf[(k * tm + r) * ROW_TILES:(k * tm + r + 1) * ROW_TILES, :] = (
                segbuf[slot, pl.ds(p, ROW_TILES), :])

    moe = None
    for k in range(TOP_K):
        yk = jnp.concatenate(
            [ybuf[pl.ds(k * tm * ROW_TILES + s, tm, stride=ROW_TILES), :]
             for s in range(ROW_TILES)], axis=1)
        term = gate_ref[:, k:k + 1] * yk
        moe = term if moe is None else moe + term
    x2 = x1_ref[...] + moe
    hp = _rms(x2, gple_ref[...]).astype(BF16)
    pg = _sigmoid(jnp.dot(hp, wpg_ref[...], preferred_element_type=F32))
    proj = jnp.dot(p_ref[...].astype(BF16), wpp_ref[...], preferred_element_type=F32)
    x3 = x2 + pg * proj
    o_ref[...] = _rms(x3, gfin_ref[...])


def _combine(seg_src, seg_len, seg_dst, pos, x1, gates_tk, ys, p2, gple, wpg, wpp, gfin):
    t = x1.shape[0]
    tm = OUT_TM
    ple = p2.shape[1]
    row = lambda i, *_: (i, 0)
    fixed = lambda i, *_: (0, 0)
    grid_spec = pltpu.PrefetchScalarGridSpec(
        num_scalar_prefetch=4,
        grid=(t // tm,),
        in_specs=[
            pl.BlockSpec((tm, D_MODEL), row),
            pl.BlockSpec((tm, TOP_K), row),
            pl.BlockSpec(memory_space=pl.ANY),
            pl.BlockSpec((tm, ple), row),
            pl.BlockSpec((1, D_MODEL), fixed),
            pl.BlockSpec((D_MODEL, D_MODEL), fixed),
            pl.BlockSpec((ple, D_MODEL), fixed),
            pl.BlockSpec((1, D_MODEL), fixed),
        ],
        out_specs=pl.BlockSpec((tm, D_MODEL), row),
        scratch_shapes=[
            pltpu.VMEM((2, TOP_K * tm * ROW_TILES, LANES), F32),
            pltpu.VMEM((TOP_K * tm * ROW_TILES, LANES), F32),
            pltpu.SemaphoreType.DMA((2,)),
        ],
    )
    return pl.pallas_call(
        _combine_kernel,
        grid_spec=grid_spec,
        out_shape=jax.ShapeDtypeStruct((t, D_MODEL), F32),
        compiler_params=pltpu.CompilerParams(
            dimension_semantics=("arbitrary",), vmem_limit_bytes=VMEM_LIMIT),
        name="combine_ple_final",
    )(seg_src, seg_len, seg_dst, pos, x1, gates_tk, ys, p2, gple, wpg, wpp, gfin)


def kernel(x, p, rel_bias, norm_mix_g, w_in, pool_w, pool_scale, w_o_attn, w_o_pool, w_out,
           norm_ffn_g, router_w, router_b, w_gate_up, b_gate_up, w_down, b_down,
           norm_ple_g, w_ple_gate, w_ple_proj, norm_final_g):
    b, s, d = x.shape
    depth = w_in.shape[0]
    t = b * s
    assert d == D_MODEL and s % MOBA_BLOCK == 0 and t % IN_TM == 0 and s % MIX_TM == 0
    assert depth == 1, "the final norm is fused into the layer's last kernel"
    n_pad = t * TOP_K + (N_EXPERTS + GATHER_DEPTH - 2) * EXPERT_TM
    n_blk = n_pad // EXPERT_TM
    bown, bprev, bfar = _t5_bias_tables(rel_bias)

    x2 = x.reshape(t, d)
    for i in range(depth):
        w_in_b = w_in[i].astype(BF16)
        wvt = w_in[i][:, 2 * ATTN_WIDTH:3 * ATTN_WIDTH].T.astype(BF16)
        q, k, vt, u, gl = _in_proj(x2, norm_mix_g[i].reshape(1, d), w_in_b, wvt, b, s)
        bh = b * N_HEADS
        attn_t = _attention(q.reshape(bh, s, HEAD_DIM), k.reshape(bh, s, HEAD_DIM),
                            vt.reshape(bh, HEAD_DIM, s), bown, bprev, bfar)
        attn = attn_t.reshape(b, ATTN_WIDTH, s)

        x1, hf_rows, idx_kt, rank_kt, gate_kt, cnt = _mix(
            x2, attn, u, gl, pool_w[i].astype(BF16), pool_scale[i].reshape(1, POOL_WIDTH),
            w_o_attn[i].astype(BF16), w_o_pool[i].astype(BF16), w_out[i].astype(BF16),
            norm_ffn_g[i].reshape(1, d), router_w[i].T, router_b[i].reshape(N_EXPERTS, 1), s)

        counts = cnt[:, 0]
        padded = (counts + EXPERT_TM - 1) // EXPERT_TM * EXPERT_TM
        pend = jnp.cumsum(padded)
        pstart = pend - padded
        e_ids = jnp.arange(N_EXPERTS, dtype=I32)
        pstart_of = jnp.sum(jnp.where(idx_kt[..., None] == e_ids, pstart, 0), axis=-1)
        dest_flat = (pstart_of + rank_kt).T.reshape(-1).astype(I32)
        blk_row0 = jnp.arange(n_blk, dtype=I32) * EXPERT_TM
        blk_expert = jnp.minimum(jnp.sum(pend[None, :] <= blk_row0[:, None], axis=1),
                                 N_EXPERTS - 1).astype(I32)
        n_used = (pend[-1:] // EXPERT_TM).astype(I32)
        pad_lo = jnp.concatenate([pstart + counts, pend[-1:]]).astype(I32)
        pad_hi = jnp.concatenate([pend, pend[-1:] + (GATHER_DEPTH - 1) * EXPERT_TM]).astype(I32)

        bgu = b_gate_up[i].reshape(N_EXPERTS, D_MODEL // LANES, LANES, 2).transpose(
            0, 1, 3, 2).reshape(N_EXPERTS, 1, 2 * D_MODEL)
        first = jnp.concatenate([jnp.ones((1,), I32),
                                 (blk_expert[1:] != blk_expert[:-1]).astype(I32)])
        parity = ((jnp.cumsum(first) - 1) % 2).astype(I32)
        later = jnp.logical_and(e_ids[None, :] > e_ids[:, None], (padded > 0)[None, :])
        next_e = jnp.min(jnp.where(later, e_ids[None, :], N_EXPERTS), axis=1)
        next_e = jnp.where(next_e == N_EXPERTS, -1, next_e).astype(I32)
        next_expert = jnp.sum(jnp.where(blk_expert[:, None] == e_ids, next_e, 0),
                              axis=1).astype(I32)
        ys = _experts(blk_expert, n_used, dest_flat, first, parity, next_expert, pad_lo, pad_hi,
                      hf_rows, w_gate_up[i], w_down[i], bgu, b_down[i][:, None, :])

        n_tiles = t // OUT_TM
        tile_cnt = jnp.sum(idx_kt.T.reshape(n_tiles, OUT_TM * TOP_K)[..., None] == e_ids,
                           axis=1).astype(I32)
        before = jnp.cumsum(tile_cnt, axis=0) - tile_cnt
        local_off = jnp.cumsum(tile_cnt, axis=1) - tile_cnt
        seg_src = ((pstart[None, :] + before) * ROW_TILES).reshape(-1).astype(I32)
        seg_len = (tile_cnt * ROW_TILES).reshape(-1).astype(I32)
        seg_dst = (local_off * ROW_TILES).reshape(-1).astype(I32)
        shift = jnp.repeat(local_off - before, OUT_TM, axis=0)
        pos = jnp.sum(jnp.where(idx_kt[..., None] == e_ids, shift[None], 0), axis=-1) + rank_kt
        pos = (pos.T.reshape(-1) * ROW_TILES).astype(I32)
        x2 = _combine(seg_src, seg_len, seg_dst, pos, x1, gate_kt.T, ys, p[i].reshape(t, -1),
                      norm_ple_g[i].reshape(1, d), w_ple_gate[i].astype(BF16),
                      w_ple_proj[i].astype(BF16), norm_final_g.reshape(1, d))
    return x2.reshape(b, s, d)
```

```python
import functools
import math

import jax
import jax.numpy as jnp
from jax import lax
from jax.experimental import pallas as pl
from jax.experimental.pallas import tpu as pltpu

F32 = jnp.float32
BF16 = jnp.bfloat16
I32 = jnp.int32

D_MODEL = 1024
N_HEADS = 8
HEAD_DIM = 64
ATTN_WIDTH = N_HEADS * HEAD_DIM
MOBA_BLOCK = 256
MOBA_TOPK = 3
NUM_BUCKETS = 32
MAX_DISTANCE = 128
POOL_WINDOWS = (2, 4, 8, 16)
POOL_GROUPS = 4
POOL_GROUP_DIM = 128
POOL_WIDTH = POOL_GROUPS * POOL_GROUP_DIM
N_EXPERTS = 32
TOP_K = 4
SWIGLU_LIMIT = 7.0
SWIGLU_ALPHA = 1.702
RMS_EPS = 1e-6

LANES = 128
SUBLANES = 8
BF16_SUBLANES = 16
ROW_TILES = D_MODEL // LANES
V_ROWS = HEAD_DIM + BF16_SUBLANES

IN_TM = 512
MIX_TM = 512
EXPERT_TM = 256
OUT_TM = 256
ATTN_GROUP = 4
ATTN_HEADS = 4
GATHER_CHUNK = EXPERT_TM // (D_MODEL // LANES)
WEIGHT_DMA_CHUNKS = 4
INVERT_UNROLL = 4
GATHER_DEPTH = 4
POOL_HALO = 16
MASKED = -1e30
LOG2E = math.log2(math.e)
Q_SCALE = HEAD_DIM ** -0.5 * LOG2E

VMEM_LIMIT = 56 * 1024 * 1024

_NT = (((1,), (1,)), ((), ()))


def _rms(x, g):
    ms = jnp.mean(x * x, axis=-1, keepdims=True)
    return x * lax.rsqrt(ms + RMS_EPS) * g


def _sigmoid(x):
    return 1.0 / (1.0 + jnp.exp(-x))


def _in_proj_kernel(x_ref, g_ref, w_ref, wvt_ref, q_ref, k_ref, vt_ref, u_ref, gl_ref):
    h = _rms(x_ref[...], g_ref[...]).astype(BF16)
    aw = ATTN_WIDTH

    def proj(lo, hi):
        return jnp.dot(h, w_ref[:, lo:hi], preferred_element_type=F32)

    qv = (proj(0, aw) * Q_SCALE).astype(BF16)
    kv = proj(aw, 2 * aw).astype(BF16)
    for hd in range(N_HEADS):
        q_ref[0, hd] = qv[:, hd * HEAD_DIM:(hd + 1) * HEAD_DIM]
        k_ref[0, hd] = kv[:, hd * HEAD_DIM:(hd + 1) * HEAD_DIM]
    vt = lax.dot_general(wvt_ref[...], h, _NT, preferred_element_type=F32).astype(BF16)
    tm = vt.shape[1]
    vt_ref[0, :, 0:HEAD_DIM, :] = vt.reshape(N_HEADS, HEAD_DIM, tm)
    extra = lax.broadcasted_iota(I32, (N_HEADS, V_ROWS - HEAD_DIM, tm), 1) == 0
    vt_ref[0, :, HEAD_DIM:V_ROWS, :] = extra.astype(BF16)
    u_ref[...] = proj(3 * aw, 3 * aw + POOL_WIDTH)
    base = 3 * aw + POOL_WIDTH
    for c in range(2):
        lo = base + c * D_MODEL
        gl_ref[:, c * D_MODEL:(c + 1) * D_MODEL] = proj(lo, lo + D_MODEL).astype(BF16)


def _in_proj(x2, g, w_bf16, wvt_bf16, batch, seq):
    t = x2.shape[0]
    in_cols = w_bf16.shape[1]
    tps = seq // IN_TM
    row = lambda i: (i, 0)
    fixed = lambda i: (0, 0)
    head_rows = lambda i: (i // tps, 0, i % tps, 0)
    head_cols = lambda i: (i // tps, 0, 0, i % tps)
    return pl.pallas_call(
        _in_proj_kernel,
        grid=(t // IN_TM,),
        in_specs=[
            pl.BlockSpec((IN_TM, D_MODEL), row),
            pl.BlockSpec((1, D_MODEL), fixed),
            pl.BlockSpec((D_MODEL, in_cols), fixed),
            pl.BlockSpec((ATTN_WIDTH, D_MODEL), fixed),
        ],
        out_specs=[
            pl.BlockSpec((1, N_HEADS, IN_TM, HEAD_DIM), head_rows),
            pl.BlockSpec((1, N_HEADS, IN_TM, HEAD_DIM), head_rows),
            pl.BlockSpec((1, N_HEADS, V_ROWS, IN_TM), head_cols),
            pl.BlockSpec((IN_TM, POOL_WIDTH), row),
            pl.BlockSpec((IN_TM, 2 * D_MODEL), row),
        ],
        out_shape=[
            jax.ShapeDtypeStruct((batch, N_HEADS, seq, HEAD_DIM), BF16),
            jax.ShapeDtypeStruct((batch, N_HEADS, seq, HEAD_DIM), BF16),
            jax.ShapeDtypeStruct((batch, N_HEADS, V_ROWS, seq), BF16),
            jax.ShapeDtypeStruct((t, POOL_WIDTH), F32),
            jax.ShapeDtypeStruct((t, 2 * D_MODEL), BF16),
        ],
        compiler_params=pltpu.CompilerParams(
            dimension_semantics=("parallel",), vmem_limit_bytes=VMEM_LIMIT),
        name="in_proj",
    )(x2, g, w_bf16, wvt_bf16)


def _attn_kernel(bfar_ref, q_ref, k_ref, vt_ref, bown_ref, bprev_ref, o_ref,
                 kmean_s, am_s, sa_s, sb_s, ga_s, gb_s, m_s, acc_s, *, n_blocks):
    hp = pl.program_id(0)
    j = pl.program_id(1)
    blk = MOBA_BLOCK
    grp = ATTN_GROUP
    gk = grp * blk
    n_groups = n_blocks // grp
    n_far = jnp.maximum(j - 1, 0)
    ng = (n_far + grp - 1) // grp
    jp = jnp.maximum(j - 1, 0)
    heads = range(ATTN_HEADS)

    @pl.when(j == 0)
    def _():
        for hh in heads:
            for n in range(n_blocks):
                kb = k_ref[hh, n * blk:(n + 1) * blk, :].astype(F32)
                kmean_s[hh, n:n + 1, :] = jnp.mean(kb, axis=0, keepdims=True)

    qs = [q_ref[hh] for hh in heads]

    def select(hh):
        gate = lax.dot_general(kmean_s[hh], qs[hh].astype(F32), _NT,
                               precision=lax.Precision.HIGHEST, preferred_element_type=F32)
        n_iota = lax.broadcasted_iota(I32, gate.shape, 0)
        past = n_iota < j
        gate = jnp.where(past, gate, jnp.finfo(F32).min)
        sel = jnp.zeros(gate.shape, jnp.bool_)
        for _ in range(MOBA_TOPK):
            mx = jnp.max(gate, axis=0, keepdims=True)
            idx = jnp.min(jnp.where(gate == mx, n_iota, n_blocks), axis=0, keepdims=True)
            pick = n_iota == idx
            sel = jnp.logical_or(sel, pick)
            gate = jnp.where(pick, -jnp.inf, gate)
        sel = jnp.logical_and(sel, past)
        bfar = bfar_ref[(hp * ATTN_HEADS + hh) % N_HEADS]
        am_s[hh] = jnp.where(jnp.logical_and(sel, n_iota < j - 1), bfar, MASKED)
        return jnp.max(jnp.where(jnp.logical_and(sel, n_iota == j - 1), 0.0, MASKED),
                       axis=0, keepdims=True)

    def produce(hh, g, s_ref, gmax_ref):
        g = jnp.minimum(g, n_groups - 1)
        kb = k_ref[hh, pl.ds(pl.multiple_of(g * gk, gk), gk), :]
        s = lax.dot_general(kb, qs[hh], _NT, preferred_element_type=F32)
        s_ref[hh] = s
        gmax = None
        for b in range(grp):
            mb = (jnp.max(s[b * blk:(b + 1) * blk], axis=0, keepdims=True)
                  + am_s[hh, pl.ds(g * grp + b, 1), :])
            gmax = mb if gmax is None else jnp.maximum(gmax, mb)
        gmax_ref[hh] = gmax

    def consume(hh, g, s_ref, gmax_ref):
        m_old = m_s[hh]
        m_new = jnp.maximum(m_old, gmax_ref[hh])
        alpha = jnp.exp2(m_old - m_new)
        parts = []
        for b in range(grp):
            shift = m_new - am_s[hh, pl.ds(g * grp + b, 1), :]
            pb = jnp.exp2(s_ref[hh, b * blk:(b + 1) * blk, :] - shift)
            parts.append(pb.astype(BF16))
        p = jnp.concatenate(parts, axis=0)
        vt = vt_ref[hh, :, pl.ds(pl.multiple_of(g * gk, gk), gk)]
        pv = jnp.dot(vt, p, preferred_element_type=F32)
        m_s[hh] = m_new
        acc_s[hh] = alpha * acc_s[hh] + pv

    def near_scores(hh, n, bias):
        kb = k_ref[hh, pl.ds(pl.multiple_of(n * blk, blk), blk), :]
        return lax.dot_general(kb, qs[hh], _NT, preferred_element_type=F32) + bias

    def near_pv(hh, n, p):
        vt = vt_ref[hh, :, pl.ds(pl.multiple_of(n * blk, blk), blk)]
        return jnp.dot(vt, p.astype(BF16), preferred_element_type=F32)

    for hh in heads:
        am_prev = select(hh)
        produce(hh, 0, sa_s, ga_s)
        s_own = near_scores(hh, j, bown_ref[hh])
        s_prev = near_scores(hh, jp, bprev_ref[hh])
        m = jnp.maximum(jnp.max(s_own, axis=0, keepdims=True),
                        jnp.max(s_prev, axis=0, keepdims=True) + am_prev)
        p_own = jnp.exp2(s_own - m)
        p_prev = jnp.exp2(s_prev - (m - am_prev))
        m_s[hh] = m
        acc_s[hh] = near_pv(hh, j, p_own) + near_pv(hh, jp, p_prev)

    def pair(h, carry):
        for hh in heads:
            consume(hh, 2 * h, sa_s, ga_s)
            produce(hh, 2 * h + 1, sb_s, gb_s)
        for hh in heads:
            consume(hh, 2 * h + 1, sb_s, gb_s)
            produce(hh, 2 * h + 2, sa_s, ga_s)
        return carry

    lax.fori_loop(0, ng // 2, pair, 0)

    @pl.when(ng % 2 == 1)
    def _():
        for hh in heads:
            consume(hh, ng - 1, sa_s, ga_s)

    for hh in heads:
        acc = acc_s[hh]
        o_ref[hh] = (acc[0:HEAD_DIM] / acc[HEAD_DIM:HEAD_DIM + 1]).astype(o_ref.dtype)


def _attention(qh, kh, vth, bown, bprev, bfar):
    bhn, s, dh = qh.shape
    nb = s // MOBA_BLOCK
    ah = ATTN_HEADS
    assert N_HEADS % ah == 0 and nb % ATTN_GROUP == 0
    kern = functools.partial(_attn_kernel, n_blocks=nb)
    head_blk = lambda b, j: (b % (N_HEADS // ah), 0, 0)
    grid_spec = pltpu.PrefetchScalarGridSpec(
        num_scalar_prefetch=0,
        grid=(bhn // ah, nb),
        in_specs=[
            pl.BlockSpec(memory_space=pltpu.SMEM),
            pl.BlockSpec((ah, MOBA_BLOCK, dh), lambda b, j: (b, j, 0)),
            pl.BlockSpec((ah, s, dh), lambda b, j: (b, 0, 0)),
            pl.BlockSpec((ah, V_ROWS, s), lambda b, j: (b, 0, 0)),
            pl.BlockSpec((ah, MOBA_BLOCK, MOBA_BLOCK), head_blk),
            pl.BlockSpec((ah, MOBA_BLOCK, MOBA_BLOCK), head_blk),
        ],
        out_specs=pl.BlockSpec((ah, dh, MOBA_BLOCK), lambda b, j: (b, 0, j)),
        scratch_shapes=[
            pltpu.VMEM((ah, nb, dh), F32),
            pltpu.VMEM((ah, nb, MOBA_BLOCK), F32),
            pltpu.VMEM((ah, ATTN_GROUP * MOBA_BLOCK, MOBA_BLOCK), F32),
            pltpu.VMEM((ah, ATTN_GROUP * MOBA_BLOCK, MOBA_BLOCK), F32),
            pltpu.VMEM((ah, 1, MOBA_BLOCK), F32),
            pltpu.VMEM((ah, 1, MOBA_BLOCK), F32),
            pltpu.VMEM((ah, 1, MOBA_BLOCK), F32),
            pltpu.VMEM((ah, V_ROWS, MOBA_BLOCK), F32),
        ],
    )
    return pl.pallas_call(
        kern,
        grid_spec=grid_spec,
        out_shape=jax.ShapeDtypeStruct((bhn, dh, s), BF16),
        compiler_params=pltpu.CompilerParams(
            dimension_semantics=("parallel", "arbitrary"), vmem_limit_bytes=VMEM_LIMIT),
        name="moba_attention",
    )(bfar, qh, kh, vth, bown, bprev)


def _t5_bias_tables(rel_bias):
    blk = MOBA_BLOCK
    n = jnp.arange(2 * blk)
    max_exact = NUM_BUCKETS // 2
    nf = jnp.maximum(n, 1).astype(F32)
    large = max_exact + (jnp.log(nf / max_exact) / math.log(MAX_DISTANCE / max_exact)
                         * (NUM_BUCKETS - max_exact)).astype(I32)
    large = jnp.minimum(large, NUM_BUCKETS - 1)
    bucket = jnp.where(n < max_exact, n, large)
    rel_bias = rel_bias.astype(F32) * LOG2E
    tbl = rel_bias[bucket].T
    key = jnp.arange(blk)[:, None]
    qry = jnp.arange(blk)[None, :]
    d = qry - key
    h = tbl.shape[0]
    wide = 2 * blk + 1
    skew = jnp.broadcast_to(jnp.pad(tbl, ((0, 0), (0, 1)))[:, None, :], (h, blk, wide))
    skew = skew.reshape(h, blk * wide)[:, :blk * 2 * blk].reshape(h, blk, 2 * blk)
    bown = jnp.where(d >= 0, skew[:, :, :blk], MASKED)
    bprev = skew[:, :, blk:]
    bfar = rel_bias[NUM_BUCKETS - 1]
    return bown, bprev, bfar


def _mix_kernel(x_ref, attn_ref, u_ref, halo_ref, gl_ref, pw_ref, ps_ref, woa_ref, wop_ref,
                wout_ref, gffn_ref, rwt_ref, rb_ref,
                x1_ref, hf_ref, idx_ref, rank_ref, gate_ref, cnt_ref,
                ext_s, carry_s, *, tiles_per_seq):
    i = pl.program_id(0)
    tm = MIX_TM

    @pl.when(i == 0)
    def _():
        carry_s[...] = jnp.zeros_like(carry_s)

    first = (i % tiles_per_seq) == 0
    ext_s[0:POOL_HALO, :] = jnp.where(first, 0.0, halo_ref[...])
    ext_s[POOL_HALO:POOL_HALO + tm, :] = u_ref[...]
    pos = (i % tiles_per_seq) * tm + lax.broadcasted_iota(I32, (tm, 1), 0)
    pooled_parts = []
    for g, w in enumerate(POOL_WINDOWS):
        c0, c1 = g * POOL_GROUP_DIM, (g + 1) * POOL_GROUP_DIM
        win = ext_s[POOL_HALO:POOL_HALO + tm, c0:c1]
        for sft in range(1, w):
            win = win + ext_s[POOL_HALO - sft:POOL_HALO - sft + tm, c0:c1]
        cnt = jnp.minimum(pos + 1, w).astype(F32)
        pin = win / cnt - ext_s[POOL_HALO:POOL_HALO + tm, c0:c1]
        pooled_parts.append(jnp.dot(pin.astype(BF16), pw_ref[g], preferred_element_type=F32))
    pooled = jnp.concatenate(pooled_parts, axis=1) * ps_ref[...]

    a = lax.dot_general(attn_ref[0], woa_ref[...], (((0,), (0,)), ((), ())),
                        preferred_element_type=F32)
    pm = jnp.dot(pooled.astype(BF16), wop_ref[...], preferred_element_type=F32)
    g0 = _sigmoid(gl_ref[:, 0:D_MODEL].astype(F32))
    g1 = _sigmoid(gl_ref[:, D_MODEL:2 * D_MODEL].astype(F32))
    merged = g0 * a + g1 * pm
    x1 = x_ref[...] + jnp.dot(merged.astype(BF16), wout_ref[...], preferred_element_type=F32)
    x1_ref[...] = x1

    hf = _rms(x1, gffn_ref[...])
    hfb = hf.astype(BF16)
    for s in range(ROW_TILES):
        hf_ref[pl.ds(s, tm, stride=ROW_TILES), :] = hfb[:, s * LANES:(s + 1) * LANES].astype(F32)

    logits = lax.dot_general(rwt_ref[...], hfb.astype(F32), _NT,
                             precision=lax.Precision.HIGHEST,
                             preferred_element_type=F32) + rb_ref[...]
    e_iota = lax.broadcasted_iota(I32, logits.shape, 0)
    vals, picks = [], []
    for _ in range(TOP_K):
        mx = jnp.max(logits, axis=0, keepdims=True)
        idx = jnp.min(jnp.where(logits == mx, e_iota, N_EXPERTS), axis=0, keepdims=True)
        pick = e_iota == idx
        vals.append(mx)
        picks.append(pick)
        idx_ref[len(picks) - 1:len(picks), :] = idx
        logits = jnp.where(pick, -jnp.inf, logits)
    ex = [jnp.exp(v - vals[0]) for v in vals]
    den = ex[0] + ex[1] + ex[2] + ex[3]
    for k in range(TOP_K):
        gate_ref[k:k + 1, :] = ex[k] / den

    onehot = jnp.zeros(e_iota.shape, F32)
    for pick in picks:
        onehot = onehot + pick.astype(F32)
    ra = lax.broadcasted_iota(I32, (tm, tm), 0)
    rb = lax.broadcasted_iota(I32, (tm, tm), 1)
    upper = (ra < rb).astype(BF16)
    before = jnp.dot(onehot.astype(BF16), upper, preferred_element_type=F32) + carry_s[...]
    for k, pick in enumerate(picks):
        rank_ref[k:k + 1, :] = jnp.sum(jnp.where(pick, before, 0.0), axis=0,
                                       keepdims=True).astype(I32)
    carry_s[...] = carry_s[...] + jnp.sum(onehot, axis=1, keepdims=True)
    cnt_ref[...] = jnp.broadcast_to(carry_s[...], cnt_ref.shape).astype(I32)


def _mix(x2, attn, u, gl, pw, ps, woa, wop, wout, gffn, rwt, rb, seq):
    t = x2.shape[0]
    tm = MIX_TM
    tiles_per_seq = seq // tm
    halo_per_tile = tm // POOL_HALO
    row = lambda i: (i, 0)
    fixed2 = lambda i: (0, 0)
    fixed3 = lambda i: (0, 0, 0)
    col = lambda i: (0, i)
    kern = functools.partial(_mix_kernel, tiles_per_seq=tiles_per_seq)
    return pl.pallas_call(
        kern,
        grid=(t // tm,),
        in_specs=[
            pl.BlockSpec((tm, D_MODEL), row),
            pl.BlockSpec((1, ATTN_WIDTH, tm), lambda i: (i // tiles_per_seq, 0, i % tiles_per_seq)),
            pl.BlockSpec((tm, POOL_WIDTH), row),
            pl.BlockSpec((POOL_HALO, POOL_WIDTH),
                         lambda i: (jnp.maximum(i * halo_per_tile - 1, 0), 0)),
            pl.BlockSpec((tm, 2 * D_MODEL), row),
            pl.BlockSpec((POOL_GROUPS, POOL_GROUP_DIM, POOL_GROUP_DIM), fixed3),
            pl.BlockSpec((1, POOL_WIDTH), fixed2),
            pl.BlockSpec((ATTN_WIDTH, D_MODEL), fixed2),
            pl.BlockSpec((POOL_WIDTH, D_MODEL), fixed2),
            pl.BlockSpec((D_MODEL, D_MODEL), fixed2),
            pl.BlockSpec((1, D_MODEL), fixed2),
            pl.BlockSpec((N_EXPERTS, D_MODEL), fixed2),
            pl.BlockSpec((N_EXPERTS, 1), fixed2),
        ],
        out_specs=[
            pl.BlockSpec((tm, D_MODEL), row),
            pl.BlockSpec((tm * ROW_TILES, LANES), row),
            pl.BlockSpec((TOP_K, tm), col),
            pl.BlockSpec((TOP_K, tm), col),
            pl.BlockSpec((TOP_K, tm), col),
            pl.BlockSpec((N_EXPERTS, LANES), fixed2),
        ],
        out_shape=[
            jax.ShapeDtypeStruct((t, D_MODEL), F32),
            jax.ShapeDtypeStruct((t * ROW_TILES, LANES), F32),
            jax.ShapeDtypeStruct((TOP_K, t), I32),
            jax.ShapeDtypeStruct((TOP_K, t), I32),
            jax.ShapeDtypeStruct((TOP_K, t), F32),
            jax.ShapeDtypeStruct((N_EXPERTS, LANES), I32),
        ],
        scratch_shapes=[
            pltpu.VMEM((POOL_HALO + tm, POOL_WIDTH), F32),
            pltpu.VMEM((N_EXPERTS, 1), F32),
        ],
        compiler_params=pltpu.CompilerParams(
            dimension_semantics=("arbitrary",), vmem_limit_bytes=VMEM_LIMIT),
        name="mix_router",
    )(x2, attn, u, u, gl, pw, ps, woa, wop, wout, gffn, rwt, rb)


def _row(ref, r):
    return ref.at[pl.ds(pl.multiple_of(r * ROW_TILES, ROW_TILES), ROW_TILES)]


def _pair_split_matrix():
    n = 2 * LANES
    r = lax.broadcasted_iota(I32, (n, n), 0)
    c = lax.broadcasted_iota(I32, (n, n), 1)
    src = jnp.where(c < LANES, 2 * c, 2 * (c - LANES) + 1)
    return (r == src).astype(BF16)


def _expert_kernel(be_ref, nused_ref, dest_ref, first_ref, par_ref, next_ref, padlo_ref, padhi_ref,
                   hf_ref, wgu_hbm, wd_hbm, bgu_ref, bd_ref,
                   y_ref, xbuf, wgu_f, wd_f, wgu_s, wd_s, rowsrc_ref, sems, wsems):
    i = pl.program_id(0)
    tm = EXPERT_TM
    n_used = nused_ref[0]
    slot = i % GATHER_DEPTH
    ahead = GATHER_DEPTH - 1

    def start_rows(blk, to_slot, r):
        src = pl.multiple_of(rowsrc_ref[blk * tm + r], ROW_TILES)
        pltpu.make_async_copy(hf_ref.at[pl.ds(src, ROW_TILES)], _row(xbuf.at[to_slot], r),
                              sems.at[to_slot]).start()

    def wait_rows(at_slot):
        pltpu.make_async_copy(hf_ref.at[pl.ds(0, tm * ROW_TILES)], xbuf.at[at_slot],
                              sems.at[at_slot]).wait()

    def weight_copies(e, w_slot, start):
        rows = D_MODEL // WEIGHT_DMA_CHUNKS
        for c in range(WEIGHT_DMA_CHUNKS):
            sl = pl.ds(c * rows, rows)
            for src, dst in ((wgu_hbm, wgu_f), (wd_hbm, wd_f)):
                if start:
                    pltpu.async_copy(src.at[e, sl], dst.at[w_slot, sl], wsems.at[w_slot],
                                     priority=1)
                else:
                    pltpu.make_async_copy(src.at[e, sl], dst.at[w_slot, sl],
                                          wsems.at[w_slot]).wait()

    @pl.when(i == 0)
    def _():
        weight_copies(be_ref[0], 0, True)

        def zero_range(e, carry):
            def zero(r, c):
                rowsrc_ref[r] = 0
                return c
            lax.fori_loop(padlo_ref[e], padhi_ref[e], zero, 0)
            return carry
        lax.fori_loop(0, N_EXPERTS + 1, zero_range, 0)

        def invert(t0, carry):
            for u in range(INVERT_UNROLL):
                tok = t0 * INVERT_UNROLL + u
                for k in range(TOP_K):
                    rowsrc_ref[dest_ref[tok * TOP_K + k]] = tok * ROW_TILES
            return carry
        lax.fori_loop(0, dest_ref.shape[0] // (TOP_K * INVERT_UNROLL), invert, 0)

        def row(r, carry):
            for blk in range(ahead):
                start_rows(blk, blk, r)
            return carry
        lax.fori_loop(0, tm, row, 0)

    @pl.when(jnp.logical_and(first_ref[i] == 1, i < n_used))
    def _():
        par = par_ref[i]
        weight_copies(be_ref[i], par, False)

        @pl.when(next_ref[i] >= 0)
        def _():
            weight_copies(next_ref[i], 1 - par, True)

        split = _pair_split_matrix()
        for c in range(2 * D_MODEL // (2 * LANES)):
            lo, hi = c * 2 * LANES, (c + 1) * 2 * LANES
            wgu_s[:, lo:hi] = jnp.dot(wgu_f[par, :, lo:hi].astype(BF16), split,
                                      preferred_element_type=F32).astype(BF16)
        wd_s[...] = wd_f[par].astype(BF16)

    @pl.when(i >= n_used)
    def _():
        y_ref[...] = jnp.zeros_like(y_ref)

    @pl.when(jnp.logical_and(i >= n_used, i < n_used + ahead))
    def _():
        wait_rows(slot)

    @pl.when(i < n_used)
    def _():
        wait_rows(slot)
        xb = jnp.concatenate(
            [xbuf[slot, pl.ds(s, tm, stride=ROW_TILES), :] for s in range(ROW_TILES)],
            axis=1).astype(BF16)
        hid_parts = []
        n_chunks = D_MODEL // LANES
        assert n_chunks * GATHER_CHUNK == tm
        for c in range(n_chunks):
            lo, hi = c * 2 * LANES, (c + 1) * 2 * LANES
            gu = jnp.dot(xb, wgu_s[:, lo:hi], preferred_element_type=F32) + bgu_ref[0, :, lo:hi]
            glu = jnp.minimum(gu[:, :LANES], SWIGLU_LIMIT)
            lin = jnp.clip(gu[:, LANES:], -SWIGLU_LIMIT, SWIGLU_LIMIT)
            hid_parts.append(((lin + 1.0) * (glu * _sigmoid(SWIGLU_ALPHA * glu))).astype(BF16))
            for r in range(c * GATHER_CHUNK, (c + 1) * GATHER_CHUNK):
                start_rows(i + ahead, (i + ahead) % GATHER_DEPTH, r)
        hid = jnp.concatenate(hid_parts, axis=1)
        y = jnp.dot(hid, wd_s[...], preferred_element_type=F32) + bd_ref[0]
        for s in range(ROW_TILES):
            y_ref[pl.ds(s, tm, stride=ROW_TILES), :] = y[:, s * LANES:(s + 1) * LANES]


def _experts(blk_expert, n_used, dest, first, parity, next_expert, pad_lo, pad_hi, hf_rows,
             wgu, wd, bgu_split, bd):
    n_blk = blk_expert.shape[0]
    tm = EXPERT_TM
    assert dest.shape[0] % (TOP_K * INVERT_UNROLL) == 0

    def ymap(i, be, nu, *_):
        return (i, 0)

    def bmap(i, be, nu, *_):
        return (be[jnp.minimum(i, nu[0] - 1)], 0, 0)

    grid_spec = pltpu.PrefetchScalarGridSpec(
        num_scalar_prefetch=8,
        grid=(n_blk,),
        in_specs=[
            pl.BlockSpec(memory_space=pl.ANY),
            pl.BlockSpec(memory_space=pl.ANY),
            pl.BlockSpec(memory_space=pl.ANY),
            pl.BlockSpec((1, 1, 2 * D_MODEL), bmap),
            pl.BlockSpec((1, 1, D_MODEL), bmap),
        ],
        out_specs=pl.BlockSpec((tm * ROW_TILES, LANES), ymap),
        scratch_shapes=[
            pltpu.VMEM((GATHER_DEPTH, tm * ROW_TILES, LANES), F32),
            pltpu.VMEM((2, D_MODEL, 2 * D_MODEL), F32),
            pltpu.VMEM((2, D_MODEL, D_MODEL), F32),
            pltpu.VMEM((D_MODEL, 2 * D_MODEL), BF16),
            pltpu.VMEM((D_MODEL, D_MODEL), BF16),
            pltpu.SMEM((n_blk * tm,), I32),
            pltpu.SemaphoreType.DMA((GATHER_DEPTH,)),
            pltpu.SemaphoreType.DMA((2,)),
        ],
    )
    return pl.pallas_call(
        _expert_kernel,
        grid_spec=grid_spec,
        out_shape=jax.ShapeDtypeStruct((n_blk * tm * ROW_TILES, LANES), F32),
        compiler_params=pltpu.CompilerParams(
            dimension_semantics=("arbitrary",), vmem_limit_bytes=VMEM_LIMIT),
        name="expert_ffn",
    )(blk_expert, n_used, dest, first, parity, next_expert, pad_lo, pad_hi, hf_rows, wgu, wd,
      bgu_split, bd)


def _combine_kernel(segsrc_ref, seglen_ref, segdst_ref, pos_ref,
                    x1_ref, gate_ref, ys_ref, p_ref, gple_ref, wpg_ref, wpp_ref, gfin_ref,
                    o_ref, segbuf, ybuf, sems):
    i = pl.program_id(0)
    n_steps = pl.num_programs(0)
    tm = OUT_TM
    slot = i % 2

    def fetch(tile, to_slot):
        for e in range(N_EXPERTS):
            n = pl.multiple_of(seglen_ref[tile * N_EXPERTS + e], ROW_TILES)
            src = pl.multiple_of(segsrc_ref[tile * N_EXPERTS + e], ROW_TILES)
            dst = pl.multiple_of(segdst_ref[tile * N_EXPERTS + e], ROW_TILES)

            @pl.when(n > 0)
            def _():
                pltpu.make_async_copy(ys_ref.at[pl.ds(src, n)],
                                      segbuf.at[to_slot, pl.ds(dst, n)], sems.at[to_slot]).start()

    @pl.when(i == 0)
    def _():
        fetch(0, 0)

    @pl.when(i + 1 < n_steps)
    def _():
        fetch(i + 1, 1 - slot)

    pltpu.make_async_copy(ys_ref.at[pl.ds(0, TOP_K * tm * ROW_TILES)], segbuf.at[slot],
                          sems.at[slot]).wait()

    for r in range(tm):
        for k in range(TOP_K):
            p = pl.multiple_of(pos_ref[(i * tm + r) * TOP_K + k], ROW_TILES)
            ybuf[(k * tm + r) * ROW_TILES:(k * tm + r + 1) * ROW_TILES, :] = (
                segbuf[slot, pl.ds(p, ROW_TILES), :])

    moe = None
    for k in range(TOP_K):
        yk = jnp.concatenate(
            [ybuf[pl.ds(k * tm * ROW_TILES + s, tm, stride=ROW_TILES), :]
             for s in range(ROW_TILES)], axis=1)
        term = gate_ref[:, k:k + 1] * yk
        moe = term if moe is None else moe + term
    x2 = x1_ref[...] + moe
    hp = _rms(x2, gple_ref[...]).astype(BF16)
    pg = _sigmoid(jnp.dot(hp, wpg_ref[...], preferred_element_type=F32))
    proj = jnp.dot(p_ref[...].astype(BF16), wpp_ref[...], preferred_element_type=F32)
    x3 = x2 + pg * proj
    o_ref[...] = _rms(x3, gfin_ref[...])


def _combine(seg_src, seg_len, seg_dst, pos, x1, gates_tk, ys, p2, gple, wpg, wpp, gfin):
    t = x1.shape[0]
    tm = OUT_TM
    ple = p2.shape[1]
    row = lambda i, *_: (i, 0)
    fixed = lambda i, *_: (0, 0)
    grid_spec = pltpu.PrefetchScalarGridSpec(
        num_scalar_prefetch=4,
        grid=(t // tm,),
        in_specs=[
            pl.BlockSpec((tm, D_MODEL), row),
            pl.BlockSpec((tm, TOP_K), row),
            pl.BlockSpec(memory_space=pl.ANY),
            pl.BlockSpec((tm, ple), row),
            pl.BlockSpec((1, D_MODEL), fixed),
            pl.BlockSpec((D_MODEL, D_MODEL), fixed),
            pl.BlockSpec((ple, D_MODEL), fixed),
            pl.BlockSpec((1, D_MODEL), fixed),
        ],
        out_specs=pl.BlockSpec((tm, D_MODEL), row),
        scratch_shapes=[
            pltpu.VMEM((2, TOP_K * tm * ROW_TILES, LANES), F32),
            pltpu.VMEM((TOP_K * tm * ROW_TILES, LANES), F32),
            pltpu.SemaphoreType.DMA((2,)),
        ],
    )
    return pl.pallas_call(
        _combine_kernel,
        grid_spec=grid_spec,
        out_shape=jax.ShapeDtypeStruct((t, D_MODEL), F32),
        compiler_params=pltpu.CompilerParams(
            dimension_semantics=("arbitrary",), vmem_limit_bytes=VMEM_LIMIT),
        name="combine_ple_final",
    )(seg_src, seg_len, seg_dst, pos, x1, gates_tk, ys, p2, gple, wpg, wpp, gfin)


def kernel(x, p, rel_bias, norm_mix_g, w_in, pool_w, pool_scale, w_o_attn, w_o_pool, w_out,
           norm_ffn_g, router_w, router_b, w_gate_up, b_gate_up, w_down, b_down,
           norm_ple_g, w_ple_gate, w_ple_proj, norm_final_g):
    b, s, d = x.shape
    depth = w_in.shape[0]
    t = b * s
    assert d == D_MODEL and s % MOBA_BLOCK == 0 and t % IN_TM == 0 and s % MIX_TM == 0
    assert depth == 1, "the final norm is fused into the layer's last kernel"
    n_pad = t * TOP_K + (N_EXPERTS + GATHER_DEPTH - 2) * EXPERT_TM
    n_blk = n_pad // EXPERT_TM
    bown, bprev, bfar = _t5_bias_tables(rel_bias)

    x2 = x.reshape(t, d)
    for i in range(depth):
        w_in_b = w_in[i].astype(BF16)
        wvt = w_in[i][:, 2 * ATTN_WIDTH:3 * ATTN_WIDTH].T.astype(BF16)
        q, k, vt, u, gl = _in_proj(x2, norm_mix_g[i].reshape(1, d), w_in_b, wvt, b, s)
        bh = b * N_HEADS
        attn_t = _attention(q.reshape(bh, s, HEAD_DIM), k.reshape(bh, s, HEAD_DIM),
                            vt.reshape(bh, V_ROWS, s), bown, bprev, bfar)
        attn = attn_t.reshape(b, ATTN_WIDTH, s)

        x1, hf_rows, idx_kt, rank_kt, gate_kt, cnt = _mix(
            x2, attn, u, gl, pool_w[i].astype(BF16), pool_scale[i].reshape(1, POOL_WIDTH),
            w_o_attn[i].astype(BF16), w_o_pool[i].astype(BF16), w_out[i].astype(BF16),
            norm_ffn_g[i].reshape(1, d), router_w[i].T, router_b[i].reshape(N_EXPERTS, 1), s)

        counts = cnt[:, 0]
        padded = (counts + EXPERT_TM - 1) // EXPERT_TM * EXPERT_TM
        pend = jnp.cumsum(padded)
        pstart = pend - padded
        e_ids = jnp.arange(N_EXPERTS, dtype=I32)
        pstart_of = jnp.sum(jnp.where(idx_kt[..., None] == e_ids, pstart, 0), axis=-1)
        dest_flat = (pstart_of + rank_kt).T.reshape(-1).astype(I32)
        blk_row0 = jnp.arange(n_blk, dtype=I32) * EXPERT_TM
        blk_expert = jnp.minimum(jnp.sum(pend[None, :] <= blk_row0[:, None], axis=1),
                                 N_EXPERTS - 1).astype(I32)
        n_used = (pend[-1:] // EXPERT_TM).astype(I32)
        pad_lo = jnp.concatenate([pstart + counts, pend[-1:]]).astype(I32)
        pad_hi = jnp.concatenate([pend, pend[-1:] + (GATHER_DEPTH - 1) * EXPERT_TM]).astype(I32)

        bgu = b_gate_up[i].reshape(N_EXPERTS, D_MODEL // LANES, LANES, 2).transpose(
            0, 1, 3, 2).reshape(N_EXPERTS, 1, 2 * D_MODEL)
        first = jnp.concatenate([jnp.ones((1,), I32),
                                 (blk_expert[1:] != blk_expert[:-1]).astype(I32)])
        parity = ((jnp.cumsum(first) - 1) % 2).astype(I32)
        later = jnp.logical_and(e_ids[None, :] > e_ids[:, None], (padded > 0)[None, :])
        next_e = jnp.min(jnp.where(later, e_ids[None, :], N_EXPERTS), axis=1)
        next_e = jnp.where(next_e == N_EXPERTS, -1, next_e).astype(I32)
        next_expert = jnp.sum(jnp.where(blk_expert[:, None] == e_ids, next_e, 0),
                              axis=1).astype(I32)
        ys = _experts(blk_expert, n_used, dest_flat, first, parity, next_expert, pad_lo, pad_hi,
                      hf_rows, w_gate_up[i], w_down[i], bgu, b_down[i][:, None, :])

        n_tiles = t // OUT_TM
        tile_cnt = jnp.sum(idx_kt.T.reshape(n_tiles, OUT_TM * TOP_K)[..., None] == e_ids,
                           axis=1).astype(I32)
        before = jnp.cumsum(tile_cnt, axis=0) - tile_cnt
        local_off = jnp.cumsum(tile_cnt, axis=1) - tile_cnt
        seg_src = ((pstart[None, :] + before) * ROW_TILES).reshape(-1).astype(I32)
        seg_len = (tile_cnt * ROW_TILES).reshape(-1).astype(I32)
        seg_dst = (local_off * ROW_TILES).reshape(-1).astype(I32)
        shift = jnp.repeat(local_off - before, OUT_TM, axis=0)
        pos = jnp.sum(jnp.where(idx_kt[..., None] == e_ids, shift[None], 0), axis=-1) + rank_kt
        pos = (pos.T.reshape(-1) * ROW_TILES).astype(I32)
        x2 = _combine(seg_src, seg_len, seg_dst, pos, x1, gate_kt.T, ys, p[i].reshape(t, -1),
                      norm_ple_g[i].reshape(1, d), w_ple_gate[i].astype(BF16),
                      w_ple_proj[i].astype(BF16), norm_final_g.reshape(1, d))
    return x2.reshape(b, s, d)
```

```python
import functools
import math

import jax
import jax.numpy as jnp
from jax import lax
from jax.experimental import pallas as pl
from jax.experimental.pallas import tpu as pltpu

F32 = jnp.float32
BF16 = jnp.bfloat16
I32 = jnp.int32

D_MODEL = 1024
N_HEADS = 8
HEAD_DIM = 64
ATTN_WIDTH = N_HEADS * HEAD_DIM
MOBA_BLOCK = 256
MOBA_TOPK = 3
NUM_BUCKETS = 32
MAX_DISTANCE = 128
POOL_WINDOWS = (2, 4, 8, 16)
POOL_GROUPS = 4
POOL_GROUP_DIM = 128
POOL_WIDTH = POOL_GROUPS * POOL_GROUP_DIM
N_EXPERTS = 32
TOP_K = 4
SWIGLU_LIMIT = 7.0
SWIGLU_ALPHA = 1.702
RMS_EPS = 1e-6

LANES = 128
SUBLANES = 8
BF16_SUBLANES = 16
ROW_TILES = D_MODEL // LANES
V_ROWS = HEAD_DIM + BF16_SUBLANES

IN_TM = 512
MIX_TM = 512
EXPERT_TM = 256
OUT_TM = 256
ATTN_GROUP = 4
ATTN_HEADS = 4
GATHER_CHUNK = EXPERT_TM // (D_MODEL // LANES)
WEIGHT_DMA_CHUNKS = 4
INVERT_UNROLL = 4
GATHER_DEPTH = 4
POOL_HALO = 16
MASKED = -1e30
LOG2E = math.log2(math.e)
Q_SCALE = HEAD_DIM ** -0.5 * LOG2E

VMEM_LIMIT = 56 * 1024 * 1024

_NT = (((1,), (1,)), ((), ()))


def _rms(x, g):
    ms = jnp.mean(x * x, axis=-1, keepdims=True)
    return x * lax.rsqrt(ms + RMS_EPS) * g


def _sigmoid(x):
    return 1.0 / (1.0 + jnp.exp(-x))


def _in_proj_kernel(x_ref, g_ref, w_ref, wvt_ref, q_ref, k_ref, vt_ref, u_ref, gl_ref):
    h = _rms(x_ref[...], g_ref[...]).astype(BF16)
    aw = ATTN_WIDTH

    def proj(lo, hi):
        return jnp.dot(h, w_ref[:, lo:hi], preferred_element_type=F32)

    qv = (proj(0, aw) * Q_SCALE).astype(BF16)
    kv = proj(aw, 2 * aw).astype(BF16)
    for hd in range(N_HEADS):
        q_ref[0, hd] = qv[:, hd * HEAD_DIM:(hd + 1) * HEAD_DIM]
        k_ref[0, hd] = kv[:, hd * HEAD_DIM:(hd + 1) * HEAD_DIM]
    vt = lax.dot_general(wvt_ref[...], h, _NT, preferred_element_type=F32).astype(BF16)
    tm = vt.shape[1]
    vt_ref[0, :, 0:HEAD_DIM, :] = vt.reshape(N_HEADS, HEAD_DIM, tm)
    extra = lax.broadcasted_iota(I32, (N_HEADS, V_ROWS - HEAD_DIM, tm), 1) == 0
    vt_ref[0, :, HEAD_DIM:V_ROWS, :] = extra.astype(BF16)
    u_ref[...] = proj(3 * aw, 3 * aw + POOL_WIDTH)
    base = 3 * aw + POOL_WIDTH
    for c in range(2):
        lo = base + c * D_MODEL
        gl_ref[:, c * D_MODEL:(c + 1) * D_MODEL] = proj(lo, lo + D_MODEL).astype(BF16)


def _in_proj(x2, g, w_bf16, wvt_bf16, batch, seq):
    t = x2.shape[0]
    in_cols = w_bf16.shape[1]
    tps = seq // IN_TM
    row = lambda i: (i, 0)
    fixed = lambda i: (0, 0)
    head_rows = lambda i: (i // tps, 0, i % tps, 0)
    head_cols = lambda i: (i // tps, 0, 0, i % tps)
    return pl.pallas_call(
        _in_proj_kernel,
        grid=(t // IN_TM,),
        in_specs=[
            pl.BlockSpec((IN_TM, D_MODEL), row),
            pl.BlockSpec((1, D_MODEL), fixed),
            pl.BlockSpec((D_MODEL, in_cols), fixed),
            pl.BlockSpec((ATTN_WIDTH, D_MODEL), fixed),
        ],
        out_specs=[
            pl.BlockSpec((1, N_HEADS, IN_TM, HEAD_DIM), head_rows),
            pl.BlockSpec((1, N_HEADS, IN_TM, HEAD_DIM), head_rows),
            pl.BlockSpec((1, N_HEADS, V_ROWS, IN_TM), head_cols),
            pl.BlockSpec((IN_TM, POOL_WIDTH), row),
            pl.BlockSpec((IN_TM, 2 * D_MODEL), row),
        ],
        out_shape=[
            jax.ShapeDtypeStruct((batch, N_HEADS, seq, HEAD_DIM), BF16),
            jax.ShapeDtypeStruct((batch, N_HEADS, seq, HEAD_DIM), BF16),
            jax.ShapeDtypeStruct((batch, N_HEADS, V_ROWS, seq), BF16),
            jax.ShapeDtypeStruct((t, POOL_WIDTH), F32),
            jax.ShapeDtypeStruct((t, 2 * D_MODEL), BF16),
        ],
        compiler_params=pltpu.CompilerParams(
            dimension_semantics=("parallel",), vmem_limit_bytes=VMEM_LIMIT),
        name="in_proj",
    )(x2, g, w_bf16, wvt_bf16)


def _attn_kernel(bfar_ref, q_ref, k_ref, vt_ref, bown_ref, bprev_ref, o_ref,
                 kmean_s, am_s, sa_s, sb_s, ga_s, gb_s, m_s, acc_s, *, n_blocks):
    hp = pl.program_id(0)
    j = pl.program_id(1)
    blk = MOBA_BLOCK
    grp = ATTN_GROUP
    gk = grp * blk
    n_groups = n_blocks // grp
    n_far = jnp.maximum(j - 1, 0)
    ng = (n_far + grp - 1) // grp
    jp = jnp.maximum(j - 1, 0)
    heads = range(ATTN_HEADS)

    @pl.when(j == 0)
    def _():
        for hh in heads:
            for n in range(n_blocks):
                kb = k_ref[hh, n * blk:(n + 1) * blk, :].astype(F32)
                kmean_s[hh, n:n + 1, :] = jnp.mean(kb, axis=0, keepdims=True)

    qs = [q_ref[hh] for hh in heads]

    def select(hh):
        gate = lax.dot_general(kmean_s[hh], qs[hh].astype(F32), _NT,
                               precision=lax.Precision.HIGHEST, preferred_element_type=F32)
        n_iota = lax.broadcasted_iota(I32, gate.shape, 0)
        past = n_iota < j
        gate = jnp.where(past, gate, jnp.finfo(F32).min)
        sel = jnp.zeros(gate.shape, jnp.bool_)
        for _ in range(MOBA_TOPK):
            mx = jnp.max(gate, axis=0, keepdims=True)
            idx = jnp.min(jnp.where(gate == mx, n_iota, n_blocks), axis=0, keepdims=True)
            pick = n_iota == idx
            sel = jnp.logical_or(sel, pick)
            gate = jnp.where(pick, -jnp.inf, gate)
        sel = jnp.logical_and(sel, past)
        bfar = bfar_ref[(hp * ATTN_HEADS + hh) % N_HEADS]
        am_s[hh] = jnp.where(jnp.logical_and(sel, n_iota < j - 1), bfar, MASKED)
        return jnp.max(jnp.where(jnp.logical_and(sel, n_iota == j - 1), 0.0, MASKED),
                       axis=0, keepdims=True)

    def produce(hh, g, s_ref, gmax_ref):
        g = jnp.minimum(g, n_groups - 1)
        kb = k_ref[hh, pl.ds(pl.multiple_of(g * gk, gk), gk), :]
        s = lax.dot_general(kb, qs[hh], _NT, preferred_element_type=F32)
        s_ref[hh] = s
        gmax = None
        for b in range(grp):
            mb = (jnp.max(s[b * blk:(b + 1) * blk], axis=0, keepdims=True)
                  + am_s[hh, pl.ds(g * grp + b, 1), :])
            gmax = mb if gmax is None else jnp.maximum(gmax, mb)
        gmax_ref[hh] = gmax

    def consume(hh, g, s_ref, gmax_ref):
        m_old = m_s[hh]
        m_new = jnp.maximum(m_old, gmax_ref[hh])
        alpha = jnp.exp2(m_old - m_new)
        parts = []
        for b in range(grp):
            shift = m_new - am_s[hh, pl.ds(g * grp + b, 1), :]
            pb = jnp.exp2(s_ref[hh, b * blk:(b + 1) * blk, :] - shift)
            parts.append(pb.astype(BF16))
        p = jnp.concatenate(parts, axis=0)
        vt = vt_ref[hh, :, pl.ds(pl.multiple_of(g * gk, gk), gk)]
        pv = jnp.dot(vt, p, preferred_element_type=F32)
        m_s[hh] = m_new
        acc_s[hh] = alpha * acc_s[hh] + pv

    def near_scores(hh, n, bias):
        kb = k_ref[hh, pl.ds(pl.multiple_of(n * blk, blk), blk), :]
        return lax.dot_general(kb, qs[hh], _NT, preferred_element_type=F32) + bias

    def near_pv(hh, n, p):
        vt = vt_ref[hh, :, pl.ds(pl.multiple_of(n * blk, blk), blk)]
        return jnp.dot(vt, p.astype(BF16), preferred_element_type=F32)

    for hh in heads:
        am_prev = select(hh)
        produce(hh, 0, sa_s, ga_s)
        s_own = near_scores(hh, j, bown_ref[hh])
        s_prev = near_scores(hh, jp, bprev_ref[hh])
        m = jnp.maximum(jnp.max(s_own, axis=0, keepdims=True),
                        jnp.max(s_prev, axis=0, keepdims=True) + am_prev)
        p_own = jnp.exp2(s_own - m)
        p_prev = jnp.exp2(s_prev - (m - am_prev))
        m_s[hh] = m
        acc_s[hh] = near_pv(hh, j, p_own) + near_pv(hh, jp, p_prev)

    def pair(h, carry):
        for hh in heads:
            consume(hh, 2 * h, sa_s, ga_s)
            produce(hh, 2 * h + 1, sb_s, gb_s)
        for hh in heads:
            consume(hh, 2 * h + 1, sb_s, gb_s)
            produce(hh, 2 * h + 2, sa_s, ga_s)
        return carry

    trips = jnp.maximum(ng - 1, 0) // 2
    lax.fori_loop(0, trips, pair, 0)
    g_last = 2 * trips

    @pl.when(ng - g_last == 1)
    def _():
        for hh in heads:
            consume(hh, g_last, sa_s, ga_s)

    @pl.when(ng - g_last == 2)
    def _():
        for hh in heads:
            consume(hh, g_last, sa_s, ga_s)
            produce(hh, g_last + 1, sb_s, gb_s)
        for hh in heads:
            consume(hh, g_last + 1, sb_s, gb_s)

    for hh in heads:
        acc = acc_s[hh]
        o_ref[hh] = (acc[0:HEAD_DIM] / acc[HEAD_DIM:HEAD_DIM + 1]).astype(o_ref.dtype)


def _attention(qh, kh, vth, bown, bprev, bfar):
    bhn, s, dh = qh.shape
    nb = s // MOBA_BLOCK
    ah = ATTN_HEADS
    assert N_HEADS % ah == 0 and nb % ATTN_GROUP == 0
    kern = functools.partial(_attn_kernel, n_blocks=nb)
    head_blk = lambda b, j: (b % (N_HEADS // ah), 0, 0)
    grid_spec = pltpu.PrefetchScalarGridSpec(
        num_scalar_prefetch=0,
        grid=(bhn // ah, nb),
        in_specs=[
            pl.BlockSpec(memory_space=pltpu.SMEM),
            pl.BlockSpec((ah, MOBA_BLOCK, dh), lambda b, j: (b, j, 0)),
            pl.BlockSpec((ah, s, dh), lambda b, j: (b, 0, 0)),
            pl.BlockSpec((ah, V_ROWS, s), lambda b, j: (b, 0, 0)),
            pl.BlockSpec((ah, MOBA_BLOCK, MOBA_BLOCK), head_blk),
            pl.BlockSpec((ah, MOBA_BLOCK, MOBA_BLOCK), head_blk),
        ],
        out_specs=pl.BlockSpec((ah, dh, MOBA_BLOCK), lambda b, j: (b, 0, j)),
        scratch_shapes=[
            pltpu.VMEM((ah, nb, dh), F32),
            pltpu.VMEM((ah, nb, MOBA_BLOCK), F32),
            pltpu.VMEM((ah, ATTN_GROUP * MOBA_BLOCK, MOBA_BLOCK), F32),
            pltpu.VMEM((ah, ATTN_GROUP * MOBA_BLOCK, MOBA_BLOCK), F32),
            pltpu.VMEM((ah, 1, MOBA_BLOCK), F32),
            pltpu.VMEM((ah, 1, MOBA_BLOCK), F32),
            pltpu.VMEM((ah, 1, MOBA_BLOCK), F32),
            pltpu.VMEM((ah, V_ROWS, MOBA_BLOCK), F32),
        ],
    )
    return pl.pallas_call(
        kern,
        grid_spec=grid_spec,
        out_shape=jax.ShapeDtypeStruct((bhn, dh, s), BF16),
        compiler_params=pltpu.CompilerParams(
            dimension_semantics=("parallel", "arbitrary"), vmem_limit_bytes=VMEM_LIMIT),
        name="moba_attention",
    )(bfar, qh, kh, vth, bown, bprev)


def _t5_bias_tables(rel_bias):
    blk = MOBA_BLOCK
    n = jnp.arange(2 * blk)
    max_exact = NUM_BUCKETS // 2
    nf = jnp.maximum(n, 1).astype(F32)
    large = max_exact + (jnp.log(nf / max_exact) / math.log(MAX_DISTANCE / max_exact)
                         * (NUM_BUCKETS - max_exact)).astype(I32)
    large = jnp.minimum(large, NUM_BUCKETS - 1)
    bucket = jnp.where(n < max_exact, n, large)
    rel_bias = rel_bias.astype(F32) * LOG2E
    tbl = rel_bias[bucket].T
    key = jnp.arange(blk)[:, None]
    qry = jnp.arange(blk)[None, :]
    d = qry - key
    h = tbl.shape[0]
    wide = 2 * blk + 1
    skew = jnp.broadcast_to(jnp.pad(tbl, ((0, 0), (0, 1)))[:, None, :], (h, blk, wide))
    skew = skew.reshape(h, blk * wide)[:, :blk * 2 * blk].reshape(h, blk, 2 * blk)
    bown = jnp.where(d >= 0, skew[:, :, :blk], MASKED)
    bprev = skew[:, :, blk:]
    bfar = rel_bias[NUM_BUCKETS - 1]
    return bown, bprev, bfar


def _mix_kernel(x_ref, attn_ref, u_ref, halo_ref, gl_ref, pw_ref, ps_ref, woa_ref, wop_ref,
                wout_ref, gffn_ref, rwt_ref, rb_ref,
                x1_ref, hf_ref, idx_ref, rank_ref, gate_ref, cnt_ref,
                ext_s, carry_s, *, tiles_per_seq):
    i = pl.program_id(0)
    tm = MIX_TM

    @pl.when(i == 0)
    def _():
        carry_s[...] = jnp.zeros_like(carry_s)

    first = (i % tiles_per_seq) == 0
    ext_s[0:POOL_HALO, :] = jnp.where(first, 0.0, halo_ref[...])
    ext_s[POOL_HALO:POOL_HALO + tm, :] = u_ref[...]
    pos = (i % tiles_per_seq) * tm + lax.broadcasted_iota(I32, (tm, 1), 0)
    pooled_parts = []
    for g, w in enumerate(POOL_WINDOWS):
        c0, c1 = g * POOL_GROUP_DIM, (g + 1) * POOL_GROUP_DIM
        win = ext_s[POOL_HALO:POOL_HALO + tm, c0:c1]
        for sft in range(1, w):
            win = win + ext_s[POOL_HALO - sft:POOL_HALO - sft + tm, c0:c1]
        cnt = jnp.minimum(pos + 1, w).astype(F32)
        pin = win / cnt - ext_s[POOL_HALO:POOL_HALO + tm, c0:c1]
        pooled_parts.append(jnp.dot(pin.astype(BF16), pw_ref[g], preferred_element_type=F32))
    pooled = jnp.concatenate(pooled_parts, axis=1) * ps_ref[...]

    a = lax.dot_general(attn_ref[0], woa_ref[...], (((0,), (0,)), ((), ())),
                        preferred_element_type=F32)
    pm = jnp.dot(pooled.astype(BF16), wop_ref[...], preferred_element_type=F32)
    g0 = _sigmoid(gl_ref[:, 0:D_MODEL].astype(F32))
    g1 = _sigmoid(gl_ref[:, D_MODEL:2 * D_MODEL].astype(F32))
    merged = g0 * a + g1 * pm
    x1 = x_ref[...] + jnp.dot(merged.astype(BF16), wout_ref[...], preferred_element_type=F32)
    x1_ref[...] = x1

    hf = _rms(x1, gffn_ref[...])
    hfb = hf.astype(BF16)
    for s in range(ROW_TILES):
        hf_ref[pl.ds(s, tm, stride=ROW_TILES), :] = hfb[:, s * LANES:(s + 1) * LANES].astype(F32)

    logits = lax.dot_general(rwt_ref[...], hfb.astype(F32), _NT,
                             precision=lax.Precision.HIGHEST,
                             preferred_element_type=F32) + rb_ref[...]
    e_iota = lax.broadcasted_iota(I32, logits.shape, 0)
    vals, picks = [], []
    for _ in range(TOP_K):
        mx = jnp.max(logits, axis=0, keepdims=True)
        idx = jnp.min(jnp.where(logits == mx, e_iota, N_EXPERTS), axis=0, keepdims=True)
        pick = e_iota == idx
        vals.append(mx)
        picks.append(pick)
        idx_ref[len(picks) - 1:len(picks), :] = idx
        logits = jnp.where(pick, -jnp.inf, logits)
    ex = [jnp.exp(v - vals[0]) for v in vals]
    den = ex[0] + ex[1] + ex[2] + ex[3]
    for k in range(TOP_K):
        gate_ref[k:k + 1, :] = ex[k] / den

    onehot = jnp.zeros(e_iota.shape, F32)
    for pick in picks:
        onehot = onehot + pick.astype(F32)
    ra = lax.broadcasted_iota(I32, (tm, tm), 0)
    rb = lax.broadcasted_iota(I32, (tm, tm), 1)
    upper = (ra < rb).astype(BF16)
    before = jnp.dot(onehot.astype(BF16), upper, preferred_element_type=F32) + carry_s[...]
    for k, pick in enumerate(picks):
        rank_ref[k:k + 1, :] = jnp.sum(jnp.where(pick, before, 0.0), axis=0,
                                       keepdims=True).astype(I32)
    carry_s[...] = carry_s[...] + jnp.sum(onehot, axis=1, keepdims=True)
    cnt_ref[...] = jnp.broadcast_to(carry_s[...], cnt_ref.shape).astype(I32)


def _mix(x2, attn, u, gl, pw, ps, woa, wop, wout, gffn, rwt, rb, seq):
    t = x2.shape[0]
    tm = MIX_TM
    tiles_per_seq = seq // tm
    halo_per_tile = tm // POOL_HALO
    row = lambda i: (i, 0)
    fixed2 = lambda i: (0, 0)
    fixed3 = lambda i: (0, 0, 0)
    col = lambda i: (0, i)
    kern = functools.partial(_mix_kernel, tiles_per_seq=tiles_per_seq)
    return pl.pallas_call(
        kern,
        grid=(t // tm,),
        in_specs=[
            pl.BlockSpec((tm, D_MODEL), row),
            pl.BlockSpec((1, ATTN_WIDTH, tm), lambda i: (i // tiles_per_seq, 0, i % tiles_per_seq)),
            pl.BlockSpec((tm, POOL_WIDTH), row),
            pl.BlockSpec((POOL_HALO, POOL_WIDTH),
                         lambda i: (jnp.maximum(i * halo_per_tile - 1, 0), 0)),
            pl.BlockSpec((tm, 2 * D_MODEL), row),
            pl.BlockSpec((POOL_GROUPS, POOL_GROUP_DIM, POOL_GROUP_DIM), fixed3),
            pl.BlockSpec((1, POOL_WIDTH), fixed2),
            pl.BlockSpec((ATTN_WIDTH, D_MODEL), fixed2),
            pl.BlockSpec((POOL_WIDTH, D_MODEL), fixed2),
            pl.BlockSpec((D_MODEL, D_MODEL), fixed2),
            pl.BlockSpec((1, D_MODEL), fixed2),
            pl.BlockSpec((N_EXPERTS, D_MODEL), fixed2),
            pl.BlockSpec((N_EXPERTS, 1), fixed2),
        ],
        out_specs=[
            pl.BlockSpec((tm, D_MODEL), row),
            pl.BlockSpec((tm * ROW_TILES, LANES), row),
            pl.BlockSpec((TOP_K, tm), col),
            pl.BlockSpec((TOP_K, tm), col),
            pl.BlockSpec((TOP_K, tm), col),
            pl.BlockSpec((N_EXPERTS, LANES), fixed2),
        ],
        out_shape=[
            jax.ShapeDtypeStruct((t, D_MODEL), F32),
            jax.ShapeDtypeStruct((t * ROW_TILES, LANES), F32),
            jax.ShapeDtypeStruct((TOP_K, t), I32),
            jax.ShapeDtypeStruct((TOP_K, t), I32),
            jax.ShapeDtypeStruct((TOP_K, t), F32),
            jax.ShapeDtypeStruct((N_EXPERTS, LANES), I32),
        ],
        scratch_shapes=[
            pltpu.VMEM((POOL_HALO + tm, POOL_WIDTH), F32),
            pltpu.VMEM((N_EXPERTS, 1), F32),
        ],
        compiler_params=pltpu.CompilerParams(
            dimension_semantics=("arbitrary",), vmem_limit_bytes=VMEM_LIMIT),
        name="mix_router",
    )(x2, attn, u, u, gl, pw, ps, woa, wop, wout, gffn, rwt, rb)


def _row(ref, r):
    return ref.at[pl.ds(pl.multiple_of(r * ROW_TILES, ROW_TILES), ROW_TILES)]


def _pair_split_matrix():
    n = 2 * LANES
    r = lax.broadcasted_iota(I32, (n, n), 0)
    c = lax.broadcasted_iota(I32, (n, n), 1)
    src = jnp.where(c < LANES, 2 * c, 2 * (c - LANES) + 1)
    return (r == src).astype(BF16)


def _expert_kernel(be_ref, nused_ref, dest_ref, first_ref, par_ref, next_ref, padlo_ref, padhi_ref,
                   hf_ref, wgu_hbm, wd_hbm, bgu_ref, bd_ref,
                   y_ref, xbuf, wgu_f, wd_f, wgu_s, wd_s, rowsrc_ref, sems, wsems):
    i = pl.program_id(0)
    tm = EXPERT_TM
    n_used = nused_ref[0]
    slot = i % GATHER_DEPTH
    ahead = GATHER_DEPTH - 1

    def start_rows(blk, to_slot, r):
        src = pl.multiple_of(rowsrc_ref[blk * tm + r], ROW_TILES)
        pltpu.make_async_copy(hf_ref.at[pl.ds(src, ROW_TILES)], _row(xbuf.at[to_slot], r),
                              sems.at[to_slot]).start()

    def wait_rows(at_slot):
        pltpu.make_async_copy(hf_ref.at[pl.ds(0, tm * ROW_TILES)], xbuf.at[at_slot],
                              sems.at[at_slot]).wait()

    def weight_copies(e, w_slot, start):
        rows = D_MODEL // WEIGHT_DMA_CHUNKS
        for c in range(WEIGHT_DMA_CHUNKS):
            sl = pl.ds(c * rows, rows)
            for src, dst in ((wgu_hbm, wgu_f), (wd_hbm, wd_f)):
                if start:
                    pltpu.async_copy(src.at[e, sl], dst.at[w_slot, sl], wsems.at[w_slot],
                                     priority=1)
                else:
                    pltpu.make_async_copy(src.at[e, sl], dst.at[w_slot, sl],
                                          wsems.at[w_slot]).wait()

    @pl.when(i == 0)
    def _():
        weight_copies(be_ref[0], 0, True)

        def zero_range(e, carry):
            def zero(r, c):
                rowsrc_ref[r] = 0
                return c
            lax.fori_loop(padlo_ref[e], padhi_ref[e], zero, 0)
            return carry
        lax.fori_loop(0, N_EXPERTS + 1, zero_range, 0)

        def invert(t0, carry):
            for u in range(INVERT_UNROLL):
                tok = t0 * INVERT_UNROLL + u
                for k in range(TOP_K):
                    rowsrc_ref[dest_ref[tok * TOP_K + k]] = tok * ROW_TILES
            return carry
        lax.fori_loop(0, dest_ref.shape[0] // (TOP_K * INVERT_UNROLL), invert, 0)

        def row(r, carry):
            for blk in range(ahead):
                start_rows(blk, blk, r)
            return carry
        lax.fori_loop(0, tm, row, 0)

    @pl.when(jnp.logical_and(first_ref[i] == 1, i < n_used))
    def _():
        par = par_ref[i]
        weight_copies(be_ref[i], par, False)

        @pl.when(next_ref[i] >= 0)
        def _():
            weight_copies(next_ref[i], 1 - par, True)

        split = _pair_split_matrix()
        for c in range(2 * D_MODEL // (2 * LANES)):
            lo, hi = c * 2 * LANES, (c + 1) * 2 * LANES
            wgu_s[:, lo:hi] = jnp.dot(wgu_f[par, :, lo:hi].astype(BF16), split,
                                      preferred_element_type=F32).astype(BF16)
        wd_s[...] = wd_f[par].astype(BF16)

    @pl.when(i >= n_used)
    def _():
        y_ref[...] = jnp.zeros_like(y_ref)

    @pl.when(jnp.logical_and(i >= n_used, i < n_used + ahead))
    def _():
        wait_rows(slot)

    @pl.when(i < n_used)
    def _():
        wait_rows(slot)
        xb = jnp.concatenate(
            [xbuf[slot, pl.ds(s, tm, stride=ROW_TILES), :] for s in range(ROW_TILES)],
            axis=1).astype(BF16)
        hid_parts = []
        n_chunks = D_MODEL // LANES
        assert n_chunks * GATHER_CHUNK == tm
        for c in range(n_chunks):
            lo, hi = c * 2 * LANES, (c + 1) * 2 * LANES
            gu = jnp.dot(xb, wgu_s[:, lo:hi], preferred_element_type=F32) + bgu_ref[0, :, lo:hi]
            glu = jnp.minimum(gu[:, :LANES], SWIGLU_LIMIT)
            lin = jnp.clip(gu[:, LANES:], -SWIGLU_LIMIT, SWIGLU_LIMIT)
            hid_parts.append(((lin + 1.0) * (glu * _sigmoid(SWIGLU_ALPHA * glu))).astype(BF16))
            for r in range(c * GATHER_CHUNK, (c + 1) * GATHER_CHUNK):
                start_rows(i + ahead, (i + ahead) % GATHER_DEPTH, r)
        hid = jnp.concatenate(hid_parts, axis=1)
        y = jnp.dot(hid, wd_s[...], preferred_element_type=F32) + bd_ref[0]
        for s in range(ROW_TILES):
            y_ref[pl.ds(s, tm, stride=ROW_TILES), :] = y[:, s * LANES:(s + 1) * LANES]


def _experts(blk_expert, n_used, dest, first, parity, next_expert, pad_lo, pad_hi, hf_rows,
             wgu, wd, bgu_split, bd):
    n_blk = blk_expert.shape[0]
    tm = EXPERT_TM
    assert dest.shape[0] % (TOP_K * INVERT_UNROLL) == 0

    def ymap(i, be, nu, *_):
        return (i, 0)

    def bmap(i, be, nu, *_):
        return (be[jnp.minimum(i, nu[0] - 1)], 0, 0)

    grid_spec = pltpu.PrefetchScalarGridSpec(
        num_scalar_prefetch=8,
        grid=(n_blk,),
        in_specs=[
            pl.BlockSpec(memory_space=pl.ANY),
            pl.BlockSpec(memory_space=pl.ANY),
            pl.BlockSpec(memory_space=pl.ANY),
            pl.BlockSpec((1, 1, 2 * D_MODEL), bmap),
            pl.BlockSpec((1, 1, D_MODEL), bmap),
        ],
        out_specs=pl.BlockSpec((tm * ROW_TILES, LANES), ymap),
        scratch_shapes=[
            pltpu.VMEM((GATHER_DEPTH, tm * ROW_TILES, LANES), F32),
            pltpu.VMEM((2, D_MODEL, 2 * D_MODEL), F32),
            pltpu.VMEM((2, D_MODEL, D_MODEL), F32),
            pltpu.VMEM((D_MODEL, 2 * D_MODEL), BF16),
            pltpu.VMEM((D_MODEL, D_MODEL), BF16),
            pltpu.SMEM((n_blk * tm,), I32),
            pltpu.SemaphoreType.DMA((GATHER_DEPTH,)),
            pltpu.SemaphoreType.DMA((2,)),
        ],
    )
    return pl.pallas_call(
        _expert_kernel,
        grid_spec=grid_spec,
        out_shape=jax.ShapeDtypeStruct((n_blk * tm * ROW_TILES, LANES), F32),
        compiler_params=pltpu.CompilerParams(
            dimension_semantics=("arbitrary",), vmem_limit_bytes=VMEM_LIMIT),
        name="expert_ffn",
    )(blk_expert, n_used, dest, first, parity, next_expert, pad_lo, pad_hi, hf_rows, wgu, wd,
      bgu_split, bd)


def _combine_kernel(segsrc_ref, seglen_ref, segdst_ref, pos_ref,
                    x1_ref, gate_ref, ys_ref, p_ref, gple_ref, wpg_ref, wpp_ref, gfin_ref,
                    o_ref, segbuf, ybuf, sems):
    i = pl.program_id(0)
    n_steps = pl.num_programs(0)
    tm = OUT_TM
    slot = i % 2

    def fetch(tile, to_slot):
        for e in range(N_EXPERTS):
            n = pl.multiple_of(seglen_ref[tile * N_EXPERTS + e], ROW_TILES)
            src = pl.multiple_of(segsrc_ref[tile * N_EXPERTS + e], ROW_TILES)
            dst = pl.multiple_of(segdst_ref[tile * N_EXPERTS + e], ROW_TILES)

            @pl.when(n > 0)
            def _():
                pltpu.make_async_copy(ys_ref.at[pl.ds(src, n)],
                                      segbuf.at[to_slot, pl.ds(dst, n)], sems.at[to_slot]).start()

    @pl.when(i == 0)
    def _():
        fetch(0, 0)

    @pl.when(i + 1 < n_steps)
    def _():
        fetch(i + 1, 1 - slot)

    pltpu.make_async_copy(ys_ref.at[pl.ds(0, TOP_K * tm * ROW_TILES)], segbuf.at[slot],
                          sems.at[slot]).wait()

    for r in range(tm):
        for k in range(TOP_K):
            p = pl.multiple_of(pos_ref[(i * tm + r) * TOP_K + k], ROW_TILES)
            ybuf[(k * tm + r) * ROW_TILES:(k * tm + r + 1) * ROW_TILES, :] = (
                segbuf[slot, pl.ds(p, ROW_TILES), :])

    moe = None
    for k in range(TOP_K):
        yk = jnp.concatenate(
            [ybuf[pl.ds(k * tm * ROW_TILES + s, tm, stride=ROW_TILES), :]
             for s in range(ROW_TILES)], axis=1)
        term = gate_ref[:, k:k + 1] * yk
        moe = term if moe is None else moe + term
    x2 = x1_ref[...] + moe
    hp = _rms(x2, gple_ref[...]).astype(BF16)
    pg = _sigmoid(jnp.dot(hp, wpg_ref[...], preferred_element_type=F32))
    proj = jnp.dot(p_ref[...].astype(BF16), wpp_ref[...], preferred_element_type=F32)
    x3 = x2 + pg * proj
    o_ref[...] = _rms(x3, gfin_ref[...])


def _combine(seg_src, seg_len, seg_dst, pos, x1, gates_tk, ys, p2, gple, wpg, wpp, gfin):
    t = x1.shape[0]
    tm = OUT_TM
    ple = p2.shape[1]
    row = lambda i, *_: (i, 0)
    fixed = lambda i, *_: (0, 0)
    grid_spec = pltpu.PrefetchScalarGridSpec(
        num_scalar_prefetch=4,
        grid=(t // tm,),
        in_specs=[
            pl.BlockSpec((tm, D_MODEL), row),
            pl.BlockSpec((tm, TOP_K), row),
            pl.BlockSpec(memory_space=pl.ANY),
            pl.BlockSpec((tm, ple), row),
            pl.BlockSpec((1, D_MODEL), fixed),
            pl.BlockSpec((D_MODEL, D_MODEL), fixed),
            pl.BlockSpec((ple, D_MODEL), fixed),
            pl.BlockSpec((1, D_MODEL), fixed),
        ],
        out_specs=pl.BlockSpec((tm, D_MODEL), row),
        scratch_shapes=[
            pltpu.VMEM((2, TOP_K * tm * ROW_TILES, LANES), F32),
            pltpu.VMEM((TOP_K * tm * ROW_TILES, LANES), F32),
            pltpu.SemaphoreType.DMA((2,)),
        ],
    )
    return pl.pallas_call(
        _combine_kernel,
        grid_spec=grid_spec,
        out_shape=jax.ShapeDtypeStruct((t, D_MODEL), F32),
        compiler_params=pltpu.CompilerParams(
            dimension_semantics=("arbitrary",), vmem_limit_bytes=VMEM_LIMIT),
        name="combine_ple_final",
    )(seg_src, seg_len, seg_dst, pos, x1, gates_tk, ys, p2, gple, wpg, wpp, gfin)


def kernel(x, p, rel_bias, norm_mix_g, w_in, pool_w, pool_scale, w_o_attn, w_o_pool, w_out,
           norm_ffn_g, router_w, router_b, w_gate_up, b_gate_up, w_down, b_down,
           norm_ple_g, w_ple_gate, w_ple_proj, norm_final_g):
    b, s, d = x.shape
    depth = w_in.shape[0]
    t = b * s
    assert d == D_MODEL and s % MOBA_BLOCK == 0 and t % IN_TM == 0 and s % MIX_TM == 0
    assert depth == 1, "the final norm is fused into the layer's last kernel"
    n_pad = t * TOP_K + (N_EXPERTS + GATHER_DEPTH - 2) * EXPERT_TM
    n_blk = n_pad // EXPERT_TM
    bown, bprev, bfar = _t5_bias_tables(rel_bias)

    x2 = x.reshape(t, d)
    for i in range(depth):
        w_in_b = w_in[i].astype(BF16)
        wvt = w_in[i][:, 2 * ATTN_WIDTH:3 * ATTN_WIDTH].T.astype(BF16)
        q, k, vt, u, gl = _in_proj(x2, norm_mix_g[i].reshape(1, d), w_in_b, wvt, b, s)
        bh = b * N_HEADS
        attn_t = _attention(q.reshape(bh, s, HEAD_DIM), k.reshape(bh, s, HEAD_DIM),
                            vt.reshape(bh, V_ROWS, s), bown, bprev, bfar)
        attn = attn_t.reshape(b, ATTN_WIDTH, s)

        x1, hf_rows, idx_kt, rank_kt, gate_kt, cnt = _mix(
            x2, attn, u, gl, pool_w[i].astype(BF16), pool_scale[i].reshape(1, POOL_WIDTH),
            w_o_attn[i].astype(BF16), w_o_pool[i].astype(BF16), w_out[i].astype(BF16),
            norm_ffn_g[i].reshape(1, d), router_w[i].T, router_b[i].reshape(N_EXPERTS, 1), s)

        counts = cnt[:, 0]
        padded = (counts + EXPERT_TM - 1) // EXPERT_TM * EXPERT_TM
        pend = jnp.cumsum(padded)
        pstart = pend - padded
        e_ids = jnp.arange(N_EXPERTS, dtype=I32)
        pstart_of = jnp.sum(jnp.where(idx_kt[..., None] == e_ids, pstart, 0), axis=-1)
        dest_flat = (pstart_of + rank_kt).T.reshape(-1).astype(I32)
        blk_row0 = jnp.arange(n_blk, dtype=I32) * EXPERT_TM
        blk_expert = jnp.minimum(jnp.sum(pend[None, :] <= blk_row0[:, None], axis=1),
                                 N_EXPERTS - 1).astype(I32)
        n_used = (pend[-1:] // EXPERT_TM).astype(I32)
        pad_lo = jnp.concatenate([pstart + counts, pend[-1:]]).astype(I32)
        pad_hi = jnp.concatenate([pend, pend[-1:] + (GATHER_DEPTH - 1) * EXPERT_TM]).astype(I32)

        bgu = b_gate_up[i].reshape(N_EXPERTS, D_MODEL // LANES, LANES, 2).transpose(
            0, 1, 3, 2).reshape(N_EXPERTS, 1, 2 * D_MODEL)
        first = jnp.concatenate([jnp.ones((1,), I32),
                                 (blk_expert[1:] != blk_expert[:-1]).astype(I32)])
        parity = ((jnp.cumsum(first) - 1) % 2).astype(I32)
        later = jnp.logical_and(e_ids[None, :] > e_ids[:, None], (padded > 0)[None, :])
        next_e = jnp.min(jnp.where(later, e_ids[None, :], N_EXPERTS), axis=1)
        next_e = jnp.where(next_e == N_EXPERTS, -1, next_e).astype(I32)
        next_expert = jnp.sum(jnp.where(blk_expert[:, None] == e_ids, next_e, 0),
                              axis=1).astype(I32)
        ys = _experts(blk_expert, n_used, dest_flat, first, parity, next_expert, pad_lo, pad_hi,
                      hf_rows, w_gate_up[i], w_down[i], bgu, b_down[i][:, None, :])

        n_tiles = t // OUT_TM
        tile_cnt = jnp.sum(idx_kt.T.reshape(n_tiles, OUT_TM * TOP_K)[..., None] == e_ids,
                           axis=1).astype(I32)
        before = jnp.cumsum(tile_cnt, axis=0) - tile_cnt
        local_off = jnp.cumsum(tile_cnt, axis=1) - tile_cnt
        seg_src = ((pstart[None, :] + before) * ROW_TILES).reshape(-1).astype(I32)
        seg_len = (tile_cnt * ROW_TILES).reshape(-1).astype(I32)
        seg_dst = (local_off * ROW_TILES).reshape(-1).astype(I32)
        shift = jnp.repeat(local_off - before, OUT_TM, axis=0)
        pos = jnp.sum(jnp.where(idx_kt[..., None] == e_ids, shift[None], 0), axis=-1) + rank_kt
        pos = (pos.T.reshape(-1) * ROW_TILES).astype(I32)
        x2 = _combine(seg_src, seg_len, seg_dst, pos, x1, gate_kt.T, ys, p[i].reshape(t, -1),
                      norm_ple_g[i].reshape(1, d), w_ple_gate[i].astype(BF16),
                      w_ple_proj[i].astype(BF16), norm_final_g.reshape(1, d))
    return x2.reshape(b, s, d)
```

```python
import functools
import math

import jax
import jax.numpy as jnp
from jax import lax
from jax.experimental import pallas as pl
from jax.experimental.pallas import tpu as pltpu

F32 = jnp.float32
BF16 = jnp.bfloat16
I32 = jnp.int32

D_MODEL = 1024
N_HEADS = 8
HEAD_DIM = 64
ATTN_WIDTH = N_HEADS * HEAD_DIM
MOBA_BLOCK = 256
MOBA_TOPK = 3
NUM_BUCKETS = 32
MAX_DISTANCE = 128
POOL_WINDOWS = (2, 4, 8, 16)
POOL_GROUPS = 4
POOL_GROUP_DIM = 128
POOL_WIDTH = POOL_GROUPS * POOL_GROUP_DIM
N_EXPERTS = 32
TOP_K = 4
SWIGLU_LIMIT = 7.0
SWIGLU_ALPHA = 1.702
RMS_EPS = 1e-6

LANES = 128
BF16_SUBLANES = 16
V7X_VMEM_BYTES = 64 * 1024 * 1024
ROW_TILES = D_MODEL // LANES
V_ROWS = HEAD_DIM + BF16_SUBLANES

IN_TM = 512
MIX_TM = 512
EXPERT_TM = 256
OUT_TM = 512
ATTN_GROUP = 4
ATTN_HEADS = 4
GATHER_CHUNK = EXPERT_TM // (D_MODEL // LANES)
INVERT_UNROLL = 4
GATHER_DEPTH = 6
POOL_HALO = 16
MASKED = -1e30
LOG2E = math.log2(math.e)
Q_SCALE = HEAD_DIM ** -0.5 * LOG2E

VMEM_LIMIT = V7X_VMEM_BYTES * 7 // 8

_NT = (((1,), (1,)), ((), ()))


def _rms(x, g):
    ms = jnp.mean(x * x, axis=-1, keepdims=True)
    return x * lax.rsqrt(ms + RMS_EPS) * g


def _sigmoid(x):
    return 1.0 / (1.0 + jnp.exp(-x))


def _in_proj_kernel(x_ref, g_ref, w_ref, wvt_ref, q_ref, k_ref, vt_ref, u_ref, gl_ref):
    h = _rms(x_ref[...], g_ref[...]).astype(BF16)
    aw = ATTN_WIDTH

    def proj(lo, hi):
        return jnp.dot(h, w_ref[:, lo:hi], preferred_element_type=F32)

    qv = (proj(0, aw) * Q_SCALE).astype(BF16)
    kv = proj(aw, 2 * aw).astype(BF16)
    for hd in range(N_HEADS):
        q_ref[0, hd] = qv[:, hd * HEAD_DIM:(hd + 1) * HEAD_DIM]
        k_ref[0, hd] = kv[:, hd * HEAD_DIM:(hd + 1) * HEAD_DIM]
    vt = lax.dot_general(wvt_ref[...], h, _NT, preferred_element_type=F32).astype(BF16)
    tm = vt.shape[1]
    vt_ref[0, :, 0:HEAD_DIM, :] = vt.reshape(N_HEADS, HEAD_DIM, tm)
    extra = lax.broadcasted_iota(I32, (N_HEADS, V_ROWS - HEAD_DIM, tm), 1) == 0
    vt_ref[0, :, HEAD_DIM:V_ROWS, :] = extra.astype(BF16)
    u_ref[...] = proj(3 * aw, 3 * aw + POOL_WIDTH)
    base = 3 * aw + POOL_WIDTH
    for c in range(2):
        lo = base + c * D_MODEL
        gl_ref[:, c * D_MODEL:(c + 1) * D_MODEL] = proj(lo, lo + D_MODEL).astype(BF16)


def _in_proj(x2, g, w_bf16, wvt_bf16, batch, seq):
    t = x2.shape[0]
    in_cols = w_bf16.shape[1]
    tps = seq // IN_TM
    row = lambda i: (i, 0)
    fixed = lambda i: (0, 0)
    head_rows = lambda i: (i // tps, 0, i % tps, 0)
    head_cols = lambda i: (i // tps, 0, 0, i % tps)
    return pl.pallas_call(
        _in_proj_kernel,
        grid=(t // IN_TM,),
        in_specs=[
            pl.BlockSpec((IN_TM, D_MODEL), row),
            pl.BlockSpec((1, D_MODEL), fixed),
            pl.BlockSpec((D_MODEL, in_cols), fixed),
            pl.BlockSpec((ATTN_WIDTH, D_MODEL), fixed),
        ],
        out_specs=[
            pl.BlockSpec((1, N_HEADS, IN_TM, HEAD_DIM), head_rows),
            pl.BlockSpec((1, N_HEADS, IN_TM, HEAD_DIM), head_rows),
            pl.BlockSpec((1, N_HEADS, V_ROWS, IN_TM), head_cols),
            pl.BlockSpec((IN_TM, POOL_WIDTH), row),
            pl.BlockSpec((IN_TM, 2 * D_MODEL), row),
        ],
        out_shape=[
            jax.ShapeDtypeStruct((batch, N_HEADS, seq, HEAD_DIM), BF16),
            jax.ShapeDtypeStruct((batch, N_HEADS, seq, HEAD_DIM), BF16),
            jax.ShapeDtypeStruct((batch, N_HEADS, V_ROWS, seq), BF16),
            jax.ShapeDtypeStruct((t, POOL_WIDTH), F32),
            jax.ShapeDtypeStruct((t, 2 * D_MODEL), BF16),
        ],
        compiler_params=pltpu.CompilerParams(
            dimension_semantics=("parallel",), vmem_limit_bytes=VMEM_LIMIT),
        name="in_proj",
    )(x2, g, w_bf16, wvt_bf16)


def _attn_kernel(bfar_ref, q_ref, k_ref, vt_ref, bown_ref, bprev_ref, o_ref,
                 kmean_s, am_s, sa_s, sb_s, ga_s, gb_s, m_s, acc_s, *, n_blocks):
    hp = pl.program_id(0)
    j = pl.program_id(1)
    blk = MOBA_BLOCK
    grp = ATTN_GROUP
    gk = grp * blk
    n_groups = n_blocks // grp
    n_far = jnp.maximum(j - 1, 0)
    ng = (n_far + grp - 1) // grp
    jp = jnp.maximum(j - 1, 0)
    heads = range(ATTN_HEADS)

    @pl.when(j == 0)
    def _():
        for hh in heads:
            for n in range(n_blocks):
                kb = k_ref[hh, n * blk:(n + 1) * blk, :].astype(F32)
                kmean_s[hh, n:n + 1, :] = jnp.mean(kb, axis=0, keepdims=True)

    qs = [q_ref[hh] for hh in heads]

    def select(hh):
        gate = lax.dot_general(kmean_s[hh], qs[hh].astype(F32), _NT,
                               precision=lax.Precision.HIGHEST, preferred_element_type=F32)
        n_iota = lax.broadcasted_iota(I32, gate.shape, 0)
        past = n_iota < j
        gate = jnp.where(past, gate, jnp.finfo(F32).min)
        sel = jnp.zeros(gate.shape, jnp.bool_)
        for _ in range(MOBA_TOPK):
            mx = jnp.max(gate, axis=0, keepdims=True)
            idx = jnp.min(jnp.where(gate == mx, n_iota, n_blocks), axis=0, keepdims=True)
            pick = n_iota == idx
            sel = jnp.logical_or(sel, pick)
            gate = jnp.where(pick, -jnp.inf, gate)
        sel = jnp.logical_and(sel, past)
        bfar = bfar_ref[(hp * ATTN_HEADS + hh) % N_HEADS]
        am_s[hh] = jnp.where(jnp.logical_and(sel, n_iota < j - 1), bfar, MASKED)
        return jnp.max(jnp.where(jnp.logical_and(sel, n_iota == j - 1), 0.0, MASKED),
                       axis=0, keepdims=True)

    def produce(hh, g, s_ref, gmax_ref):
        g = jnp.minimum(g, n_groups - 1)
        kb = k_ref[hh, pl.ds(pl.multiple_of(g * gk, gk), gk), :]
        s = lax.dot_general(kb, qs[hh], _NT, preferred_element_type=F32)
        s_ref[hh] = s
        gmax = None
        for b in range(grp):
            mb = (jnp.max(s[b * blk:(b + 1) * blk], axis=0, keepdims=True)
                  + am_s[hh, pl.ds(g * grp + b, 1), :])
            gmax = mb if gmax is None else jnp.maximum(gmax, mb)
        gmax_ref[hh] = gmax

    def consume(hh, g, s_ref, gmax_ref):
        m_old = m_s[hh]
        m_new = jnp.maximum(m_old, gmax_ref[hh])
        alpha = jnp.exp2(m_old - m_new)
        parts = []
        for b in range(grp):
            shift = m_new - am_s[hh, pl.ds(g * grp + b, 1), :]
            pb = jnp.exp2(s_ref[hh, b * blk:(b + 1) * blk, :] - shift)
            parts.append(pb.astype(BF16))
        p = jnp.concatenate(parts, axis=0)
        vt = vt_ref[hh, :, pl.ds(pl.multiple_of(g * gk, gk), gk)]
        pv = jnp.dot(vt, p, preferred_element_type=F32)
        m_s[hh] = m_new
        acc_s[hh] = alpha * acc_s[hh] + pv

    def near_scores(hh, n, bias):
        kb = k_ref[hh, pl.ds(pl.multiple_of(n * blk, blk), blk), :]
        return lax.dot_general(kb, qs[hh], _NT, preferred_element_type=F32) + bias

    def near_pv(hh, n, p):
        vt = vt_ref[hh, :, pl.ds(pl.multiple_of(n * blk, blk), blk)]
        return jnp.dot(vt, p.astype(BF16), preferred_element_type=F32)

    for hh in heads:
        am_prev = select(hh)
        produce(hh, 0, sa_s, ga_s)
        s_own = near_scores(hh, j, bown_ref[hh])
        s_prev = near_scores(hh, jp, bprev_ref[hh])
        m = jnp.maximum(jnp.max(s_own, axis=0, keepdims=True),
                        jnp.max(s_prev, axis=0, keepdims=True) + am_prev)
        p_own = jnp.exp2(s_own - m)
        p_prev = jnp.exp2(s_prev - (m - am_prev))
        m_s[hh] = m
        acc_s[hh] = near_pv(hh, j, p_own) + near_pv(hh, jp, p_prev)

    def pair(h, carry):
        for hh in heads:
            consume(hh, 2 * h, sa_s, ga_s)
            produce(hh, 2 * h + 1, sb_s, gb_s)
        for hh in heads:
            consume(hh, 2 * h + 1, sb_s, gb_s)
            produce(hh, 2 * h + 2, sa_s, ga_s)
        return carry

    trips = jnp.maximum(ng - 1, 0) // 2
    lax.fori_loop(0, trips, pair, 0)
    g_last = 2 * trips

    @pl.when(ng - g_last == 1)
    def _():
        for hh in heads:
            consume(hh, g_last, sa_s, ga_s)

    @pl.when(ng - g_last == 2)
    def _():
        for hh in heads:
            consume(hh, g_last, sa_s, ga_s)
            produce(hh, g_last + 1, sb_s, gb_s)
        for hh in heads:
            consume(hh, g_last + 1, sb_s, gb_s)

    for hh in heads:
        acc = acc_s[hh]
        o_ref[hh] = (acc[0:HEAD_DIM] / acc[HEAD_DIM:HEAD_DIM + 1]).astype(o_ref.dtype)


def _attention(qh, kh, vth, bown, bprev, bfar):
    bhn, s, dh = qh.shape
    nb = s // MOBA_BLOCK
    ah = ATTN_HEADS
    assert N_HEADS % ah == 0 and nb % ATTN_GROUP == 0
    kern = functools.partial(_attn_kernel, n_blocks=nb)
    head_blk = lambda b, j: (b % (N_HEADS // ah), 0, 0)
    grid_spec = pltpu.PrefetchScalarGridSpec(
        num_scalar_prefetch=0,
        grid=(bhn // ah, nb),
        in_specs=[
            pl.BlockSpec(memory_space=pltpu.SMEM),
            pl.BlockSpec((ah, MOBA_BLOCK, dh), lambda b, j: (b, j, 0)),
            pl.BlockSpec((ah, s, dh), lambda b, j: (b, 0, 0)),
            pl.BlockSpec((ah, V_ROWS, s), lambda b, j: (b, 0, 0)),
            pl.BlockSpec((ah, MOBA_BLOCK, MOBA_BLOCK), head_blk),
            pl.BlockSpec((ah, MOBA_BLOCK, MOBA_BLOCK), head_blk),
        ],
        out_specs=pl.BlockSpec((ah, dh, MOBA_BLOCK), lambda b, j: (b, 0, j)),
        scratch_shapes=[
            pltpu.VMEM((ah, nb, dh), F32),
            pltpu.VMEM((ah, nb, MOBA_BLOCK), F32),
            pltpu.VMEM((ah, ATTN_GROUP * MOBA_BLOCK, MOBA_BLOCK), F32),
            pltpu.VMEM((ah, ATTN_GROUP * MOBA_BLOCK, MOBA_BLOCK), F32),
            pltpu.VMEM((ah, 1, MOBA_BLOCK), F32),
            pltpu.VMEM((ah, 1, MOBA_BLOCK), F32),
            pltpu.VMEM((ah, 1, MOBA_BLOCK), F32),
            pltpu.VMEM((ah, V_ROWS, MOBA_BLOCK), F32),
        ],
    )
    return pl.pallas_call(
        kern,
        grid_spec=grid_spec,
        out_shape=jax.ShapeDtypeStruct((bhn, dh, s), BF16),
        compiler_params=pltpu.CompilerParams(
            dimension_semantics=("parallel", "arbitrary"), vmem_limit_bytes=VMEM_LIMIT),
        name="moba_attention",
    )(bfar, qh, kh, vth, bown, bprev)


def _t5_bias_tables(rel_bias):
    blk = MOBA_BLOCK
    n = jnp.arange(2 * blk)
    max_exact = NUM_BUCKETS // 2
    nf = jnp.maximum(n, 1).astype(F32)
    large = max_exact + (jnp.log(nf / max_exact) / math.log(MAX_DISTANCE / max_exact)
                         * (NUM_BUCKETS - max_exact)).astype(I32)
    large = jnp.minimum(large, NUM_BUCKETS - 1)
    bucket = jnp.where(n < max_exact, n, large)
    rel_bias = rel_bias.astype(F32) * LOG2E
    tbl = rel_bias[bucket].T
    key = jnp.arange(blk)[:, None]
    qry = jnp.arange(blk)[None, :]
    d = qry - key
    h = tbl.shape[0]
    wide = 2 * blk + 1
    skew = jnp.broadcast_to(jnp.pad(tbl, ((0, 0), (0, 1)))[:, None, :], (h, blk, wide))
    skew = skew.reshape(h, blk * wide)[:, :blk * 2 * blk].reshape(h, blk, 2 * blk)
    bown = jnp.where(d >= 0, skew[:, :, :blk], MASKED)
    bprev = skew[:, :, blk:]
    bfar = rel_bias[NUM_BUCKETS - 1]
    return bown, bprev, bfar


def _mix_kernel(x_ref, attn_ref, u_ref, halo_ref, gl_ref, pw_ref, ps_ref, woa_ref, wop_ref,
                wout_ref, gffn_ref, rwt_ref, rb_ref,
                x1_ref, hf_ref, idx_ref, rank_ref, gate_ref, cnt_ref,
                ext_s, carry_s, *, tiles_per_seq):
    i = pl.program_id(0)
    tm = MIX_TM

    @pl.when(i == 0)
    def _():
        carry_s[...] = jnp.zeros_like(carry_s)

    first = (i % tiles_per_seq) == 0
    ext_s[0:POOL_HALO, :] = jnp.where(first, 0.0, halo_ref[...])
    ext_s[POOL_HALO:POOL_HALO + tm, :] = u_ref[...]
    pos = (i % tiles_per_seq) * tm + lax.broadcasted_iota(I32, (tm, 1), 0)
    pooled_parts = []
    for g, w in enumerate(POOL_WINDOWS):
        c0, c1 = g * POOL_GROUP_DIM, (g + 1) * POOL_GROUP_DIM
        win = ext_s[POOL_HALO:POOL_HALO + tm, c0:c1]
        for sft in range(1, w):
            win = win + ext_s[POOL_HALO - sft:POOL_HALO - sft + tm, c0:c1]
        cnt = jnp.minimum(pos + 1, w).astype(F32)
        pin = win / cnt - ext_s[POOL_HALO:POOL_HALO + tm, c0:c1]
        pooled_parts.append(jnp.dot(pin.astype(BF16), pw_ref[g], preferred_element_type=F32))
    pooled = jnp.concatenate(pooled_parts, axis=1) * ps_ref[...]

    a = lax.dot_general(attn_ref[0], woa_ref[...], (((0,), (0,)), ((), ())),
                        preferred_element_type=F32)
    pm = jnp.dot(pooled.astype(BF16), wop_ref[...], preferred_element_type=F32)
    g0 = _sigmoid(gl_ref[:, 0:D_MODEL].astype(F32))
    g1 = _sigmoid(gl_ref[:, D_MODEL:2 * D_MODEL].astype(F32))
    merged = g0 * a + g1 * pm
    x1 = x_ref[...] + jnp.dot(merged.astype(BF16), wout_ref[...], preferred_element_type=F32)
    x1_ref[...] = x1

    hf = _rms(x1, gffn_ref[...])
    hfb = hf.astype(BF16)
    for s in range(ROW_TILES):
        hf_ref[pl.ds(s, tm, stride=ROW_TILES), :] = hfb[:, s * LANES:(s + 1) * LANES].astype(F32)

    logits = lax.dot_general(rwt_ref[...], hfb.astype(F32), _NT,
                             precision=lax.Precision.HIGHEST,
                             preferred_element_type=F32) + rb_ref[...]
    e_iota = lax.broadcasted_iota(I32, logits.shape, 0)
    vals, picks = [], []
    for _ in range(TOP_K):
        mx = jnp.max(logits, axis=0, keepdims=True)
        idx = jnp.min(jnp.where(logits == mx, e_iota, N_EXPERTS), axis=0, keepdims=True)
        pick = e_iota == idx
        vals.append(mx)
        picks.append(pick)
        idx_ref[len(picks) - 1:len(picks), :] = idx
        logits = jnp.where(pick, -jnp.inf, logits)
    ex = [jnp.exp(v - vals[0]) for v in vals]
    den = ex[0] + ex[1] + ex[2] + ex[3]
    for k in range(TOP_K):
        gate_ref[k:k + 1, :] = ex[k] / den

    onehot = jnp.zeros(e_iota.shape, F32)
    for pick in picks:
        onehot = onehot + pick.astype(F32)
    ra = lax.broadcasted_iota(I32, (tm, tm), 0)
    rb = lax.broadcasted_iota(I32, (tm, tm), 1)
    upper = (ra < rb).astype(BF16)
    before = jnp.dot(onehot.astype(BF16), upper, preferred_element_type=F32) + carry_s[...]
    for k, pick in enumerate(picks):
        rank_ref[k:k + 1, :] = jnp.sum(jnp.where(pick, before, 0.0), axis=0,
                                       keepdims=True).astype(I32)
    carry_s[...] = carry_s[...] + jnp.sum(onehot, axis=1, keepdims=True)
    cnt_ref[...] = jnp.broadcast_to(carry_s[...], cnt_ref.shape).astype(I32)


def _mix(x2, attn, u, gl, pw, ps, woa, wop, wout, gffn, rwt, rb, seq):
    t = x2.shape[0]
    tm = MIX_TM
    tiles_per_seq = seq // tm
    halo_per_tile = tm // POOL_HALO
    row = lambda i: (i, 0)
    fixed2 = lambda i: (0, 0)
    fixed3 = lambda i: (0, 0, 0)
    col = lambda i: (0, i)
    kern = functools.partial(_mix_kernel, tiles_per_seq=tiles_per_seq)
    return pl.pallas_call(
        kern,
        grid=(t // tm,),
        in_specs=[
            pl.BlockSpec((tm, D_MODEL), row),
            pl.BlockSpec((1, ATTN_WIDTH, tm), lambda i: (i // tiles_per_seq, 0, i % tiles_per_seq)),
            pl.BlockSpec((tm, POOL_WIDTH), row),
            pl.BlockSpec((POOL_HALO, POOL_WIDTH),
                         lambda i: (jnp.maximum(i * halo_per_tile - 1, 0), 0)),
            pl.BlockSpec((tm, 2 * D_MODEL), row),
            pl.BlockSpec((POOL_GROUPS, POOL_GROUP_DIM, POOL_GROUP_DIM), fixed3),
            pl.BlockSpec((1, POOL_WIDTH), fixed2),
            pl.BlockSpec((ATTN_WIDTH, D_MODEL), fixed2),
            pl.BlockSpec((POOL_WIDTH, D_MODEL), fixed2),
            pl.BlockSpec((D_MODEL, D_MODEL), fixed2),
            pl.BlockSpec((1, D_MODEL), fixed2),
            pl.BlockSpec((N_EXPERTS, D_MODEL), fixed2),
            pl.BlockSpec((N_EXPERTS, 1), fixed2),
        ],
        out_specs=[
            pl.BlockSpec((tm, D_MODEL), row),
            pl.BlockSpec((tm * ROW_TILES, LANES), row),
            pl.BlockSpec((TOP_K, tm), col),
            pl.BlockSpec((TOP_K, tm), col),
            pl.BlockSpec((TOP_K, tm), col),
            pl.BlockSpec((N_EXPERTS, LANES), fixed2),
        ],
        out_shape=[
            jax.ShapeDtypeStruct((t, D_MODEL), F32),
            jax.ShapeDtypeStruct((t * ROW_TILES, LANES), F32),
            jax.ShapeDtypeStruct((TOP_K, t), I32),
            jax.ShapeDtypeStruct((TOP_K, t), I32),
            jax.ShapeDtypeStruct((TOP_K, t), F32),
            jax.ShapeDtypeStruct((N_EXPERTS, LANES), I32),
        ],
        scratch_shapes=[
            pltpu.VMEM((POOL_HALO + tm, POOL_WIDTH), F32),
            pltpu.VMEM((N_EXPERTS, 1), F32),
        ],
        compiler_params=pltpu.CompilerParams(
            dimension_semantics=("arbitrary",), vmem_limit_bytes=VMEM_LIMIT),
        name="mix_router",
    )(x2, attn, u, u, gl, pw, ps, woa, wop, wout, gffn, rwt, rb)


def _row(ref, r):
    return ref.at[pl.ds(pl.multiple_of(r * ROW_TILES, ROW_TILES), ROW_TILES)]


def _pair_split_matrix():
    n = 2 * LANES
    r = lax.broadcasted_iota(I32, (n, n), 0)
    c = lax.broadcasted_iota(I32, (n, n), 1)
    src = jnp.where(c < LANES, 2 * c, 2 * (c - LANES) + 1)
    return (r == src).astype(BF16)


def _expert_kernel(be_ref, nused_ref, dest_ref, first_ref, par_ref, next_ref, padlo_ref, padhi_ref,
                   hf_ref, wgu_hbm, wd_hbm, bgu_ref, bd_ref,
                   ys_hbm, xbuf, ybuf, wgu_f, wd_f, wgu_s, wd_s, rowsrc_ref, sems, ysems, wsems,
                   *, n_blocks):
    tm = EXPERT_TM
    blk_rows = tm * ROW_TILES
    n_used = nused_ref[0]
    ahead = GATHER_DEPTH - 1

    def start_rows(blk, to_slot, r):
        src = pl.multiple_of(rowsrc_ref[blk * tm + r], ROW_TILES)
        pltpu.make_async_copy(hf_ref.at[pl.ds(src, ROW_TILES)], _row(xbuf.at[to_slot], r),
                              sems.at[to_slot]).start()

    def wait_rows(at_slot):
        pltpu.make_async_copy(hf_ref.at[pl.ds(0, blk_rows)], xbuf.at[at_slot],
                              sems.at[at_slot]).wait()

    def weight_copies(e, w_slot, start):
        for src, dst in ((wgu_hbm, wgu_f), (wd_hbm, wd_f)):
            cp = pltpu.make_async_copy(src.at[e], dst.at[w_slot], wsems.at[w_slot])
            if start:
                cp.start()
            else:
                cp.wait()

    def y_copy(blk, y_slot):
        dst = ys_hbm.at[pl.ds(pl.multiple_of(blk * blk_rows, blk_rows), blk_rows)]
        return pltpu.make_async_copy(ybuf.at[y_slot], dst, ysems.at[y_slot])

    weight_copies(be_ref[0], 0, True)

    def zero_range(e, carry):
        def zero(r, c):
            rowsrc_ref[r] = 0
            return c
        lax.fori_loop(padlo_ref[e], padhi_ref[e], zero, 0)
        return carry
    lax.fori_loop(0, N_EXPERTS + 1, zero_range, 0)

    def invert(t0, carry):
        for u in range(INVERT_UNROLL):
            tok = t0 * INVERT_UNROLL + u
            for k in range(TOP_K):
                rowsrc_ref[dest_ref[tok * TOP_K + k]] = tok * ROW_TILES
        return carry
    lax.fori_loop(0, dest_ref.shape[0] // (TOP_K * INVERT_UNROLL), invert, 0)

    def first_rows(r, carry):
        for blk in range(ahead):
            start_rows(blk, blk, r)
        return carry
    lax.fori_loop(0, tm, first_rows, 0)

    def block(i, carry):
        slot = i % GATHER_DEPTH
        y_slot = i % 2
        e = be_ref[i]

        @pl.when(first_ref[i] == 1)
        def _():
            par = par_ref[i]
            weight_copies(e, par, False)

            @pl.when(next_ref[i] >= 0)
            def _():
                weight_copies(next_ref[i], 1 - par, True)

            split = _pair_split_matrix()
            for c in range(2 * D_MODEL // (2 * LANES)):
                lo, hi = c * 2 * LANES, (c + 1) * 2 * LANES
                wgu_s[:, lo:hi] = jnp.dot(wgu_f[par, :, lo:hi].astype(BF16), split,
                                          preferred_element_type=F32).astype(BF16)
            wd_s[...] = wd_f[par].astype(BF16)

        @pl.when(i >= 2)
        def _():
            y_copy(i - 2, y_slot).wait()

        wait_rows(slot)
        xb = jnp.concatenate(
            [xbuf[slot, pl.ds(s, tm, stride=ROW_TILES), :] for s in range(ROW_TILES)],
            axis=1).astype(BF16)
        hid_parts = []
        n_chunks = D_MODEL // LANES
        assert n_chunks * GATHER_CHUNK == tm
        for c in range(n_chunks):
            lo, hi = c * 2 * LANES, (c + 1) * 2 * LANES
            gu = jnp.dot(xb, wgu_s[:, lo:hi], preferred_element_type=F32) + bgu_ref[e, :, lo:hi]
            glu = jnp.minimum(gu[:, :LANES], SWIGLU_LIMIT)
            lin = jnp.clip(gu[:, LANES:], -SWIGLU_LIMIT, SWIGLU_LIMIT)
            hid_parts.append(((lin + 1.0) * (glu * _sigmoid(SWIGLU_ALPHA * glu))).astype(BF16))
            for r in range(c * GATHER_CHUNK, (c + 1) * GATHER_CHUNK):
                start_rows(i + ahead, (i + ahead) % GATHER_DEPTH, r)
        hid = jnp.concatenate(hid_parts, axis=1)
        y = jnp.dot(hid, wd_s[...], preferred_element_type=F32) + bd_ref[e]
        for s in range(ROW_TILES):
            ybuf[y_slot, pl.ds(s, tm, stride=ROW_TILES), :] = y[:, s * LANES:(s + 1) * LANES]
        y_copy(i, y_slot).start()
        return carry

    lax.fori_loop(0, n_used, block, 0)

    def drain_rows(t, carry):
        wait_rows((n_used + t) % GATHER_DEPTH)
        return carry
    lax.fori_loop(0, ahead, drain_rows, 0)
    y_copy(n_used - 2, n_used % 2).wait()
    y_copy(n_used - 1, (n_used + 1) % 2).wait()

    ybuf[0] = jnp.zeros(ybuf.shape[1:], F32)

    def zero_tail(blk, carry):
        y_copy(blk, 0).start()
        return carry
    lax.fori_loop(n_used, n_blocks, zero_tail, 0)

    def wait_tail(blk, carry):
        y_copy(blk, 0).wait()
        return carry
    lax.fori_loop(n_used, n_blocks, wait_tail, 0)


def _experts(blk_expert, n_used, dest, first, parity, next_expert, pad_lo, pad_hi, hf_rows,
             wgu, wd, bgu_split, bd):
    n_blk = blk_expert.shape[0]
    tm = EXPERT_TM
    assert dest.shape[0] % (TOP_K * INVERT_UNROLL) == 0
    whole = lambda i, *_: (0, 0, 0)
    grid_spec = pltpu.PrefetchScalarGridSpec(
        num_scalar_prefetch=8,
        grid=(1,),
        in_specs=[
            pl.BlockSpec(memory_space=pl.ANY),
            pl.BlockSpec(memory_space=pl.ANY),
            pl.BlockSpec(memory_space=pl.ANY),
            pl.BlockSpec(bgu_split.shape, whole),
            pl.BlockSpec(bd.shape, whole),
        ],
        out_specs=pl.BlockSpec(memory_space=pl.ANY),
        scratch_shapes=[
            pltpu.VMEM((GATHER_DEPTH, tm * ROW_TILES, LANES), F32),
            pltpu.VMEM((2, tm * ROW_TILES, LANES), F32),
            pltpu.VMEM((2, D_MODEL, 2 * D_MODEL), F32),
            pltpu.VMEM((2, D_MODEL, D_MODEL), F32),
            pltpu.VMEM((D_MODEL, 2 * D_MODEL), BF16),
            pltpu.VMEM((D_MODEL, D_MODEL), BF16),
            pltpu.SMEM((n_blk * tm,), I32),
            pltpu.SemaphoreType.DMA((GATHER_DEPTH,)),
            pltpu.SemaphoreType.DMA((2,)),
            pltpu.SemaphoreType.DMA((2,)),
        ],
    )
    return pl.pallas_call(
        functools.partial(_expert_kernel, n_blocks=n_blk),
        grid_spec=grid_spec,
        out_shape=jax.ShapeDtypeStruct((n_blk * tm * ROW_TILES, LANES), F32),
        compiler_params=pltpu.CompilerParams(
            dimension_semantics=("arbitrary",), vmem_limit_bytes=VMEM_LIMIT),
        name="expert_ffn",
    )(blk_expert, n_used, dest, first, parity, next_expert, pad_lo, pad_hi, hf_rows, wgu, wd,
      bgu_split, bd)


def _combine_kernel(segsrc_ref, seglen_ref, segdst_ref, pos_ref,
                    x1_ref, gate_ref, ys_ref, p_ref, gple_ref, wpg_ref, wpp_ref, gfin_ref,
                    o_ref, segbuf, ybuf, sems):
    i = pl.program_id(0)
    n_steps = pl.num_programs(0)
    tm = OUT_TM
    slot = i % 2

    def fetch(tile, to_slot):
        for e in range(N_EXPERTS):
            n = pl.multiple_of(seglen_ref[tile * N_EXPERTS + e], ROW_TILES)
            src = pl.multiple_of(segsrc_ref[tile * N_EXPERTS + e], ROW_TILES)
            dst = pl.multiple_of(segdst_ref[tile * N_EXPERTS + e], ROW_TILES)

            @pl.when(n > 0)
            def _():
                pltpu.make_async_copy(ys_ref.at[pl.ds(src, n)],
                                      segbuf.at[to_slot, pl.ds(dst, n)], sems.at[to_slot]).start()

    @pl.when(i == 0)
    def _():
        fetch(0, 0)

    @pl.when(i + 1 < n_steps)
    def _():
        fetch(i + 1, 1 - slot)

    pltpu.make_async_copy(ys_ref.at[pl.ds(0, TOP_K * tm * ROW_TILES)], segbuf.at[slot],
                          sems.at[slot]).wait()

    for r in range(tm):
        for k in range(TOP_K):
            p = pl.multiple_of(pos_ref[(i * tm + r) * TOP_K + k], ROW_TILES)
            ybuf[(k * tm + r) * ROW_TILES:(k * tm + r + 1) * ROW_TILES, :] = (
                segbuf[slot, pl.ds(p, ROW_TILES), :])

    moe = None
    for k in range(TOP_K):
        yk = jnp.concatenate(
            [ybuf[pl.ds(k * tm * ROW_TILES + s, tm, stride=ROW_TILES), :]
             for s in range(ROW_TILES)], axis=1)
        term = gate_ref[:, k:k + 1] * yk
        moe = term if moe is None else moe + term
    x2 = x1_ref[...] + moe
    hp = _rms(x2, gple_ref[...]).astype(BF16)
    pg = _sigmoid(jnp.dot(hp, wpg_ref[...], preferred_element_type=F32))
    proj = jnp.dot(p_ref[...].astype(BF16), wpp_ref[...], preferred_element_type=F32)
    x3 = x2 + pg * proj
    o_ref[...] = _rms(x3, gfin_ref[...])


def _combine(seg_src, seg_len, seg_dst, pos, x1, gates_tk, ys, p2, gple, wpg, wpp, gfin):
    t = x1.shape[0]
    tm = OUT_TM
    ple = p2.shape[1]
    row = lambda i, *_: (i, 0)
    fixed = lambda i, *_: (0, 0)
    grid_spec = pltpu.PrefetchScalarGridSpec(
        num_scalar_prefetch=4,
        grid=(t // tm,),
        in_specs=[
            pl.BlockSpec((tm, D_MODEL), row),
            pl.BlockSpec((tm, TOP_K), row),
            pl.BlockSpec(memory_space=pl.ANY),
            pl.BlockSpec((tm, ple), row),
            pl.BlockSpec((1, D_MODEL), fixed),
            pl.BlockSpec((D_MODEL, D_MODEL), fixed),
            pl.BlockSpec((ple, D_MODEL), fixed),
            pl.BlockSpec((1, D_MODEL), fixed),
        ],
        out_specs=pl.BlockSpec((tm, D_MODEL), row),
        scratch_shapes=[
            pltpu.VMEM((2, TOP_K * tm * ROW_TILES, LANES), F32),
            pltpu.VMEM((TOP_K * tm * ROW_TILES, LANES), F32),
            pltpu.SemaphoreType.DMA((2,)),
        ],
    )
    return pl.pallas_call(
        _combine_kernel,
        grid_spec=grid_spec,
        out_shape=jax.ShapeDtypeStruct((t, D_MODEL), F32),
        compiler_params=pltpu.CompilerParams(
            dimension_semantics=("arbitrary",), vmem_limit_bytes=VMEM_LIMIT),
        name="combine_ple_final",
    )(seg_src, seg_len, seg_dst, pos, x1, gates_tk, ys, p2, gple, wpg, wpp, gfin)


def kernel(x, p, rel_bias, norm_mix_g, w_in, pool_w, pool_scale, w_o_attn, w_o_pool, w_out,
           norm_ffn_g, router_w, router_b, w_gate_up, b_gate_up, w_down, b_down,
           norm_ple_g, w_ple_gate, w_ple_proj, norm_final_g):
    b, s, d = x.shape
    depth = w_in.shape[0]
    t = b * s
    assert d == D_MODEL and s % MOBA_BLOCK == 0 and t % IN_TM == 0 and s % MIX_TM == 0
    assert depth == 1, "the final norm is fused into the layer's last kernel"
    n_pad = t * TOP_K + (N_EXPERTS + GATHER_DEPTH - 2) * EXPERT_TM
    n_blk = n_pad // EXPERT_TM
    bown, bprev, bfar = _t5_bias_tables(rel_bias)

    x2 = x.reshape(t, d)
    for i in range(depth):
        w_in_b = w_in[i].astype(BF16)
        wvt = w_in[i][:, 2 * ATTN_WIDTH:3 * ATTN_WIDTH].T.astype(BF16)
        q, k, vt, u, gl = _in_proj(x2, norm_mix_g[i].reshape(1, d), w_in_b, wvt, b, s)
        bh = b * N_HEADS
        attn_t = _attention(q.reshape(bh, s, HEAD_DIM), k.reshape(bh, s, HEAD_DIM),
                            vt.reshape(bh, V_ROWS, s), bown, bprev, bfar)
        attn = attn_t.reshape(b, ATTN_WIDTH, s)

        x1, hf_rows, idx_kt, rank_kt, gate_kt, cnt = _mix(
            x2, attn, u, gl, pool_w[i].astype(BF16), pool_scale[i].reshape(1, POOL_WIDTH),
            w_o_attn[i].astype(BF16), w_o_pool[i].astype(BF16), w_out[i].astype(BF16),
            norm_ffn_g[i].reshape(1, d), router_w[i].T, router_b[i].reshape(N_EXPERTS, 1), s)

        counts = cnt[:, 0]
        padded = (counts + EXPERT_TM - 1) // EXPERT_TM * EXPERT_TM
        pend = jnp.cumsum(padded)
        pstart = pend - padded
        e_ids = jnp.arange(N_EXPERTS, dtype=I32)
        pstart_of = jnp.sum(jnp.where(idx_kt[..., None] == e_ids, pstart, 0), axis=-1)
        dest_flat = (pstart_of + rank_kt).T.reshape(-1).astype(I32)
        blk_row0 = jnp.arange(n_blk, dtype=I32) * EXPERT_TM
        blk_expert = jnp.minimum(jnp.sum(pend[None, :] <= blk_row0[:, None], axis=1),
                                 N_EXPERTS - 1).astype(I32)
        n_used = (pend[-1:] // EXPERT_TM).astype(I32)
        pad_lo = jnp.concatenate([pstart + counts, pend[-1:]]).astype(I32)
        pad_hi = jnp.concatenate([pend, pend[-1:] + (GATHER_DEPTH - 1) * EXPERT_TM]).astype(I32)

        bgu = b_gate_up[i].reshape(N_EXPERTS, D_MODEL // LANES, LANES, 2).transpose(
            0, 1, 3, 2).reshape(N_EXPERTS, 1, 2 * D_MODEL)
        first = jnp.concatenate([jnp.ones((1,), I32),
                                 (blk_expert[1:] != blk_expert[:-1]).astype(I32)])
        parity = ((jnp.cumsum(first) - 1) % 2).astype(I32)
        later = jnp.logical_and(e_ids[None, :] > e_ids[:, None], (padded > 0)[None, :])
        next_e = jnp.min(jnp.where(later, e_ids[None, :], N_EXPERTS), axis=1)
        next_e = jnp.where(next_e == N_EXPERTS, -1, next_e).astype(I32)
        next_expert = jnp.sum(jnp.where(blk_expert[:, None] == e_ids, next_e, 0),
                              axis=1).astype(I32)
        ys = _experts(blk_expert, n_used, dest_flat, first, parity, next_expert, pad_lo, pad_hi,
                      hf_rows, w_gate_up[i], w_down[i], bgu, b_down[i][:, None, :])

        n_tiles = t // OUT_TM
        tile_cnt = jnp.sum(idx_kt.T.reshape(n_tiles, OUT_TM * TOP_K)[..., None] == e_ids,
                           axis=1).astype(I32)
        before = jnp.cumsum(tile_cnt, axis=0) - tile_cnt
        local_off = jnp.cumsum(tile_cnt, axis=1) - tile_cnt
        seg_src = ((pstart[None, :] + before) * ROW_TILES).reshape(-1).astype(I32)
        seg_len = (tile_cnt * ROW_TILES).reshape(-1).astype(I32)
        seg_dst = (local_off * ROW_TILES).reshape(-1).astype(I32)
        shift = jnp.repeat(local_off - before, OUT_TM, axis=0)
        pos = jnp.sum(jnp.where(idx_kt[..., None] == e_ids, shift[None], 0), axis=-1) + rank_kt
        pos = (pos.T.reshape(-1) * ROW_TILES).astype(I32)
        x2 = _combine(seg_src, seg_len, seg_dst, pos, x1, gate_kt.T, ys, p[i].reshape(t, -1),
                      norm_ple_g[i].reshape(1, d), w_ple_gate[i].astype(BF16),
                      w_ple_proj[i].astype(BF16), norm_final_g.reshape(1, d))
    return x2.reshape(b, s, d)
```

```python
import functools
import math

import jax
import jax.numpy as jnp
from jax import lax
from jax.experimental import pallas as pl
from jax.experimental.pallas import tpu as pltpu

F32 = jnp.float32
BF16 = jnp.bfloat16
I32 = jnp.int32

D_MODEL = 1024
N_HEADS = 8
HEAD_DIM = 64
ATTN_WIDTH = N_HEADS * HEAD_DIM
MOBA_BLOCK = 256
MOBA_TOPK = 3
NUM_BUCKETS = 32
MAX_DISTANCE = 128
POOL_WINDOWS = (2, 4, 8, 16)
POOL_GROUPS = 4
POOL_GROUP_DIM = 128
POOL_WIDTH = POOL_GROUPS * POOL_GROUP_DIM
N_EXPERTS = 32
TOP_K = 4
SWIGLU_LIMIT = 7.0
SWIGLU_ALPHA = 1.702
RMS_EPS = 1e-6

LANES = 128
BF16_SUBLANES = 16
V7X_VMEM_BYTES = 64 * 1024 * 1024
ROW_TILES = D_MODEL // LANES
V_ROWS = HEAD_DIM + BF16_SUBLANES

IN_TM = 512
MIX_TM = 512
EXPERT_TM = 256
OUT_TM = 512
ATTN_GROUP = 4
ATTN_HEADS = 8
GATHER_CHUNK = EXPERT_TM // (D_MODEL // LANES)
INVERT_UNROLL = 4
GATHER_DEPTH = 6
POOL_HALO = 16
MASKED = -1e30
LOG2E = math.log2(math.e)
Q_SCALE = HEAD_DIM ** -0.5 * LOG2E

VMEM_LIMIT = V7X_VMEM_BYTES * 7 // 8

_NT = (((1,), (1,)), ((), ()))


def _rms(x, g):
    ms = jnp.mean(x * x, axis=-1, keepdims=True)
    return x * lax.rsqrt(ms + RMS_EPS) * g


def _sigmoid(x):
    return 1.0 / (1.0 + jnp.exp(-x))


def _in_proj_kernel(x_ref, g_ref, w_ref, wvt_ref, q_ref, k_ref, vt_ref, u_ref, gl_ref):
    h = _rms(x_ref[...], g_ref[...]).astype(BF16)
    aw = ATTN_WIDTH

    def proj(lo, hi):
        return jnp.dot(h, w_ref[:, lo:hi], preferred_element_type=F32)

    qv = (proj(0, aw) * Q_SCALE).astype(BF16)
    kv = proj(aw, 2 * aw).astype(BF16)
    for hd in range(N_HEADS):
        q_ref[0, hd] = qv[:, hd * HEAD_DIM:(hd + 1) * HEAD_DIM]
        k_ref[0, hd] = kv[:, hd * HEAD_DIM:(hd + 1) * HEAD_DIM]
    vt = lax.dot_general(wvt_ref[...], h, _NT, preferred_element_type=F32).astype(BF16)
    tm = vt.shape[1]
    vt_ref[0, :, 0:HEAD_DIM, :] = vt.reshape(N_HEADS, HEAD_DIM, tm)
    extra = lax.broadcasted_iota(I32, (N_HEADS, V_ROWS - HEAD_DIM, tm), 1) == 0
    vt_ref[0, :, HEAD_DIM:V_ROWS, :] = extra.astype(BF16)
    u_ref[...] = proj(3 * aw, 3 * aw + POOL_WIDTH)
    base = 3 * aw + POOL_WIDTH
    for c in range(2):
        lo = base + c * D_MODEL
        gl_ref[:, c * D_MODEL:(c + 1) * D_MODEL] = proj(lo, lo + D_MODEL).astype(BF16)


def _in_proj(x2, g, w_bf16, wvt_bf16, batch, seq):
    t = x2.shape[0]
    in_cols = w_bf16.shape[1]
    tps = seq // IN_TM
    row = lambda i: (i, 0)
    fixed = lambda i: (0, 0)
    head_rows = lambda i: (i // tps, 0, i % tps, 0)
    head_cols = lambda i: (i // tps, 0, 0, i % tps)
    return pl.pallas_call(
        _in_proj_kernel,
        grid=(t // IN_TM,),
        in_specs=[
            pl.BlockSpec((IN_TM, D_MODEL), row),
            pl.BlockSpec((1, D_MODEL), fixed),
            pl.BlockSpec((D_MODEL, in_cols), fixed),
            pl.BlockSpec((ATTN_WIDTH, D_MODEL), fixed),
        ],
        out_specs=[
            pl.BlockSpec((1, N_HEADS, IN_TM, HEAD_DIM), head_rows),
            pl.BlockSpec((1, N_HEADS, IN_TM, HEAD_DIM), head_rows),
            pl.BlockSpec((1, N_HEADS, V_ROWS, IN_TM), head_cols),
            pl.BlockSpec((IN_TM, POOL_WIDTH), row),
            pl.BlockSpec((IN_TM, 2 * D_MODEL), row),
        ],
        out_shape=[
            jax.ShapeDtypeStruct((batch, N_HEADS, seq, HEAD_DIM), BF16),
            jax.ShapeDtypeStruct((batch, N_HEADS, seq, HEAD_DIM), BF16),
            jax.ShapeDtypeStruct((batch, N_HEADS, V_ROWS, seq), BF16),
            jax.ShapeDtypeStruct((t, POOL_WIDTH), F32),
            jax.ShapeDtypeStruct((t, 2 * D_MODEL), BF16),
        ],
        compiler_params=pltpu.CompilerParams(
            dimension_semantics=("parallel",), vmem_limit_bytes=VMEM_LIMIT),
        name="in_proj",
    )(x2, g, w_bf16, wvt_bf16)


def _attn_kernel(bfar_ref, q_ref, k_ref, vt_ref, bown_ref, bprev_ref, o_ref,
                 kmean_s, am_s, sa_s, sb_s, ga_s, gb_s, m_s, acc_s, *, n_blocks):
    hp = pl.program_id(0)
    j = pl.program_id(1)
    blk = MOBA_BLOCK
    grp = ATTN_GROUP
    gk = grp * blk
    n_groups = n_blocks // grp
    n_far = jnp.maximum(j - 1, 0)
    ng = (n_far + grp - 1) // grp
    jp = jnp.maximum(j - 1, 0)
    heads = range(ATTN_HEADS)

    @pl.when(j == 0)
    def _():
        for hh in heads:
            for n in range(n_blocks):
                kb = k_ref[hh, n * blk:(n + 1) * blk, :].astype(F32)
                kmean_s[hh, n:n + 1, :] = jnp.mean(kb, axis=0, keepdims=True)

    qs = [q_ref[hh] for hh in heads]

    def select(hh):
        gate = lax.dot_general(kmean_s[hh], qs[hh].astype(F32), _NT,
                               precision=lax.Precision.HIGHEST, preferred_element_type=F32)
        n_iota = lax.broadcasted_iota(I32, gate.shape, 0)
        past = n_iota < j
        gate = jnp.where(past, gate, jnp.finfo(F32).min)
        sel = jnp.zeros(gate.shape, jnp.bool_)
        for _ in range(MOBA_TOPK):
            mx = jnp.max(gate, axis=0, keepdims=True)
            idx = jnp.min(jnp.where(gate == mx, n_iota, n_blocks), axis=0, keepdims=True)
            pick = n_iota == idx
            sel = jnp.logical_or(sel, pick)
            gate = jnp.where(pick, -jnp.inf, gate)
        sel = jnp.logical_and(sel, past)
        bfar = bfar_ref[(hp * ATTN_HEADS + hh) % N_HEADS]
        am_s[hh] = jnp.where(jnp.logical_and(sel, n_iota < j - 1), bfar, MASKED)
        return jnp.max(jnp.where(jnp.logical_and(sel, n_iota == j - 1), 0.0, MASKED),
                       axis=0, keepdims=True)

    def produce(hh, g, s_ref, gmax_ref):
        g = jnp.minimum(g, n_groups - 1)
        kb = k_ref[hh, pl.ds(pl.multiple_of(g * gk, gk), gk), :]
        s = lax.dot_general(kb, qs[hh], _NT, preferred_element_type=F32)
        s_ref[hh] = s
        gmax = None
        for b in range(grp):
            mb = (jnp.max(s[b * blk:(b + 1) * blk], axis=0, keepdims=True)
                  + am_s[hh, pl.ds(g * grp + b, 1), :])
            gmax = mb if gmax is None else jnp.maximum(gmax, mb)
        gmax_ref[hh] = gmax

    def consume(hh, g, s_ref, gmax_ref):
        m_old = m_s[hh]
        m_new = jnp.maximum(m_old, gmax_ref[hh])
        alpha = jnp.exp2(m_old - m_new)
        parts = []
        for b in range(grp):
            shift = m_new - am_s[hh, pl.ds(g * grp + b, 1), :]
            pb = jnp.exp2(s_ref[hh, b * blk:(b + 1) * blk, :] - shift)
            parts.append(pb.astype(BF16))
        p = jnp.concatenate(parts, axis=0)
        vt = vt_ref[hh, :, pl.ds(pl.multiple_of(g * gk, gk), gk)]
        pv = jnp.dot(vt, p, preferred_element_type=F32)
        m_s[hh] = m_new
        acc_s[hh] = alpha * acc_s[hh] + pv

    def near_scores(hh, n, bias):
        kb = k_ref[hh, pl.ds(pl.multiple_of(n * blk, blk), blk), :]
        return lax.dot_general(kb, qs[hh], _NT, preferred_element_type=F32) + bias

    def near_pv(hh, n, p):
        vt = vt_ref[hh, :, pl.ds(pl.multiple_of(n * blk, blk), blk)]
        return jnp.dot(vt, p.astype(BF16), preferred_element_type=F32)

    for hh in heads:
        am_prev = select(hh)
        produce(hh, 0, sa_s, ga_s)
        s_own = near_scores(hh, j, bown_ref[hh])
        s_prev = near_scores(hh, jp, bprev_ref[hh])
        m = jnp.maximum(jnp.max(s_own, axis=0, keepdims=True),
                        jnp.max(s_prev, axis=0, keepdims=True) + am_prev)
        p_own = jnp.exp2(s_own - m)
        p_prev = jnp.exp2(s_prev - (m - am_prev))
        m_s[hh] = m
        acc_s[hh] = near_pv(hh, j, p_own) + near_pv(hh, jp, p_prev)

    def pair(h, carry):
        for hh in heads:
            consume(hh, 2 * h, sa_s, ga_s)
            produce(hh, 2 * h + 1, sb_s, gb_s)
        for hh in heads:
            consume(hh, 2 * h + 1, sb_s, gb_s)
            produce(hh, 2 * h + 2, sa_s, ga_s)
        return carry

    trips = jnp.maximum(ng - 1, 0) // 2
    lax.fori_loop(0, trips, pair, 0)
    g_last = 2 * trips

    @pl.when(ng - g_last == 1)
    def _():
        for hh in heads:
            consume(hh, g_last, sa_s, ga_s)

    @pl.when(ng - g_last == 2)
    def _():
        for hh in heads:
            consume(hh, g_last, sa_s, ga_s)
            produce(hh, g_last + 1, sb_s, gb_s)
        for hh in heads:
            consume(hh, g_last + 1, sb_s, gb_s)

    for hh in heads:
        acc = acc_s[hh]
        o_ref[hh] = (acc[0:HEAD_DIM] / acc[HEAD_DIM:HEAD_DIM + 1]).astype(o_ref.dtype)


def _attention(qh, kh, vth, bown, bprev, bfar):
    bhn, s, dh = qh.shape
    nb = s // MOBA_BLOCK
    ah = ATTN_HEADS
    assert N_HEADS % ah == 0 and nb % ATTN_GROUP == 0
    kern = functools.partial(_attn_kernel, n_blocks=nb)
    head_blk = lambda b, j: (b % (N_HEADS // ah), 0, 0)
    grid_spec = pltpu.PrefetchScalarGridSpec(
        num_scalar_prefetch=0,
        grid=(bhn // ah, nb),
        in_specs=[
            pl.BlockSpec(memory_space=pltpu.SMEM),
            pl.BlockSpec((ah, MOBA_BLOCK, dh), lambda b, j: (b, j, 0)),
            pl.BlockSpec((ah, s, dh), lambda b, j: (b, 0, 0), pipeline_mode=pl.Buffered(1)),
            pl.BlockSpec((ah, V_ROWS, s), lambda b, j: (b, 0, 0), pipeline_mode=pl.Buffered(1)),
            pl.BlockSpec((ah, MOBA_BLOCK, MOBA_BLOCK), head_blk, pipeline_mode=pl.Buffered(1)),
            pl.BlockSpec((ah, MOBA_BLOCK, MOBA_BLOCK), head_blk, pipeline_mode=pl.Buffered(1)),
        ],
        out_specs=pl.BlockSpec((ah, dh, MOBA_BLOCK), lambda b, j: (b, 0, j)),
        scratch_shapes=[
            pltpu.VMEM((ah, nb, dh), F32),
            pltpu.VMEM((ah, nb, MOBA_BLOCK), F32),
            pltpu.VMEM((ah, ATTN_GROUP * MOBA_BLOCK, MOBA_BLOCK), F32),
            pltpu.VMEM((ah, ATTN_GROUP * MOBA_BLOCK, MOBA_BLOCK), F32),
            pltpu.VMEM((ah, 1, MOBA_BLOCK), F32),
            pltpu.VMEM((ah, 1, MOBA_BLOCK), F32),
            pltpu.VMEM((ah, 1, MOBA_BLOCK), F32),
            pltpu.VMEM((ah, V_ROWS, MOBA_BLOCK), F32),
        ],
    )
    return pl.pallas_call(
        kern,
        grid_spec=grid_spec,
        out_shape=jax.ShapeDtypeStruct((bhn, dh, s), BF16),
        compiler_params=pltpu.CompilerParams(
            dimension_semantics=("parallel", "arbitrary"), vmem_limit_bytes=VMEM_LIMIT),
        name="moba_attention",
    )(bfar, qh, kh, vth, bown, bprev)


def _t5_bias_tables(rel_bias):
    blk = MOBA_BLOCK
    n = jnp.arange(2 * blk)
    max_exact = NUM_BUCKETS // 2
    nf = jnp.maximum(n, 1).astype(F32)
    large = max_exact + (jnp.log(nf / max_exact) / math.log(MAX_DISTANCE / max_exact)
                         * (NUM_BUCKETS - max_exact)).astype(I32)
    large = jnp.minimum(large, NUM_BUCKETS - 1)
    bucket = jnp.where(n < max_exact, n, large)
    rel_bias = rel_bias.astype(F32) * LOG2E
    tbl = rel_bias[bucket].T
    key = jnp.arange(blk)[:, None]
    qry = jnp.arange(blk)[None, :]
    d = qry - key
    h = tbl.shape[0]
    wide = 2 * blk + 1
    skew = jnp.broadcast_to(jnp.pad(tbl, ((0, 0), (0, 1)))[:, None, :], (h, blk, wide))
    skew = skew.reshape(h, blk * wide)[:, :blk * 2 * blk].reshape(h, blk, 2 * blk)
    bown = jnp.where(d >= 0, skew[:, :, :blk], MASKED)
    bprev = skew[:, :, blk:]
    bfar = rel_bias[NUM_BUCKETS - 1]
    return bown, bprev, bfar


def _mix_kernel(x_ref, attn_ref, u_ref, halo_ref, gl_ref, pw_ref, ps_ref, woa_ref, wop_ref,
                wout_ref, gffn_ref, rwt_ref, rb_ref,
                x1_ref, hf_ref, idx_ref, rank_ref, gate_ref, cnt_ref,
                ext_s, carry_s, *, tiles_per_seq):
    i = pl.program_id(0)
    tm = MIX_TM

    @pl.when(i == 0)
    def _():
        carry_s[...] = jnp.zeros_like(carry_s)

    first = (i % tiles_per_seq) == 0
    ext_s[0:POOL_HALO, :] = jnp.where(first, 0.0, halo_ref[...])
    ext_s[POOL_HALO:POOL_HALO + tm, :] = u_ref[...]
    pos = (i % tiles_per_seq) * tm + lax.broadcasted_iota(I32, (tm, 1), 0)
    pooled_parts = []
    for g, w in enumerate(POOL_WINDOWS):
        c0, c1 = g * POOL_GROUP_DIM, (g + 1) * POOL_GROUP_DIM
        win = ext_s[POOL_HALO:POOL_HALO + tm, c0:c1]
        for sft in range(1, w):
            win = win + ext_s[POOL_HALO - sft:POOL_HALO - sft + tm, c0:c1]
        cnt = jnp.minimum(pos + 1, w).astype(F32)
        pin = win / cnt - ext_s[POOL_HALO:POOL_HALO + tm, c0:c1]
        pooled_parts.append(jnp.dot(pin.astype(BF16), pw_ref[g], preferred_element_type=F32))
    pooled = jnp.concatenate(pooled_parts, axis=1) * ps_ref[...]

    a = lax.dot_general(attn_ref[0], woa_ref[...], (((0,), (0,)), ((), ())),
                        preferred_element_type=F32)
    pm = jnp.dot(pooled.astype(BF16), wop_ref[...], preferred_element_type=F32)
    g0 = _sigmoid(gl_ref[:, 0:D_MODEL].astype(F32))
    g1 = _sigmoid(gl_ref[:, D_MODEL:2 * D_MODEL].astype(F32))
    merged = g0 * a + g1 * pm
    x1 = x_ref[...] + jnp.dot(merged.astype(BF16), wout_ref[...], preferred_element_type=F32)
    x1_ref[...] = x1

    hf = _rms(x1, gffn_ref[...])
    hfb = hf.astype(BF16)
    for s in range(ROW_TILES):
        hf_ref[pl.ds(s, tm, stride=ROW_TILES), :] = hfb[:, s * LANES:(s + 1) * LANES].astype(F32)

    logits = lax.dot_general(rwt_ref[...], hfb.astype(F32), _NT,
                             precision=lax.Precision.HIGHEST,
                             preferred_element_type=F32) + rb_ref[...]
    e_iota = lax.broadcasted_iota(I32, logits.shape, 0)
    vals, picks = [], []
    for _ in range(TOP_K):
        mx = jnp.max(logits, axis=0, keepdims=True)
        idx = jnp.min(jnp.where(logits == mx, e_iota, N_EXPERTS), axis=0, keepdims=True)
        pick = e_iota == idx
        vals.append(mx)
        picks.append(pick)
        idx_ref[len(picks) - 1:len(picks), :] = idx
        logits = jnp.where(pick, -jnp.inf, logits)
    ex = [jnp.exp(v - vals[0]) for v in vals]
    den = ex[0] + ex[1] + ex[2] + ex[3]
    for k in range(TOP_K):
        gate_ref[k:k + 1, :] = ex[k] / den

    onehot = jnp.zeros(e_iota.shape, F32)
    for pick in picks:
        onehot = onehot + pick.astype(F32)
    ra = lax.broadcasted_iota(I32, (tm, tm), 0)
    rb = lax.broadcasted_iota(I32, (tm, tm), 1)
    upper = (ra < rb).astype(BF16)
    before = jnp.dot(onehot.astype(BF16), upper, preferred_element_type=F32) + carry_s[...]
    for k, pick in enumerate(picks):
        rank_ref[k:k + 1, :] = jnp.sum(jnp.where(pick, before, 0.0), axis=0,
                                       keepdims=True).astype(I32)
    carry_s[...] = carry_s[...] + jnp.sum(onehot, axis=1, keepdims=True)
    cnt_ref[...] = jnp.broadcast_to(carry_s[...], cnt_ref.shape).astype(I32)


def _mix(x2, attn, u, gl, pw, ps, woa, wop, wout, gffn, rwt, rb, seq):
    t = x2.shape[0]
    tm = MIX_TM
    tiles_per_seq = seq // tm
    halo_per_tile = tm // POOL_HALO
    row = lambda i: (i, 0)
    fixed2 = lambda i: (0, 0)
    fixed3 = lambda i: (0, 0, 0)
    col = lambda i: (0, i)
    kern = functools.partial(_mix_kernel, tiles_per_seq=tiles_per_seq)
    return pl.pallas_call(
        kern,
        grid=(t // tm,),
        in_specs=[
            pl.BlockSpec((tm, D_MODEL), row),
            pl.BlockSpec((1, ATTN_WIDTH, tm), lambda i: (i // tiles_per_seq, 0, i % tiles_per_seq)),
            pl.BlockSpec((tm, POOL_WIDTH), row),
            pl.BlockSpec((POOL_HALO, POOL_WIDTH),
                         lambda i: (jnp.maximum(i * halo_per_tile - 1, 0), 0)),
            pl.BlockSpec((tm, 2 * D_MODEL), row),
            pl.BlockSpec((POOL_GROUPS, POOL_GROUP_DIM, POOL_GROUP_DIM), fixed3),
            pl.BlockSpec((1, POOL_WIDTH), fixed2),
            pl.BlockSpec((ATTN_WIDTH, D_MODEL), fixed2),
            pl.BlockSpec((POOL_WIDTH, D_MODEL), fixed2),
            pl.BlockSpec((D_MODEL, D_MODEL), fixed2),
            pl.BlockSpec((1, D_MODEL), fixed2),
            pl.BlockSpec((N_EXPERTS, D_MODEL), fixed2),
            pl.BlockSpec((N_EXPERTS, 1), fixed2),
        ],
        out_specs=[
            pl.BlockSpec((tm, D_MODEL), row),
            pl.BlockSpec((tm * ROW_TILES, LANES), row),
            pl.BlockSpec((TOP_K, tm), col),
            pl.BlockSpec((TOP_K, tm), col),
            pl.BlockSpec((TOP_K, tm), col),
            pl.BlockSpec((N_EXPERTS, LANES), fixed2),
        ],
        out_shape=[
            jax.ShapeDtypeStruct((t, D_MODEL), F32),
            jax.ShapeDtypeStruct((t * ROW_TILES, LANES), F32),
            jax.ShapeDtypeStruct((TOP_K, t), I32),
            jax.ShapeDtypeStruct((TOP_K, t), I32),
            jax.ShapeDtypeStruct((TOP_K, t), F32),
            jax.ShapeDtypeStruct((N_EXPERTS, LANES), I32),
        ],
        scratch_shapes=[
            pltpu.VMEM((POOL_HALO + tm, POOL_WIDTH), F32),
            pltpu.VMEM((N_EXPERTS, 1), F32),
        ],
        compiler_params=pltpu.CompilerParams(
            dimension_semantics=("arbitrary",), vmem_limit_bytes=VMEM_LIMIT),
        name="mix_router",
    )(x2, attn, u, u, gl, pw, ps, woa, wop, wout, gffn, rwt, rb)


def _row(ref, r):
    return ref.at[pl.ds(pl.multiple_of(r * ROW_TILES, ROW_TILES), ROW_TILES)]


def _pair_split_matrix():
    n = 2 * LANES
    r = lax.broadcasted_iota(I32, (n, n), 0)
    c = lax.broadcasted_iota(I32, (n, n), 1)
    src = jnp.where(c < LANES, 2 * c, 2 * (c - LANES) + 1)
    return (r == src).astype(BF16)


def _expert_kernel(be_ref, nused_ref, dest_ref, first_ref, par_ref, next_ref, padlo_ref, padhi_ref,
                   hf_ref, wgu_hbm, wd_hbm, bgu_ref, bd_ref,
                   ys_hbm, xbuf, ybuf, wgu_f, wd_f, wgu_s, wd_s, rowsrc_ref, sems, ysems, wsems,
                   *, n_blocks):
    tm = EXPERT_TM
    blk_rows = tm * ROW_TILES
    n_used = nused_ref[0]
    ahead = GATHER_DEPTH - 1

    def start_rows(blk, to_slot, r):
        src = pl.multiple_of(rowsrc_ref[blk * tm + r], ROW_TILES)
        pltpu.make_async_copy(hf_ref.at[pl.ds(src, ROW_TILES)], _row(xbuf.at[to_slot], r),
                              sems.at[to_slot]).start()

    def wait_rows(at_slot):
        pltpu.make_async_copy(hf_ref.at[pl.ds(0, blk_rows)], xbuf.at[at_slot],
                              sems.at[at_slot]).wait()

    def weight_copies(e, w_slot, start):
        for src, dst in ((wgu_hbm, wgu_f), (wd_hbm, wd_f)):
            cp = pltpu.make_async_copy(src.at[e], dst.at[w_slot], wsems.at[w_slot])
            if start:
                cp.start()
            else:
                cp.wait()

    def y_copy(blk, y_slot):
        dst = ys_hbm.at[pl.ds(pl.multiple_of(blk * blk_rows, blk_rows), blk_rows)]
        return pltpu.make_async_copy(ybuf.at[y_slot], dst, ysems.at[y_slot])

    weight_copies(be_ref[0], 0, True)

    def zero_range(e, carry):
        def zero(r, c):
            rowsrc_ref[r] = 0
            return c
        lax.fori_loop(padlo_ref[e], padhi_ref[e], zero, 0)
        return carry
    lax.fori_loop(0, N_EXPERTS + 1, zero_range, 0)

    def invert(t0, carry):
        for u in range(INVERT_UNROLL):
            tok = t0 * INVERT_UNROLL + u
            for k in range(TOP_K):
                rowsrc_ref[dest_ref[tok * TOP_K + k]] = tok * ROW_TILES
        return carry
    lax.fori_loop(0, dest_ref.shape[0] // (TOP_K * INVERT_UNROLL), invert, 0)

    def first_rows(r, carry):
        for blk in range(ahead):
            start_rows(blk, blk, r)
        return carry
    lax.fori_loop(0, tm, first_rows, 0)

    def block(i, carry):
        slot = i % GATHER_DEPTH
        y_slot = i % 2
        e = be_ref[i]

        @pl.when(first_ref[i] == 1)
        def _():
            par = par_ref[i]
            weight_copies(e, par, False)

            @pl.when(next_ref[i] >= 0)
            def _():
                weight_copies(next_ref[i], 1 - par, True)

            split = _pair_split_matrix()
            for c in range(2 * D_MODEL // (2 * LANES)):
                lo, hi = c * 2 * LANES, (c + 1) * 2 * LANES
                wgu_s[:, lo:hi] = jnp.dot(wgu_f[par, :, lo:hi].astype(BF16), split,
                                          preferred_element_type=F32).astype(BF16)
            wd_s[...] = wd_f[par].astype(BF16)

        @pl.when(i >= 2)
        def _():
            y_copy(i - 2, y_slot).wait()

        wait_rows(slot)
        xb = jnp.concatenate(
            [xbuf[slot, pl.ds(s, tm, stride=ROW_TILES), :] for s in range(ROW_TILES)],
            axis=1).astype(BF16)
        hid_parts = []
        n_chunks = D_MODEL // LANES
        assert n_chunks * GATHER_CHUNK == tm
        for c in range(n_chunks):
            lo, hi = c * 2 * LANES, (c + 1) * 2 * LANES
            gu = jnp.dot(xb, wgu_s[:, lo:hi], preferred_element_type=F32) + bgu_ref[e, :, lo:hi]
            glu = jnp.minimum(gu[:, :LANES], SWIGLU_LIMIT)
            lin = jnp.clip(gu[:, LANES:], -SWIGLU_LIMIT, SWIGLU_LIMIT)
            hid_parts.append(((lin + 1.0) * (glu * _sigmoid(SWIGLU_ALPHA * glu))).astype(BF16))
            for r in range(c * GATHER_CHUNK, (c + 1) * GATHER_CHUNK):
                start_rows(i + ahead, (i + ahead) % GATHER_DEPTH, r)
        hid = jnp.concatenate(hid_parts, axis=1)
        y = jnp.dot(hid, wd_s[...], preferred_element_type=F32) + bd_ref[e]
        for s in range(ROW_TILES):
            ybuf[y_slot, pl.ds(s, tm, stride=ROW_TILES), :] = y[:, s * LANES:(s + 1) * LANES]
        y_copy(i, y_slot).start()
        return carry

    lax.fori_loop(0, n_used, block, 0)

    def drain_rows(t, carry):
        wait_rows((n_used + t) % GATHER_DEPTH)
        return carry
    lax.fori_loop(0, ahead, drain_rows, 0)
    y_copy(n_used - 2, n_used % 2).wait()
    y_copy(n_used - 1, (n_used + 1) % 2).wait()

    ybuf[0] = jnp.zeros(ybuf.shape[1:], F32)

    def zero_tail(blk, carry):
        y_copy(blk, 0).start()
        return carry
    lax.fori_loop(n_used, n_blocks, zero_tail, 0)

    def wait_tail(blk, carry):
        y_copy(blk, 0).wait()
        return carry
    lax.fori_loop(n_used, n_blocks, wait_tail, 0)


def _experts(blk_expert, n_used, dest, first, parity, next_expert, pad_lo, pad_hi, hf_rows,
             wgu, wd, bgu_split, bd):
    n_blk = blk_expert.shape[0]
    tm = EXPERT_TM
    assert dest.shape[0] % (TOP_K * INVERT_UNROLL) == 0
    whole = lambda i, *_: (0, 0, 0)
    grid_spec = pltpu.PrefetchScalarGridSpec(
        num_scalar_prefetch=8,
        grid=(1,),
        in_specs=[
            pl.BlockSpec(memory_space=pl.ANY),
            pl.BlockSpec(memory_space=pl.ANY),
            pl.BlockSpec(memory_space=pl.ANY),
            pl.BlockSpec(bgu_split.shape, whole),
            pl.BlockSpec(bd.shape, whole),
        ],
        out_specs=pl.BlockSpec(memory_space=pl.ANY),
        scratch_shapes=[
            pltpu.VMEM((GATHER_DEPTH, tm * ROW_TILES, LANES), F32),
            pltpu.VMEM((2, tm * ROW_TILES, LANES), F32),
            pltpu.VMEM((2, D_MODEL, 2 * D_MODEL), F32),
            pltpu.VMEM((2, D_MODEL, D_MODEL), F32),
            pltpu.VMEM((D_MODEL, 2 * D_MODEL), BF16),
            pltpu.VMEM((D_MODEL, D_MODEL), BF16),
            pltpu.SMEM((n_blk * tm,), I32),
            pltpu.SemaphoreType.DMA((GATHER_DEPTH,)),
            pltpu.SemaphoreType.DMA((2,)),
            pltpu.SemaphoreType.DMA((2,)),
        ],
    )
    return pl.pallas_call(
        functools.partial(_expert_kernel, n_blocks=n_blk),
        grid_spec=grid_spec,
        out_shape=jax.ShapeDtypeStruct((n_blk * tm * ROW_TILES, LANES), F32),
        compiler_params=pltpu.CompilerParams(
            dimension_semantics=("arbitrary",), vmem_limit_bytes=VMEM_LIMIT),
        name="expert_ffn",
    )(blk_expert, n_used, dest, first, parity, next_expert, pad_lo, pad_hi, hf_rows, wgu, wd,
      bgu_split, bd)


def _combine_kernel(segsrc_ref, seglen_ref, segdst_ref, pos_ref,
                    x1_ref, gate_ref, ys_ref, p_ref, gple_ref, wpg_ref, wpp_ref, gfin_ref,
                    o_ref, segbuf, ybuf, sems):
    i = pl.program_id(0)
    n_steps = pl.num_programs(0)
    tm = OUT_TM
    slot = i % 2

    def fetch(tile, to_slot):
        for e in range(N_EXPERTS):
            n = pl.multiple_of(seglen_ref[tile * N_EXPERTS + e], ROW_TILES)
            src = pl.multiple_of(segsrc_ref[tile * N_EXPERTS + e], ROW_TILES)
            dst = pl.multiple_of(segdst_ref[tile * N_EXPERTS + e], ROW_TILES)

            @pl.when(n > 0)
            def _():
                pltpu.make_async_copy(ys_ref.at[pl.ds(src, n)],
                                      segbuf.at[to_slot, pl.ds(dst, n)], sems.at[to_slot]).start()

    @pl.when(i == 0)
    def _():
        fetch(0, 0)

    @pl.when(i + 1 < n_steps)
    def _():
        fetch(i + 1, 1 - slot)

    pltpu.make_async_copy(ys_ref.at[pl.ds(0, TOP_K * tm * ROW_TILES)], segbuf.at[slot],
                          sems.at[slot]).wait()

    for r in range(tm):
        for k in range(TOP_K):
            p = pl.multiple_of(pos_ref[(i * tm + r) * TOP_K + k], ROW_TILES)
            ybuf[(k * tm + r) * ROW_TILES:(k * tm + r + 1) * ROW_TILES, :] = (
                segbuf[slot, pl.ds(p, ROW_TILES), :])

    moe = None
    for k in range(TOP_K):
        yk = jnp.concatenate(
            [ybuf[pl.ds(k * tm * ROW_TILES + s, tm, stride=ROW_TILES), :]
             for s in range(ROW_TILES)], axis=1)
        term = gate_ref[:, k:k + 1] * yk
        moe = term if moe is None else moe + term
    x2 = x1_ref[...] + moe
    hp = _rms(x2, gple_ref[...]).astype(BF16)
    pg = _sigmoid(jnp.dot(hp, wpg_ref[...], preferred_element_type=F32))
    proj = jnp.dot(p_ref[...].astype(BF16), wpp_ref[...], preferred_element_type=F32)
    x3 = x2 + pg * proj
    o_ref[...] = _rms(x3, gfin_ref[...])


def _combine(seg_src, seg_len, seg_dst, pos, x1, gates_tk, ys, p2, gple, wpg, wpp, gfin):
    t = x1.shape[0]
    tm = OUT_TM
    ple = p2.shape[1]
    row = lambda i, *_: (i, 0)
    fixed = lambda i, *_: (0, 0)
    grid_spec = pltpu.PrefetchScalarGridSpec(
        num_scalar_prefetch=4,
        grid=(t // tm,),
        in_specs=[
            pl.BlockSpec((tm, D_MODEL), row),
            pl.BlockSpec((tm, TOP_K), row),
            pl.BlockSpec(memory_space=pl.ANY),
            pl.BlockSpec((tm, ple), row),
            pl.BlockSpec((1, D_MODEL), fixed),
            pl.BlockSpec((D_MODEL, D_MODEL), fixed),
            pl.BlockSpec((ple, D_MODEL), fixed),
            pl.BlockSpec((1, D_MODEL), fixed),
        ],
        out_specs=pl.BlockSpec((tm, D_MODEL), row),
        scratch_shapes=[
            pltpu.VMEM((2, TOP_K * tm * ROW_TILES, LANES), F32),
            pltpu.VMEM((TOP_K * tm * ROW_TILES, LANES), F32),
            pltpu.SemaphoreType.DMA((2,)),
        ],
    )
    return pl.pallas_call(
        _combine_kernel,
        grid_spec=grid_spec,
        out_shape=jax.ShapeDtypeStruct((t, D_MODEL), F32),
        compiler_params=pltpu.CompilerParams(
            dimension_semantics=("arbitrary",), vmem_limit_bytes=VMEM_LIMIT),
        name="combine_ple_final",
    )(seg_src, seg_len, seg_dst, pos, x1, gates_tk, ys, p2, gple, wpg, wpp, gfin)


def kernel(x, p, rel_bias, norm_mix_g, w_in, pool_w, pool_scale, w_o_attn, w_o_pool, w_out,
           norm_ffn_g, router_w, router_b, w_gate_up, b_gate_up, w_down, b_down,
           norm_ple_g, w_ple_gate, w_ple_proj, norm_final_g):
    b, s, d = x.shape
    depth = w_in.shape[0]
    t = b * s
    assert d == D_MODEL and s % MOBA_BLOCK == 0 and t % IN_TM == 0 and s % MIX_TM == 0
    assert depth == 1, "the final norm is fused into the layer's last kernel"
    n_pad = t * TOP_K + (N_EXPERTS + GATHER_DEPTH - 2) * EXPERT_TM
    n_blk = n_pad // EXPERT_TM
    bown, bprev, bfar = _t5_bias_tables(rel_bias)

    x2 = x.reshape(t, d)
    for i in range(depth):
        w_in_b = w_in[i].astype(BF16)
        wvt = w_in[i][:, 2 * ATTN_WIDTH:3 * ATTN_WIDTH].T.astype(BF16)
        q, k, vt, u, gl = _in_proj(x2, norm_mix_g[i].reshape(1, d), w_in_b, wvt, b, s)
        bh = b * N_HEADS
        attn_t = _attention(q.reshape(bh, s, HEAD_DIM), k.reshape(bh, s, HEAD_DIM),
                            vt.reshape(bh, V_ROWS, s), bown, bprev, bfar)
        attn = attn_t.reshape(b, ATTN_WIDTH, s)

        x1, hf_rows, idx_kt, rank_kt, gate_kt, cnt = _mix(
            x2, attn, u, gl, pool_w[i].astype(BF16), pool_scale[i].reshape(1, POOL_WIDTH),
            w_o_attn[i].astype(BF16), w_o_pool[i].astype(BF16), w_out[i].astype(BF16),
            norm_ffn_g[i].reshape(1, d), router_w[i].T, router_b[i].reshape(N_EXPERTS, 1), s)

        counts = cnt[:, 0]
        padded = (counts + EXPERT_TM - 1) // EXPERT_TM * EXPERT_TM
        pend = jnp.cumsum(padded)
        pstart = pend - padded
        e_ids = jnp.arange(N_EXPERTS, dtype=I32)
        pstart_of = jnp.sum(jnp.where(idx_kt[..., None] == e_ids, pstart, 0), axis=-1)
        dest_flat = (pstart_of + rank_kt).T.reshape(-1).astype(I32)
        blk_row0 = jnp.arange(n_blk, dtype=I32) * EXPERT_TM
        blk_expert = jnp.minimum(jnp.sum(pend[None, :] <= blk_row0[:, None], axis=1),
                                 N_EXPERTS - 1).astype(I32)
        n_used = (pend[-1:] // EXPERT_TM).astype(I32)
        pad_lo = jnp.concatenate([pstart + counts, pend[-1:]]).astype(I32)
        pad_hi = jnp.concatenate([pend, pend[-1:] + (GATHER_DEPTH - 1) * EXPERT_TM]).astype(I32)

        bgu = b_gate_up[i].reshape(N_EXPERTS, D_MODEL // LANES, LANES, 2).transpose(
            0, 1, 3, 2).reshape(N_EXPERTS, 1, 2 * D_MODEL)
        first = jnp.concatenate([jnp.ones((1,), I32),
                                 (blk_expert[1:] != blk_expert[:-1]).astype(I32)])
        parity = ((jnp.cumsum(first) - 1) % 2).astype(I32)
        later = jnp.logical_and(e_ids[None, :] > e_ids[:, None], (padded > 0)[None, :])
        next_e = jnp.min(jnp.where(later, e_ids[None, :], N_EXPERTS), axis=1)
        next_e = jnp.where(next_e == N_EXPERTS, -1, next_e).astype(I32)
        next_expert = jnp.sum(jnp.where(blk_expert[:, None] == e_ids, next_e, 0),
                              axis=1).astype(I32)
        ys = _experts(blk_expert, n_used, dest_flat, first, parity, next_expert, pad_lo, pad_hi,
                      hf_rows, w_gate_up[i], w_down[i], bgu, b_down[i][:, None, :])

        n_tiles = t // OUT_TM
        tile_cnt = jnp.sum(idx_kt.T.reshape(n_tiles, OUT_TM * TOP_K)[..., None] == e_ids,
                           axis=1).astype(I32)
        before = jnp.cumsum(tile_cnt, axis=0) - tile_cnt
        local_off = jnp.cumsum(tile_cnt, axis=1) - tile_cnt
        seg_src = ((pstart[None, :] + before) * ROW_TILES).reshape(-1).astype(I32)
        seg_len = (tile_cnt * ROW_TILES).reshape(-1).astype(I32)
        seg_dst = (local_off * ROW_TILES).reshape(-1).astype(I32)
        shift = jnp.repeat(local_off - before, OUT_TM, axis=0)
        pos = jnp.sum(jnp.where(idx_kt[..., None] == e_ids, shift[None], 0), axis=-1) + rank_kt
        pos = (pos.T.reshape(-1) * ROW_TILES).astype(I32)
        x2 = _combine(seg_src, seg_len, seg_dst, pos, x1, gate_kt.T, ys, p[i].reshape(t, -1),
                      norm_ple_g[i].reshape(1, d), w_ple_gate[i].astype(BF16),
                      w_ple_proj[i].astype(BF16), norm_final_g.reshape(1, d))
    return x2.reshape(b, s, d)
```

```python
import functools
import math

import jax
import jax.numpy as jnp
from jax import lax
from jax.experimental import pallas as pl
from jax.experimental.pallas import tpu as pltpu

F32 = jnp.float32
BF16 = jnp.bfloat16
I32 = jnp.int32

D_MODEL = 1024
N_HEADS = 8
HEAD_DIM = 64
ATTN_WIDTH = N_HEADS * HEAD_DIM
MOBA_BLOCK = 256
MOBA_TOPK = 3
NUM_BUCKETS = 32
MAX_DISTANCE = 128
POOL_WINDOWS = (2, 4, 8, 16)
POOL_GROUPS = 4
POOL_GROUP_DIM = 128
POOL_WIDTH = POOL_GROUPS * POOL_GROUP_DIM
N_EXPERTS = 32
TOP_K = 4
SWIGLU_LIMIT = 7.0
SWIGLU_ALPHA = 1.702
RMS_EPS = 1e-6

LANES = 128
BF16_SUBLANES = 16
V7X_VMEM_BYTES = 64 * 1024 * 1024
ROW_TILES = D_MODEL // LANES
V_ROWS = HEAD_DIM + BF16_SUBLANES

IN_TM = 512
MIX_TM = 512
EXPERT_TM = 256
OUT_TM = 512
ATTN_GROUP = 4
ATTN_HEADS = 8
GATHER_CHUNK = EXPERT_TM // (D_MODEL // LANES)
INVERT_UNROLL = 4
GATHER_DEPTH = 6
POOL_HALO = 16
MASKED = -1e30
LOG2E = math.log2(math.e)
Q_SCALE = HEAD_DIM ** -0.5 * LOG2E

VMEM_LIMIT = V7X_VMEM_BYTES * 7 // 8

_NT = (((1,), (1,)), ((), ()))


def _rms(x, g):
    ms = jnp.mean(x * x, axis=-1, keepdims=True)
    return x * lax.rsqrt(ms + RMS_EPS) * g


def _sigmoid(x):
    return 1.0 / (1.0 + jnp.exp(-x))


def _in_proj_kernel(x_ref, g_ref, w_ref, wvt_ref, q_ref, k_ref, vt_ref, u_ref, gl_ref):
    h = _rms(x_ref[...], g_ref[...]).astype(BF16)
    aw = ATTN_WIDTH

    def proj(lo, hi):
        return jnp.dot(h, w_ref[:, lo:hi], preferred_element_type=F32)

    qv = (proj(0, aw) * Q_SCALE).astype(BF16)
    kv = proj(aw, 2 * aw).astype(BF16)
    for hd in range(N_HEADS):
        q_ref[0, hd] = qv[:, hd * HEAD_DIM:(hd + 1) * HEAD_DIM]
        k_ref[0, hd] = kv[:, hd * HEAD_DIM:(hd + 1) * HEAD_DIM]
    vt = lax.dot_general(wvt_ref[...], h, _NT, preferred_element_type=F32).astype(BF16)
    tm = vt.shape[1]
    vt_ref[0, :, 0:HEAD_DIM, :] = vt.reshape(N_HEADS, HEAD_DIM, tm)
    extra = lax.broadcasted_iota(I32, (N_HEADS, V_ROWS - HEAD_DIM, tm), 1) == 0
    vt_ref[0, :, HEAD_DIM:V_ROWS, :] = extra.astype(BF16)
    u_ref[...] = proj(3 * aw, 3 * aw + POOL_WIDTH)
    base = 3 * aw + POOL_WIDTH
    for c in range(2):
        lo = base + c * D_MODEL
        gl_ref[:, c * D_MODEL:(c + 1) * D_MODEL] = proj(lo, lo + D_MODEL).astype(BF16)


def _in_proj(x2, g, w_bf16, wvt_bf16, batch, seq):
    t = x2.shape[0]
    in_cols = w_bf16.shape[1]
    tps = seq // IN_TM
    row = lambda i: (i, 0)
    fixed = lambda i: (0, 0)
    head_rows = lambda i: (i // tps, 0, i % tps, 0)
    head_cols = lambda i: (i // tps, 0, 0, i % tps)
    return pl.pallas_call(
        _in_proj_kernel,
        grid=(t // IN_TM,),
        in_specs=[
            pl.BlockSpec((IN_TM, D_MODEL), row),
            pl.BlockSpec((1, D_MODEL), fixed),
            pl.BlockSpec((D_MODEL, in_cols), fixed),
            pl.BlockSpec((ATTN_WIDTH, D_MODEL), fixed),
        ],
        out_specs=[
            pl.BlockSpec((1, N_HEADS, IN_TM, HEAD_DIM), head_rows),
            pl.BlockSpec((1, N_HEADS, IN_TM, HEAD_DIM), head_rows),
            pl.BlockSpec((1, N_HEADS, V_ROWS, IN_TM), head_cols),
            pl.BlockSpec((IN_TM, POOL_WIDTH), row),
            pl.BlockSpec((IN_TM, 2 * D_MODEL), row),
        ],
        out_shape=[
            jax.ShapeDtypeStruct((batch, N_HEADS, seq, HEAD_DIM), BF16),
            jax.ShapeDtypeStruct((batch, N_HEADS, seq, HEAD_DIM), BF16),
            jax.ShapeDtypeStruct((batch, N_HEADS, V_ROWS, seq), BF16),
            jax.ShapeDtypeStruct((t, POOL_WIDTH), F32),
            jax.ShapeDtypeStruct((t, 2 * D_MODEL), BF16),
        ],
        compiler_params=pltpu.CompilerParams(
            dimension_semantics=("parallel",), vmem_limit_bytes=VMEM_LIMIT),
        name="in_proj",
    )(x2, g, w_bf16, wvt_bf16)


def _attn_kernel(bfar_ref, q_ref, k_ref, vt_ref, tbl_ref, o_ref,
                 kmean_s, bown_s, bprev_s, am_s, sa_s, sb_s, ga_s, gb_s, m_s, acc_s, *, n_blocks):
    hp = pl.program_id(0)
    j = pl.program_id(1)
    blk = MOBA_BLOCK
    grp = ATTN_GROUP
    gk = grp * blk
    n_groups = n_blocks // grp
    n_far = jnp.maximum(j - 1, 0)
    ng = (n_far + grp - 1) // grp
    jp = jnp.maximum(j - 1, 0)
    heads = range(ATTN_HEADS)

    @pl.when(j == 0)
    def _():
        key = lax.broadcasted_iota(I32, (blk, blk), 0)
        qry = lax.broadcasted_iota(I32, (blk, blk), 1)
        for hh in heads:
            for n in range(n_blocks):
                kb = k_ref[hh, n * blk:(n + 1) * blk, :].astype(F32)
                kmean_s[hh, n:n + 1, :] = jnp.mean(kb, axis=0, keepdims=True)
            by_dist = jnp.broadcast_to(tbl_ref[hh], (blk, 2 * blk))
            skew = pltpu.roll(by_dist, 0, 1, stride=1, stride_axis=0)
            bown_s[hh] = jnp.where(qry >= key, skew[:, 0:blk], MASKED)
            bprev_s[hh] = skew[:, blk:2 * blk]

    qs = [q_ref[hh] for hh in heads]

    def select(hh):
        gate = lax.dot_general(kmean_s[hh], qs[hh].astype(F32), _NT,
                               precision=lax.Precision.HIGHEST, preferred_element_type=F32)
        n_iota = lax.broadcasted_iota(I32, gate.shape, 0)
        past = n_iota < j
        gate = jnp.where(past, gate, jnp.finfo(F32).min)
        sel = jnp.zeros(gate.shape, jnp.bool_)
        for _ in range(MOBA_TOPK):
            mx = jnp.max(gate, axis=0, keepdims=True)
            idx = jnp.min(jnp.where(gate == mx, n_iota, n_blocks), axis=0, keepdims=True)
            pick = n_iota == idx
            sel = jnp.logical_or(sel, pick)
            gate = jnp.where(pick, -jnp.inf, gate)
        sel = jnp.logical_and(sel, past)
        bfar = bfar_ref[(hp * ATTN_HEADS + hh) % N_HEADS]
        am_s[hh] = jnp.where(jnp.logical_and(sel, n_iota < j - 1), bfar, MASKED)
        return jnp.max(jnp.where(jnp.logical_and(sel, n_iota == j - 1), 0.0, MASKED),
                       axis=0, keepdims=True)

    def produce(hh, g, s_ref, gmax_ref):
        g = jnp.minimum(g, n_groups - 1)
        kb = k_ref[hh, pl.ds(pl.multiple_of(g * gk, gk), gk), :]
        s = lax.dot_general(kb, qs[hh], _NT, preferred_element_type=F32)
        s_ref[hh] = s
        gmax = None
        for b in range(grp):
            mb = (jnp.max(s[b * blk:(b + 1) * blk], axis=0, keepdims=True)
                  + am_s[hh, pl.ds(g * grp + b, 1), :])
            gmax = mb if gmax is None else jnp.maximum(gmax, mb)
        gmax_ref[hh] = gmax

    def consume(hh, g, s_ref, gmax_ref):
        m_old = m_s[hh]
        m_new = jnp.maximum(m_old, gmax_ref[hh])
        alpha = jnp.exp2(m_old - m_new)
        parts = []
        for b in range(grp):
            shift = m_new - am_s[hh, pl.ds(g * grp + b, 1), :]
            pb = jnp.exp2(s_ref[hh, b * blk:(b + 1) * blk, :] - shift)
            parts.append(pb.astype(BF16))
        p = jnp.concatenate(parts, axis=0)
        vt = vt_ref[hh, :, pl.ds(pl.multiple_of(g * gk, gk), gk)]
        pv = jnp.dot(vt, p, preferred_element_type=F32)
        m_s[hh] = m_new
        acc_s[hh] = alpha * acc_s[hh] + pv

    def near_scores(hh, n, bias):
        kb = k_ref[hh, pl.ds(pl.multiple_of(n * blk, blk), blk), :]
        return lax.dot_general(kb, qs[hh], _NT, preferred_element_type=F32) + bias

    def near_pv(hh, n, p):
        vt = vt_ref[hh, :, pl.ds(pl.multiple_of(n * blk, blk), blk)]
        return jnp.dot(vt, p.astype(BF16), preferred_element_type=F32)

    for hh in heads:
        am_prev = select(hh)
        produce(hh, 0, sa_s, ga_s)
        s_own = near_scores(hh, j, bown_s[hh])
        s_prev = near_scores(hh, jp, bprev_s[hh])
        m = jnp.maximum(jnp.max(s_own, axis=0, keepdims=True),
                        jnp.max(s_prev, axis=0, keepdims=True) + am_prev)
        p_own = jnp.exp2(s_own - m)
        p_prev = jnp.exp2(s_prev - (m - am_prev))
        m_s[hh] = m
        acc_s[hh] = near_pv(hh, j, p_own) + near_pv(hh, jp, p_prev)

    def pair(h, carry):
        for hh in heads:
            consume(hh, 2 * h, sa_s, ga_s)
            produce(hh, 2 * h + 1, sb_s, gb_s)
        for hh in heads:
            consume(hh, 2 * h + 1, sb_s, gb_s)
            produce(hh, 2 * h + 2, sa_s, ga_s)
        return carry

    trips = jnp.maximum(ng - 1, 0) // 2
    lax.fori_loop(0, trips, pair, 0)
    g_last = 2 * trips

    @pl.when(ng - g_last == 1)
    def _():
        for hh in heads:
            consume(hh, g_last, sa_s, ga_s)

    @pl.when(ng - g_last == 2)
    def _():
        for hh in heads:
            consume(hh, g_last, sa_s, ga_s)
            produce(hh, g_last + 1, sb_s, gb_s)
        for hh in heads:
            consume(hh, g_last + 1, sb_s, gb_s)

    for hh in heads:
        acc = acc_s[hh]
        o_ref[hh] = (acc[0:HEAD_DIM] / acc[HEAD_DIM:HEAD_DIM + 1]).astype(o_ref.dtype)


def _attention(qh, kh, vth, tbl, bfar):
    bhn, s, dh = qh.shape
    nb = s // MOBA_BLOCK
    ah = ATTN_HEADS
    assert N_HEADS % ah == 0 and nb % ATTN_GROUP == 0
    kern = functools.partial(_attn_kernel, n_blocks=nb)
    head_blk = lambda b, j: (b % (N_HEADS // ah), 0, 0)
    grid_spec = pltpu.PrefetchScalarGridSpec(
        num_scalar_prefetch=0,
        grid=(bhn // ah, nb),
        in_specs=[
            pl.BlockSpec(memory_space=pltpu.SMEM),
            pl.BlockSpec((ah, MOBA_BLOCK, dh), lambda b, j: (b, j, 0)),
            pl.BlockSpec((ah, s, dh), lambda b, j: (b, 0, 0), pipeline_mode=pl.Buffered(1)),
            pl.BlockSpec((ah, V_ROWS, s), lambda b, j: (b, 0, 0), pipeline_mode=pl.Buffered(1)),
            pl.BlockSpec((ah, 1, 2 * MOBA_BLOCK), head_blk),
        ],
        out_specs=pl.BlockSpec((ah, dh, MOBA_BLOCK), lambda b, j: (b, 0, j)),
        scratch_shapes=[
            pltpu.VMEM((ah, nb, dh), F32),
            pltpu.VMEM((ah, MOBA_BLOCK, MOBA_BLOCK), F32),
            pltpu.VMEM((ah, MOBA_BLOCK, MOBA_BLOCK), F32),
            pltpu.VMEM((ah, nb, MOBA_BLOCK), F32),
            pltpu.VMEM((ah, ATTN_GROUP * MOBA_BLOCK, MOBA_BLOCK), F32),
            pltpu.VMEM((ah, ATTN_GROUP * MOBA_BLOCK, MOBA_BLOCK), F32),
            pltpu.VMEM((ah, 1, MOBA_BLOCK), F32),
            pltpu.VMEM((ah, 1, MOBA_BLOCK), F32),
            pltpu.VMEM((ah, 1, MOBA_BLOCK), F32),
            pltpu.VMEM((ah, V_ROWS, MOBA_BLOCK), F32),
        ],
    )
    return pl.pallas_call(
        kern,
        grid_spec=grid_spec,
        out_shape=jax.ShapeDtypeStruct((bhn, dh, s), BF16),
        compiler_params=pltpu.CompilerParams(
            dimension_semantics=("parallel", "arbitrary"), vmem_limit_bytes=VMEM_LIMIT),
        name="moba_attention",
    )(bfar, qh, kh, vth, tbl)


def _t5_bias_tables(rel_bias):
    blk = MOBA_BLOCK
    n = jnp.arange(2 * blk)
    max_exact = NUM_BUCKETS // 2
    nf = jnp.maximum(n, 1).astype(F32)
    large = max_exact + (jnp.log(nf / max_exact) / math.log(MAX_DISTANCE / max_exact)
                         * (NUM_BUCKETS - max_exact)).astype(I32)
    large = jnp.minimum(large, NUM_BUCKETS - 1)
    bucket = jnp.where(n < max_exact, n, large)
    rel_bias = rel_bias.astype(F32) * LOG2E
    tbl = rel_bias[bucket].T[:, None, :]
    bfar = rel_bias[NUM_BUCKETS - 1]
    return tbl, bfar


def _mix_kernel(x_ref, attn_ref, u_ref, halo_ref, gl_ref, pw_ref, ps_ref, woa_ref, wop_ref,
                wout_ref, gffn_ref, rwt_ref, rb_ref,
                x1_ref, hf_ref, idx_ref, rank_ref, gate_ref, cnt_ref,
                ext_s, carry_s, *, tiles_per_seq):
    i = pl.program_id(0)
    tm = MIX_TM

    @pl.when(i == 0)
    def _():
        carry_s[...] = jnp.zeros_like(carry_s)

    first = (i % tiles_per_seq) == 0
    ext_s[0:POOL_HALO, :] = jnp.where(first, 0.0, halo_ref[...])
    ext_s[POOL_HALO:POOL_HALO + tm, :] = u_ref[...]
    pos = (i % tiles_per_seq) * tm + lax.broadcasted_iota(I32, (tm, 1), 0)
    pooled_parts = []
    for g, w in enumerate(POOL_WINDOWS):
        c0, c1 = g * POOL_GROUP_DIM, (g + 1) * POOL_GROUP_DIM
        win = ext_s[POOL_HALO:POOL_HALO + tm, c0:c1]
        for sft in range(1, w):
            win = win + ext_s[POOL_HALO - sft:POOL_HALO - sft + tm, c0:c1]
        cnt = jnp.minimum(pos + 1, w).astype(F32)
        pin = win / cnt - ext_s[POOL_HALO:POOL_HALO + tm, c0:c1]
        pooled_parts.append(jnp.dot(pin.astype(BF16), pw_ref[g], preferred_element_type=F32))
    pooled = jnp.concatenate(pooled_parts, axis=1) * ps_ref[...]

    a = lax.dot_general(attn_ref[0], woa_ref[...], (((0,), (0,)), ((), ())),
                        preferred_element_type=F32)
    pm = jnp.dot(pooled.astype(BF16), wop_ref[...], preferred_element_type=F32)
    g0 = _sigmoid(gl_ref[:, 0:D_MODEL].astype(F32))
    g1 = _sigmoid(gl_ref[:, D_MODEL:2 * D_MODEL].astype(F32))
    merged = g0 * a + g1 * pm
    x1 = x_ref[...] + jnp.dot(merged.astype(BF16), wout_ref[...], preferred_element_type=F32)
    x1_ref[...] = x1

    hf = _rms(x1, gffn_ref[...])
    hfb = hf.astype(BF16)
    for s in range(ROW_TILES):
        hf_ref[pl.ds(s, tm, stride=ROW_TILES), :] = hfb[:, s * LANES:(s + 1) * LANES].astype(F32)

    logits = lax.dot_general(rwt_ref[...], hfb.astype(F32), _NT,
                             precision=lax.Precision.HIGHEST,
                             preferred_element_type=F32) + rb_ref[...]
    e_iota = lax.broadcasted_iota(I32, logits.shape, 0)
    vals, picks = [], []
    for _ in range(TOP_K):
        mx = jnp.max(logits, axis=0, keepdims=True)
        idx = jnp.min(jnp.where(logits == mx, e_iota, N_EXPERTS), axis=0, keepdims=True)
        pick = e_iota == idx
        vals.append(mx)
        picks.append(pick)
        idx_ref[len(picks) - 1:len(picks), :] = idx
        logits = jnp.where(pick, -jnp.inf, logits)
    ex = [jnp.exp(v - vals[0]) for v in vals]
    den = ex[0] + ex[1] + ex[2] + ex[3]
    for k in range(TOP_K):
        gate_ref[k:k + 1, :] = ex[k] / den

    onehot = jnp.zeros(e_iota.shape, F32)
    for pick in picks:
        onehot = onehot + pick.astype(F32)
    ra = lax.broadcasted_iota(I32, (tm, tm), 0)
    rb = lax.broadcasted_iota(I32, (tm, tm), 1)
    upper = (ra < rb).astype(BF16)
    before = jnp.dot(onehot.astype(BF16), upper, preferred_element_type=F32) + carry_s[...]
    for k, pick in enumerate(picks):
        rank_ref[k:k + 1, :] = jnp.sum(jnp.where(pick, before, 0.0), axis=0,
                                       keepdims=True).astype(I32)
    carry_s[...] = carry_s[...] + jnp.sum(onehot, axis=1, keepdims=True)
    cnt_ref[...] = jnp.broadcast_to(carry_s[...], cnt_ref.shape).astype(I32)


def _mix(x2, attn, u, gl, pw, ps, woa, wop, wout, gffn, rwt, rb, seq):
    t = x2.shape[0]
    tm = MIX_TM
    tiles_per_seq = seq // tm
    halo_per_tile = tm // POOL_HALO
    row = lambda i: (i, 0)
    fixed2 = lambda i: (0, 0)
    fixed3 = lambda i: (0, 0, 0)
    col = lambda i: (0, i)
    kern = functools.partial(_mix_kernel, tiles_per_seq=tiles_per_seq)
    return pl.pallas_call(
        kern,
        grid=(t // tm,),
        in_specs=[
            pl.BlockSpec((tm, D_MODEL), row),
            pl.BlockSpec((1, ATTN_WIDTH, tm), lambda i: (i // tiles_per_seq, 0, i % tiles_per_seq)),
            pl.BlockSpec((tm, POOL_WIDTH), row),
            pl.BlockSpec((POOL_HALO, POOL_WIDTH),
                         lambda i: (jnp.maximum(i * halo_per_tile - 1, 0), 0)),
            pl.BlockSpec((tm, 2 * D_MODEL), row),
            pl.BlockSpec((POOL_GROUPS, POOL_GROUP_DIM, POOL_GROUP_DIM), fixed3),
            pl.BlockSpec((1, POOL_WIDTH), fixed2),
            pl.BlockSpec((ATTN_WIDTH, D_MODEL), fixed2),
            pl.BlockSpec((POOL_WIDTH, D_MODEL), fixed2),
            pl.BlockSpec((D_MODEL, D_MODEL), fixed2),
            pl.BlockSpec((1, D_MODEL), fixed2),
            pl.BlockSpec((N_EXPERTS, D_MODEL), fixed2),
            pl.BlockSpec((N_EXPERTS, 1), fixed2),
        ],
        out_specs=[
            pl.BlockSpec((tm, D_MODEL), row),
            pl.BlockSpec((tm * ROW_TILES, LANES), row),
            pl.BlockSpec((TOP_K, tm), col),
            pl.BlockSpec((TOP_K, tm), col),
            pl.BlockSpec((TOP_K, tm), col),
            pl.BlockSpec((N_EXPERTS, LANES), fixed2),
        ],
        out_shape=[
            jax.ShapeDtypeStruct((t, D_MODEL), F32),
            jax.ShapeDtypeStruct((t * ROW_TILES, LANES), F32),
            jax.ShapeDtypeStruct((TOP_K, t), I32),
            jax.ShapeDtypeStruct((TOP_K, t), I32),
            jax.ShapeDtypeStruct((TOP_K, t), F32),
            jax.ShapeDtypeStruct((N_EXPERTS, LANES), I32),
        ],
        scratch_shapes=[
            pltpu.VMEM((POOL_HALO + tm, POOL_WIDTH), F32),
            pltpu.VMEM((N_EXPERTS, 1), F32),
        ],
        compiler_params=pltpu.CompilerParams(
            dimension_semantics=("arbitrary",), vmem_limit_bytes=VMEM_LIMIT),
        name="mix_router",
    )(x2, attn, u, u, gl, pw, ps, woa, wop, wout, gffn, rwt, rb)


def _row(ref, r):
    return ref.at[pl.ds(pl.multiple_of(r * ROW_TILES, ROW_TILES), ROW_TILES)]


def _pair_split_matrix():
    n = 2 * LANES
    r = lax.broadcasted_iota(I32, (n, n), 0)
    c = lax.broadcasted_iota(I32, (n, n), 1)
    src = jnp.where(c < LANES, 2 * c, 2 * (c - LANES) + 1)
    return (r == src).astype(BF16)


def _expert_kernel(be_ref, nused_ref, dest_ref, first_ref, par_ref, next_ref, padlo_ref, padhi_ref,
                   hf_ref, wgu_hbm, wd_hbm, bgu_ref, bd_ref,
                   ys_hbm, xbuf, ybuf, wgu_f, wd_f, wgu_s, wd_s, rowsrc_ref, sems, ysems, wsems,
                   *, n_blocks):
    tm = EXPERT_TM
    blk_rows = tm * ROW_TILES
    n_used = nused_ref[0]
    ahead = GATHER_DEPTH - 1

    def start_rows(blk, to_slot, r):
        src = pl.multiple_of(rowsrc_ref[blk * tm + r], ROW_TILES)
        pltpu.make_async_copy(hf_ref.at[pl.ds(src, ROW_TILES)], _row(xbuf.at[to_slot], r),
                              sems.at[to_slot]).start()

    def wait_rows(at_slot):
        pltpu.make_async_copy(hf_ref.at[pl.ds(0, blk_rows)], xbuf.at[at_slot],
                              sems.at[at_slot]).wait()

    def weight_copies(e, w_slot, start):
        for src, dst in ((wgu_hbm, wgu_f), (wd_hbm, wd_f)):
            cp = pltpu.make_async_copy(src.at[e], dst.at[w_slot], wsems.at[w_slot])
            if start:
                cp.start()
            else:
                cp.wait()

    def y_copy(blk, y_slot):
        dst = ys_hbm.at[pl.ds(pl.multiple_of(blk * blk_rows, blk_rows), blk_rows)]
        return pltpu.make_async_copy(ybuf.at[y_slot], dst, ysems.at[y_slot])

    weight_copies(be_ref[0], 0, True)

    def zero_range(e, carry):
        def zero(r, c):
            rowsrc_ref[r] = 0
            return c
        lax.fori_loop(padlo_ref[e], padhi_ref[e], zero, 0)
        return carry
    lax.fori_loop(0, N_EXPERTS + 1, zero_range, 0)

    def invert(t0, carry):
        for u in range(INVERT_UNROLL):
            tok = t0 * INVERT_UNROLL + u
            for k in range(TOP_K):
                rowsrc_ref[dest_ref[tok * TOP_K + k]] = tok * ROW_TILES
        return carry
    lax.fori_loop(0, dest_ref.shape[0] // (TOP_K * INVERT_UNROLL), invert, 0)

    def first_rows(r, carry):
        for blk in range(ahead):
            start_rows(blk, blk, r)
        return carry
    lax.fori_loop(0, tm, first_rows, 0)

    def block(i, carry):
        slot = i % GATHER_DEPTH
        y_slot = i % 2
        e = be_ref[i]

        @pl.when(first_ref[i] == 1)
        def _():
            par = par_ref[i]
            weight_copies(e, par, False)

            @pl.when(next_ref[i] >= 0)
            def _():
                weight_copies(next_ref[i], 1 - par, True)

            split = _pair_split_matrix()
            for c in range(2 * D_MODEL // (2 * LANES)):
                lo, hi = c * 2 * LANES, (c + 1) * 2 * LANES
                wgu_s[:, lo:hi] = jnp.dot(wgu_f[par, :, lo:hi].astype(BF16), split,
                                          preferred_element_type=F32).astype(BF16)
            wd_s[...] = wd_f[par].astype(BF16)

        @pl.when(i >= 2)
        def _():
            y_copy(i - 2, y_slot).wait()

        wait_rows(slot)
        xb = jnp.concatenate(
            [xbuf[slot, pl.ds(s, tm, stride=ROW_TILES), :] for s in range(ROW_TILES)],
            axis=1).astype(BF16)
        hid_parts = []
        n_chunks = D_MODEL // LANES
        assert n_chunks * GATHER_CHUNK == tm
        for c in range(n_chunks):
            lo, hi = c * 2 * LANES, (c + 1) * 2 * LANES
            gu = jnp.dot(xb, wgu_s[:, lo:hi], preferred_element_type=F32) + bgu_ref[e, :, lo:hi]
            glu = jnp.minimum(gu[:, :LANES], SWIGLU_LIMIT)
            lin = jnp.clip(gu[:, LANES:], -SWIGLU_LIMIT, SWIGLU_LIMIT)
            hid_parts.append(((lin + 1.0) * (glu * _sigmoid(SWIGLU_ALPHA * glu))).astype(BF16))
            for r in range(c * GATHER_CHUNK, (c + 1) * GATHER_CHUNK):
                start_rows(i + ahead, (i + ahead) % GATHER_DEPTH, r)
        hid = jnp.concatenate(hid_parts, axis=1)
        y = jnp.dot(hid, wd_s[...], preferred_element_type=F32) + bd_ref[e]
        for s in range(ROW_TILES):
            ybuf[y_slot, pl.ds(s, tm, stride=ROW_TILES), :] = y[:, s * LANES:(s + 1) * LANES]
        y_copy(i, y_slot).start()
        return carry

    lax.fori_loop(0, n_used, block, 0)

    def drain_rows(t, carry):
        wait_rows((n_used + t) % GATHER_DEPTH)
        return carry
    lax.fori_loop(0, ahead, drain_rows, 0)
    y_copy(n_used - 2, n_used % 2).wait()
    y_copy(n_used - 1, (n_used + 1) % 2).wait()

    ybuf[0] = jnp.zeros(ybuf.shape[1:], F32)

    def zero_tail(blk, carry):
        y_copy(blk, 0).start()
        return carry
    lax.fori_loop(n_used, n_blocks, zero_tail, 0)

    def wait_tail(blk, carry):
        y_copy(blk, 0).wait()
        return carry
    lax.fori_loop(n_used, n_blocks, wait_tail, 0)


def _experts(blk_expert, n_used, dest, first, parity, next_expert, pad_lo, pad_hi, hf_rows,
             wgu, wd, bgu_split, bd):
    n_blk = blk_expert.shape[0]
    tm = EXPERT_TM
    assert dest.shape[0] % (TOP_K * INVERT_UNROLL) == 0
    whole = lambda i, *_: (0, 0, 0)
    grid_spec = pltpu.PrefetchScalarGridSpec(
        num_scalar_prefetch=8,
        grid=(1,),
        in_specs=[
            pl.BlockSpec(memory_space=pl.ANY),
            pl.BlockSpec(memory_space=pl.ANY),
            pl.BlockSpec(memory_space=pl.ANY),
            pl.BlockSpec(bgu_split.shape, whole),
            pl.BlockSpec(bd.shape, whole),
        ],
        out_specs=pl.BlockSpec(memory_space=pl.ANY),
        scratch_shapes=[
            pltpu.VMEM((GATHER_DEPTH, tm * ROW_TILES, LANES), F32),
            pltpu.VMEM((2, tm * ROW_TILES, LANES), F32),
            pltpu.VMEM((2, D_MODEL, 2 * D_MODEL), F32),
            pltpu.VMEM((2, D_MODEL, D_MODEL), F32),
            pltpu.VMEM((D_MODEL, 2 * D_MODEL), BF16),
            pltpu.VMEM((D_MODEL, D_MODEL), BF16),
            pltpu.SMEM((n_blk * tm,), I32),
            pltpu.SemaphoreType.DMA((GATHER_DEPTH,)),
            pltpu.SemaphoreType.DMA((2,)),
            pltpu.SemaphoreType.DMA((2,)),
        ],
    )
    return pl.pallas_call(
        functools.partial(_expert_kernel, n_blocks=n_blk),
        grid_spec=grid_spec,
        out_shape=jax.ShapeDtypeStruct((n_blk * tm * ROW_TILES, LANES), F32),
        compiler_params=pltpu.CompilerParams(
            dimension_semantics=("arbitrary",), vmem_limit_bytes=VMEM_LIMIT),
        name="expert_ffn",
    )(blk_expert, n_used, dest, first, parity, next_expert, pad_lo, pad_hi, hf_rows, wgu, wd,
      bgu_split, bd)


def _combine_kernel(segsrc_ref, seglen_ref, segdst_ref, pos_ref,
                    x1_ref, gate_ref, ys_ref, p_ref, gple_ref, wpg_ref, wpp_ref, gfin_ref,
                    o_ref, segbuf, ybuf, sems):
    i = pl.program_id(0)
    n_steps = pl.num_programs(0)
    tm = OUT_TM
    slot = i % 2

    def fetch(tile, to_slot):
        for e in range(N_EXPERTS):
            n = pl.multiple_of(seglen_ref[tile * N_EXPERTS + e], ROW_TILES)
            src = pl.multiple_of(segsrc_ref[tile * N_EXPERTS + e], ROW_TILES)
            dst = pl.multiple_of(segdst_ref[tile * N_EXPERTS + e], ROW_TILES)

            @pl.when(n > 0)
            def _():
                pltpu.make_async_copy(ys_ref.at[pl.ds(src, n)],
                                      segbuf.at[to_slot, pl.ds(dst, n)], sems.at[to_slot]).start()

    @pl.when(i == 0)
    def _():
        fetch(0, 0)

    @pl.when(i + 1 < n_steps)
    def _():
        fetch(i + 1, 1 - slot)

    pltpu.make_async_copy(ys_ref.at[pl.ds(0, TOP_K * tm * ROW_TILES)], segbuf.at[slot],
                          sems.at[slot]).wait()

    for r in range(tm):
        for k in range(TOP_K):
            p = pl.multiple_of(pos_ref[(i * tm + r) * TOP_K + k], ROW_TILES)
            ybuf[(k * tm + r) * ROW_TILES:(k * tm + r + 1) * ROW_TILES, :] = (
                segbuf[slot, pl.ds(p, ROW_TILES), :])

    moe = None
    for k in range(TOP_K):
        yk = jnp.concatenate(
            [ybuf[pl.ds(k * tm * ROW_TILES + s, tm, stride=ROW_TILES), :]
             for s in range(ROW_TILES)], axis=1)
        term = gate_ref[:, k:k + 1] * yk
        moe = term if moe is None else moe + term
    x2 = x1_ref[...] + moe
    hp = _rms(x2, gple_ref[...]).astype(BF16)
    pg = _sigmoid(jnp.dot(hp, wpg_ref[...], preferred_element_type=F32))
    proj = jnp.dot(p_ref[...].astype(BF16), wpp_ref[...], preferred_element_type=F32)
    x3 = x2 + pg * proj
    o_ref[...] = _rms(x3, gfin_ref[...])


def _combine(seg_src, seg_len, seg_dst, pos, x1, gates_tk, ys, p2, gple, wpg, wpp, gfin):
    t = x1.shape[0]
    tm = OUT_TM
    ple = p2.shape[1]
    row = lambda i, *_: (i, 0)
    fixed = lambda i, *_: (0, 0)
    grid_spec = pltpu.PrefetchScalarGridSpec(
        num_scalar_prefetch=4,
        grid=(t // tm,),
        in_specs=[
            pl.BlockSpec((tm, D_MODEL), row),
            pl.BlockSpec((tm, TOP_K), row),
            pl.BlockSpec(memory_space=pl.ANY),
            pl.BlockSpec((tm, ple), row),
            pl.BlockSpec((1, D_MODEL), fixed),
            pl.BlockSpec((D_MODEL, D_MODEL), fixed),
            pl.BlockSpec((ple, D_MODEL), fixed),
            pl.BlockSpec((1, D_MODEL), fixed),
        ],
        out_specs=pl.BlockSpec((tm, D_MODEL), row),
        scratch_shapes=[
            pltpu.VMEM((2, TOP_K * tm * ROW_TILES, LANES), F32),
            pltpu.VMEM((TOP_K * tm * ROW_TILES, LANES), F32),
            pltpu.SemaphoreType.DMA((2,)),
        ],
    )
    return pl.pallas_call(
        _combine_kernel,
        grid_spec=grid_spec,
        out_shape=jax.ShapeDtypeStruct((t, D_MODEL), F32),
        compiler_params=pltpu.CompilerParams(
            dimension_semantics=("arbitrary",), vmem_limit_bytes=VMEM_LIMIT),
        name="combine_ple_final",
    )(seg_src, seg_len, seg_dst, pos, x1, gates_tk, ys, p2, gple, wpg, wpp, gfin)


def kernel(x, p, rel_bias, norm_mix_g, w_in, pool_w, pool_scale, w_o_attn, w_o_pool, w_out,
           norm_ffn_g, router_w, router_b, w_gate_up, b_gate_up, w_down, b_down,
           norm_ple_g, w_ple_gate, w_ple_proj, norm_final_g):
    b, s, d = x.shape
    depth = w_in.shape[0]
    t = b * s
    assert d == D_MODEL and s % MOBA_BLOCK == 0 and t % IN_TM == 0 and s % MIX_TM == 0
    assert depth == 1, "the final norm is fused into the layer's last kernel"
    n_pad = t * TOP_K + (N_EXPERTS + GATHER_DEPTH - 2) * EXPERT_TM
    n_blk = n_pad // EXPERT_TM
    tbl, bfar = _t5_bias_tables(rel_bias)

    x2 = x.reshape(t, d)
    for i in range(depth):
        w_in_b = w_in[i].astype(BF16)
        wvt = w_in[i][:, 2 * ATTN_WIDTH:3 * ATTN_WIDTH].T.astype(BF16)
        q, k, vt, u, gl = _in_proj(x2, norm_mix_g[i].reshape(1, d), w_in_b, wvt, b, s)
        bh = b * N_HEADS
        attn_t = _attention(q.reshape(bh, s, HEAD_DIM), k.reshape(bh, s, HEAD_DIM),
                            vt.reshape(bh, V_ROWS, s), tbl, bfar)
        attn = attn_t.reshape(b, ATTN_WIDTH, s)

        x1, hf_rows, idx_kt, rank_kt, gate_kt, cnt = _mix(
            x2, attn, u, gl, pool_w[i].astype(BF16), pool_scale[i].reshape(1, POOL_WIDTH),
            w_o_attn[i].astype(BF16), w_o_pool[i].astype(BF16), w_out[i].astype(BF16),
            norm_ffn_g[i].reshape(1, d), router_w[i].T, router_b[i].reshape(N_EXPERTS, 1), s)

        counts = cnt[:, 0]
        padded = (counts + EXPERT_TM - 1) // EXPERT_TM * EXPERT_TM
        pend = jnp.cumsum(padded)
        pstart = pend - padded
        e_ids = jnp.arange(N_EXPERTS, dtype=I32)
        pstart_of = jnp.sum(jnp.where(idx_kt[..., None] == e_ids, pstart, 0), axis=-1)
        dest_flat = (pstart_of + rank_kt).T.reshape(-1).astype(I32)
        blk_row0 = jnp.arange(n_blk, dtype=I32) * EXPERT_TM
        blk_expert = jnp.minimum(jnp.sum(pend[None, :] <= blk_row0[:, None], axis=1),
                                 N_EXPERTS - 1).astype(I32)
        n_used = (pend[-1:] // EXPERT_TM).astype(I32)
        pad_lo = jnp.concatenate([pstart + counts, pend[-1:]]).astype(I32)
        pad_hi = jnp.concatenate([pend, pend[-1:] + (GATHER_DEPTH - 1) * EXPERT_TM]).astype(I32)

        bgu = b_gate_up[i].reshape(N_EXPERTS, D_MODEL // LANES, LANES, 2).transpose(
            0, 1, 3, 2).reshape(N_EXPERTS, 1, 2 * D_MODEL)
        first = jnp.concatenate([jnp.ones((1,), I32),
                                 (blk_expert[1:] != blk_expert[:-1]).astype(I32)])
        parity = ((jnp.cumsum(first) - 1) % 2).astype(I32)
        later = jnp.logical_and(e_ids[None, :] > e_ids[:, None], (padded > 0)[None, :])
        next_e = jnp.min(jnp.where(later, e_ids[None, :], N_EXPERTS), axis=1)
        next_e = jnp.where(next_e == N_EXPERTS, -1, next_e).astype(I32)
        next_expert = jnp.sum(jnp.where(blk_expert[:, None] == e_ids, next_e, 0),
                              axis=1).astype(I32)
        ys = _experts(blk_expert, n_used, dest_flat, first, parity, next_expert, pad_lo, pad_hi,
                      hf_rows, w_gate_up[i], w_down[i], bgu, b_down[i][:, None, :])

        n_tiles = t // OUT_TM
        tile_cnt = jnp.sum(idx_kt.T.reshape(n_tiles, OUT_TM * TOP_K)[..., None] == e_ids,
                           axis=1).astype(I32)
        before = jnp.cumsum(tile_cnt, axis=0) - tile_cnt
        local_off = jnp.cumsum(tile_cnt, axis=1) - tile_cnt
        seg_src = ((pstart[None, :] + before) * ROW_TILES).reshape(-1).astype(I32)
        seg_len = (tile_cnt * ROW_TILES).reshape(-1).astype(I32)
        seg_dst = (local_off * ROW_TILES).reshape(-1).astype(I32)
        shift = jnp.repeat(local_off - before, OUT_TM, axis=0)
        pos = jnp.sum(jnp.where(idx_kt[..., None] == e_ids, shift[None], 0), axis=-1) + rank_kt
        pos = (pos.T.reshape(-1) * ROW_TILES).astype(I32)
        x2 = _combine(seg_src, seg_len, seg_dst, pos, x1, gate_kt.T, ys, p[i].reshape(t, -1),
                      norm_ple_g[i].reshape(1, d), w_ple_gate[i].astype(BF16),
                      w_ple_proj[i].astype(BF16), norm_final_g.reshape(1, d))
    return x2.reshape(b, s, d)
```
